```python
import jax, jax.numpy as jnp
from jax import lax
import numpy as np

D_MODEL = 1024
BATCH = 16
SEQ = 4096
DEPTH = 1
DEC_BATCH = 8
DEC_SEQ = 32
PAST_LEN = 4096

CHUNK = 64
ATT_HEADS = 8
ATT_HD = 64
ATT_LEFT = 8
ATT_CTX = ATT_LEFT * CHUNK
ATT_BAND = (ATT_LEFT + 1) * CHUNK
MAX_REL = 128
ML_HEADS = 4
ML_DK = 128
ML_DV = 128
PEER_HEADS = 8
PEER_NKEYS = 128
PEER_EXPERTS = PEER_NKEYS * PEER_NKEYS
PEER_DKEY = 128
PEER_TOPK = 16
PEER_BLOCK = 256
EPS = 1e-6
NEG = -1e30

ATT_W = ATT_HEADS * ATT_HD
ML_QK = ML_HEADS * ML_DK
ML_W = ML_HEADS * ML_DV
IN_SIZES = (ATT_W, ATT_W, ATT_W, ML_QK, ML_QK, ML_W, ML_W, ML_HEADS, ML_HEADS, D_MODEL, D_MODEL)
IN_COLS = sum(IN_SIZES)
SPLIT_POINTS = tuple(int(s) for s in np.cumsum(IN_SIZES)[:-1])

kernel_name = 'hybrid_chunk_attn_mlstm_peer_step'


def rmsnorm(x, g):
    xf = x.astype(jnp.float32)
    y = xf * lax.rsqrt(jnp.mean(xf * xf, axis=-1, keepdims=True) + EPS)
    return (y * g.astype(jnp.float32)).astype(x.dtype)


def _project(xn, w_in, b_in, f_bias):
    z = xn @ w_in + b_in
    aq, ak, av, mq, mk, mv, mo, mi, mf, ga, gb = jnp.split(z, SPLIT_POINTS, axis=-1)
    lead = xn.shape[:-1]
    att = tuple(t.reshape(lead + (ATT_HEADS, ATT_HD)) for t in (aq, ak, av))
    f32 = jnp.float32
    mq = mq.reshape(lead + (ML_HEADS, ML_DK)).astype(f32)
    mk = mk.reshape(lead + (ML_HEADS, ML_DK)).astype(f32) * (ML_DK ** -0.5)
    mv = mv.reshape(lead + (ML_HEADS, ML_DV)).astype(f32)
    ig = mi.astype(f32)
    lf = jax.nn.log_sigmoid(mf.astype(f32) + f_bias.astype(f32))
    return att, (mq, mk, mv, ig, lf), mo, ga, gb


def _rel_bias(table, d):
    return jnp.transpose(table[jnp.clip(d, -MAX_REL, MAX_REL) + MAX_REL], (2, 0, 1))


def _attend(q, kb, vb, bias, valid):
    s = jnp.einsum('blhd,bkhd->bhlk', q, kb).astype(jnp.float32) * (ATT_HD ** -0.5) + bias
    s = jnp.where(valid, s, NEG)
    p = jax.nn.softmax(s, axis=-1).astype(vb.dtype)
    return jnp.einsum('bhlk,bkhd->blhd', p, vb)


def chunk_attention_prompt(q, k, v, table):
    B, S = q.shape[:2]
    nc = S // CHUNK
    pad = ((0, 0), (ATT_CTX, 0), (0, 0), (0, 0))
    kp = jnp.pad(k, pad)
    vp = jnp.pad(v, pad)
    qc = jnp.moveaxis(q.reshape(B, nc, CHUNK, ATT_HEADS, ATT_HD), 1, 0)
    d = jnp.arange(CHUNK)[:, None] + ATT_CTX - jnp.arange(ATT_BAND)[None, :]
    bias = _rel_bias(table, d)

    def one_chunk(args):
        c, qb = args
        start = c * CHUNK
        kb = lax.dynamic_slice_in_dim(kp, start, ATT_BAND, axis=1)
        vb = lax.dynamic_slice_in_dim(vp, start, ATT_BAND, axis=1)
        valid = (start + jnp.arange(ATT_BAND)) >= ATT_CTX
        return _attend(qb, kb, vb, bias, valid)

    out = lax.map(one_chunk, (jnp.arange(nc), qc))
    return jnp.moveaxis(out, 0, 1).reshape(B, S, ATT_W)


def chunk_attention_sample(q, k, v, cache_k, cache_v, table):
    B, T = q.shape[:2]
    P = cache_k.shape[1]
    kall = jnp.concatenate([cache_k.astype(k.dtype), k], axis=1)
    vall = jnp.concatenate([cache_v.astype(v.dtype), v], axis=1)
    d = jnp.arange(T)[:, None] + P - jnp.arange(P + T)[None, :]
    bias = _rel_bias(table, d)
    out = _attend(q, kall, vall, bias, jnp.ones((P + T,), dtype=bool))
    return out.reshape(B, T, ATT_W)


def mlstm_chunk(state, q, k, v, ig, lf):
    C, n, m = state
    L = q.shape[2]
    b = jnp.cumsum(lf, axis=-1)
    a = b + m[..., None]
    D = b[..., :, None] - b[..., None, :] + ig[..., None, :]
    D = jnp.where(jnp.tril(jnp.ones((L, L), dtype=bool)), D, NEG)
    m_t = jnp.maximum(a, jnp.max(D, axis=-1))
    w_inter = jnp.exp(a - m_t)
    W = jnp.exp(D - m_t[..., None])
    Wqk = W * jnp.einsum('bhtd,bhsd->bhts', q, k)
    num = w_inter[..., None] * jnp.einsum('bhtd,bhde->bhte', q, C) + jnp.einsum('bhts,bhse->bhte', Wqk, v)
    den = w_inter * jnp.einsum('bhtd,bhd->bht', q, n) + jnp.sum(Wqk, axis=-1)
    h = num / jnp.maximum(jnp.abs(den), jnp.exp(-m_t))[..., None]
    m_new = m_t[..., -1]
    decay = jnp.exp(a[..., -1] - m_new)
    wl = W[..., -1, :]
    C_new = decay[..., None, None] * C + jnp.einsum('bhs,bhsd,bhse->bhde', wl, k, v)
    n_new = decay[..., None] * n + jnp.einsum('bhs,bhsd->bhd', wl, k)
    return (C_new, n_new, m_new), h


def mlstm_prompt(mq, mk, mv, ig, lf):
    B, S = mq.shape[:2]
    nc = S // CHUNK

    def to_chunks(t):
        t = jnp.moveaxis(t.reshape((B, nc, CHUNK) + t.shape[2:]), 1, 0)
        return jnp.swapaxes(t, 2, 3)

    f32 = jnp.float32
    init = (jnp.zeros((B, ML_HEADS, ML_DK, ML_DV), f32), jnp.zeros((B, ML_HEADS, ML_DK), f32),
            jnp.full((B, ML_HEADS), NEG, f32))
    xs = (to_chunks(mq), to_chunks(mk), to_chunks(mv), to_chunks(ig), to_chunks(lf))
    state, h = lax.scan(lambda st, c: mlstm_chunk(st, *c), init, xs)
    h = jnp.swapaxes(jnp.moveaxis(h, 0, 1), 2, 3).reshape(B, S, ML_W)
    return state, h


def mlstm_sample(mq, mk, mv, ig, lf, C, n, m):
    B, T = mq.shape[:2]
    f32 = jnp.float32
    state = (C.astype(f32), n.astype(f32), m.astype(f32))
    sw = lambda t: jnp.swapaxes(t, 1, 2)
    state, h = mlstm_chunk(state, sw(mq), sw(mk), sw(mv), sw(ig), sw(lf))
    return state, jnp.swapaxes(h, 1, 2).reshape(B, T, ML_W)


def _merge(att_out, h, mo, ga, gb, w_att_branch, w_ml_branch, w_out):
    dt = att_out.dtype
    ml_out = (jax.nn.sigmoid(mo.astype(jnp.float32)) * h).astype(dt)
    a = att_out @ w_att_branch
    b = ml_out @ w_ml_branch
    merged = jax.nn.sigmoid(ga) * a + jax.nn.sigmoid(gb) * b
    return merged @ w_out


def peer_ffn(xn, wq, sub_keys, expert_u, expert_v):
    lead = xn.shape[:-1]
    xf = xn.reshape(-1, D_MODEL)
    N = xf.shape[0]
    nblk = -(-N // PEER_BLOCK)
    xb_all = jnp.pad(xf, ((0, nblk * PEER_BLOCK - N), (0, 0))).reshape(nblk, PEER_BLOCK, D_MODEL)

    def block(xb):
        nb = xb.shape[0]
        q = (xb @ wq).reshape(nb, PEER_HEADS, 2, PEER_DKEY // 2)
        s = jnp.einsum('nhpd,hpkd->nhpk', q, sub_keys).astype(jnp.float32)
        sv, si = lax.top_k(s, PEER_TOPK)
        cand = (sv[:, :, 0, :, None] + sv[:, :, 1, None, :]).reshape(nb, PEER_HEADS, PEER_TOPK * PEER_TOPK)
        cidx = (si[:, :, 0, :, None] * PEER_NKEYS + si[:, :, 1, None, :]).reshape(nb, PEER_HEADS, PEER_TOPK * PEER_TOPK)
        fv, fi = lax.top_k(cand, PEER_TOPK)
        eidx = jnp.take_along_axis(cidx, fi, axis=-1)
        g = jax.nn.softmax(fv, axis=-1)
        u = expert_u[eidx]
        vv = expert_v[eidx]
        act = jax.nn.gelu(jnp.einsum('nhkd,nd->nhk', u, xb).astype(jnp.float32), approximate=False)
        return jnp.einsum('nhk,nhkd->nd', (g * act).astype(xb.dtype), vv)

    out = lax.map(block, xb_all).reshape(-1, D_MODEL)[:N]
    return out.reshape(lead + (D_MODEL,))


def setup_inputs(seed: int = 0) -> dict:
    key = jax.random.key(seed)
    ks = jax.random.split(key, 24)
    nrm = lambda k, s, sc: jax.random.normal(k, s, jnp.float32) * sc
    P = min(ATT_CTX, PAST_LEN)
    return {
        'x_prompt': nrm(ks[0], (BATCH, SEQ, D_MODEL), 1.0),
        'x_sample': nrm(ks[1], (DEC_BATCH, DEC_SEQ, D_MODEL), 1.0),
        'cache_att_k': nrm(ks[2], (DEPTH, DEC_BATCH, P, ATT_HEADS, ATT_HD), 1.0),
        'cache_att_v': nrm(ks[3], (DEPTH, DEC_BATCH, P, ATT_HEADS, ATT_HD), 1.0),
        'state_mlstm_C': nrm(ks[4], (DEPTH, DEC_BATCH, ML_HEADS, ML_DK, ML_DV), 0.3),
        'state_mlstm_n': nrm(ks[5], (DEPTH, DEC_BATCH, ML_HEADS, ML_DK), 0.3),
        'state_mlstm_m': nrm(ks[6], (DEPTH, DEC_BATCH, ML_HEADS), 1.0),
        'norm1_g': 1.0 + nrm(ks[7], (DEPTH, D_MODEL), 0.02),
        'w_in': nrm(ks[8], (DEPTH, D_MODEL, IN_COLS), D_MODEL ** -0.5),
        'b_in': nrm(ks[9], (DEPTH, IN_COLS), 0.02),
        'ml_f_bias': jnp.linspace(3.0, 6.0, ML_HEADS, dtype=jnp.float32)[None, :] + nrm(ks[10], (DEPTH, ML_HEADS), 0.1),
        'att_rel_bias': nrm(ks[11], (DEPTH, 2 * MAX_REL + 1, ATT_HEADS), 0.1),
        'w_att_branch': nrm(ks[12], (DEPTH, ATT_W, D_MODEL), ATT_W ** -0.5),
        'w_ml_branch': nrm(ks[13], (DEPTH, ML_W, D_MODEL), ML_W ** -0.5),
        'w_out': nrm(ks[14], (DEPTH, D_MODEL, D_MODEL), D_MODEL ** -0.5),
        'norm2_g': 1.0 + nrm(ks[15], (DEPTH, D_MODEL), 0.02),
        'peer_wq': nrm(ks[16], (DEPTH, D_MODEL, PEER_HEADS * PEER_DKEY), D_MODEL ** -0.5),
        'peer_sub_keys': nrm(ks[17], (DEPTH, PEER_HEADS, 2, PEER_NKEYS, PEER_DKEY // 2), (PEER_DKEY // 2) ** -0.5),
        'peer_u': nrm(ks[18], (DEPTH, PEER_EXPERTS, D_MODEL), D_MODEL ** -0.5),
        'peer_v': nrm(ks[19], (DEPTH, PEER_EXPERTS, D_MODEL), 0.25),
        'final_g': 1.0 + nrm(ks[20], (D_MODEL,), 0.02),
    }


def reference(x_prompt, x_sample, cache_att_k, cache_att_v, state_mlstm_C, state_mlstm_n, state_mlstm_m,
              norm1_g, w_in, b_in, ml_f_bias, att_rel_bias, w_att_branch, w_ml_branch, w_out,
              norm2_g, peer_wq, peer_sub_keys, peer_u, peer_v, final_g):
    xp = x_prompt
    xs = x_sample
    kp_l, vp_l, Cp_l, np_l, mp_l = [], [], [], [], []
    ks_l, vs_l, Cs_l, ns_l, ms_l = [], [], [], [], []
    for l in range(DEPTH):
        xn = rmsnorm(xp, norm1_g[l])
        (aq, ak, av), (mq, mk, mv, ig, lf), mo, ga, gb = _project(xn, w_in[l], b_in[l], ml_f_bias[l])
        att = chunk_attention_prompt(aq, ak, av, att_rel_bias[l])
        (Cn, nn_, mn), h = mlstm_prompt(mq, mk, mv, ig, lf)
        xp = xp + _merge(att, h, mo, ga, gb, w_att_branch[l], w_ml_branch[l], w_out[l])
        xp = xp + peer_ffn(rmsnorm(xp, norm2_g[l]), peer_wq[l], peer_sub_keys[l], peer_u[l], peer_v[l])
        keep = min(ATT_CTX, xp.shape[1])
        kp_l.append(ak[:, -keep:])
        vp_l.append(av[:, -keep:])
        Cp_l.append(Cn)
        np_l.append(nn_)
        mp_l.append(mn)
        xn = rmsnorm(xs, norm1_g[l])
        (aq, ak, av), (mq, mk, mv, ig, lf), mo, ga, gb = _project(xn, w_in[l], b_in[l], ml_f_bias[l])
        att = chunk_attention_sample(aq, ak, av, cache_att_k[l], cache_att_v[l], att_rel_bias[l])
        (Cn, nn_, mn), h = mlstm_sample(mq, mk, mv, ig, lf, state_mlstm_C[l], state_mlstm_n[l], state_mlstm_m[l])
        xs = xs + _merge(att, h, mo, ga, gb, w_att_branch[l], w_ml_branch[l], w_out[l])
        xs = xs + peer_ffn(rmsnorm(xs, norm2_g[l]), peer_wq[l], peer_sub_keys[l], peer_u[l], peer_v[l])
        ks_l.append(ak)
        vs_l.append(av)
        Cs_l.append(Cn)
        ns_l.append(nn_)
        ms_l.append(mn)
    y_prompt = rmsnorm(xp, final_g)
    y_sample = rmsnorm(xs, final_g)
    return (y_prompt, y_sample,
            jnp.stack(kp_l), jnp.stack(vp_l), jnp.stack(Cp_l), jnp.stack(np_l), jnp.stack(mp_l),
            jnp.stack(ks_l), jnp.stack(vs_l), jnp.stack(Cs_l), jnp.stack(ns_l), jnp.stack(ms_l))
```

```python
import functools

import jax
import jax.numpy as jnp
import numpy as np
from jax import lax
from jax.experimental import pallas as pl
from jax.experimental.pallas import tpu as pltpu

F32 = jnp.float32
BF16 = jnp.bfloat16

D_MODEL = 1024
CHUNK = 64
ATT_HEADS = 8
ATT_HD = 64
ATT_W = ATT_HEADS * ATT_HD
ATT_CTX = 8 * CHUNK
MAX_REL = 128
ML_HEADS = 4
ML_DK = 128
ML_W = ML_HEADS * ML_DK
PEER_HEADS = 8
PEER_NKEYS = 128
PEER_DKEY = 128
PEER_TOPK = 16
PEER_PICKS = PEER_HEADS * PEER_TOPK
EPS = 1e-6
NEG = -1e30

LANES = 128
SUBLANES = 8
ROW_TILE = 256
GATE_COLS = LANES
MAIN_COLS = 7 * 512 + 2 * D_MODEL
GROUP = SUBLANES
PEER_BLOCK = 256
GATHER_SLOTS = 2
VMEM_LIMIT = 56 * 1024 * 1024


def _params(*sem):
    return pltpu.CompilerParams(dimension_semantics=sem, vmem_limit_bytes=VMEM_LIMIT)


def _rms(x, g):
    return x * lax.rsqrt(jnp.mean(x * x, axis=-1, keepdims=True) + EPS) * g


def _inproj_kernel(x_ref, g_ref, w_ref, b_ref, wgh_ref, wgl_ref, bg_ref, fb_ref,
                   q_o, k_o, v_o, kf_o, vf_o, mq_o, mk_o, mv_o, mo_o, ga_o, gb_o, gt_o):
    xn = _rms(x_ref[...], g_ref[...])
    xh = xn.astype(BF16)

    def proj(off, width):
        return jnp.dot(xh, w_ref[:, off:off + width], preferred_element_type=F32) + b_ref[:, off:off + width]

    q_o[...] = (proj(0, 512) * (ATT_HD ** -0.5)).astype(BF16)
    k = proj(512, 512)
    kf_o[...] = k
    k_o[...] = k.astype(BF16)
    v = proj(1024, 512)
    vf_o[...] = v
    v_o[...] = v.astype(BF16)
    mq_o[...] = proj(1536, 512).astype(BF16)
    mk_o[...] = (proj(2048, 512) * (ML_DK ** -0.5)).astype(BF16)
    mv_o[...] = proj(2560, 512).astype(BF16)
    mo_o[...] = proj(3072, 512)
    ga_o[...] = proj(3584, D_MODEL)
    gb_o[...] = proj(3584 + D_MODEL, D_MODEL)
    xl = (xn - xh.astype(F32)).astype(BF16)
    zg = (jnp.dot(xh, wgh_ref[...], preferred_element_type=F32)
          + jnp.dot(xh, wgl_ref[...], preferred_element_type=F32)
          + jnp.dot(xl, wgh_ref[...], preferred_element_type=F32)) + bg_ref[...]
    t = zg + fb_ref[...]
    log_sig = jnp.minimum(t, 0.0) - jnp.log1p(jnp.exp(-jnp.abs(t)))
    lane = lax.broadcasted_iota(jnp.int32, zg.shape, 1)
    gt_o[...] = jnp.where(lane < ML_HEADS, zg, log_sig)


def _inproj(x, g1, w_main, b_main, wg_hi, wg_lo, bg, fb):
    n = x.shape[0]
    tm = min(ROW_TILE, n)
    row = lambda w: pl.BlockSpec((tm, w), lambda i: (i, 0))
    full = lambda a: pl.BlockSpec(a.shape, lambda i: (0, 0))
    widths = [(512, BF16)] * 3 + [(512, F32)] * 2 + [(512, BF16)] * 3 + [(512, F32), (D_MODEL, F32), (D_MODEL, F32),
                                                                     (GATE_COLS, F32)]
    return pl.pallas_call(
        _inproj_kernel,
        grid=(n // tm,),
        in_specs=[row(D_MODEL), full(g1), full(w_main), full(b_main), full(wg_hi), full(wg_lo), full(bg), full(fb)],
        out_specs=[row(w) for w, _ in widths],
        out_shape=[jax.ShapeDtypeStruct((n, w), dt) for w, dt in widths],
        compiler_params=_params("parallel"),
        name="inproj",
    )(x, g1, w_main, b_main, wg_hi, wg_lo, bg, fb)


def _attn_kernel(q_ref, k_ref, v_ref, bias_ref, o_ref, *, L, KB, pad):
    c = pl.program_id(1)
    start = pl.multiple_of(c * L, L)
    kb = k_ref[0, pl.ds(start, KB), :]
    vb = v_ref[0, pl.ds(start, KB), :]
    q = q_ref[0].astype(F32)
    col = lax.broadcasted_iota(jnp.int32, (1, KB), 1)
    valid = (start + col) >= pad
    lo = lax.broadcasted_iota(jnp.int32, (L, LANES), 1) < ATT_HD
    for hp in range(ATT_HEADS // 2):
        sl = slice(hp * LANES, (hp + 1) * LANES)
        qp, kp, vp = q[:, sl], kb[:, sl], vb[:, sl]
        outs = []
        for half in range(2):
            qm = jnp.where(lo if half == 0 else jnp.logical_not(lo), qp, 0.0).astype(BF16)
            s = lax.dot_general(qm, kp, (((1,), (1,)), ((), ())), preferred_element_type=F32)
            s = s + bias_ref[hp * 2 + half]
            s = jnp.where(valid, s, NEG)
            e = jnp.exp(s - jnp.max(s, axis=-1, keepdims=True))
            p = e / jnp.sum(e, axis=-1, keepdims=True)
            outs.append(jnp.dot(p.astype(BF16), vp, preferred_element_type=F32))
        o_ref[0, :, sl] = jnp.where(lo, outs[0], outs[1]).astype(BF16)


def _attention(q, k, v, bias, *, L, KB, pad):
    B, S, _ = q.shape
    ktot = k.shape[1]
    return pl.pallas_call(
        functools.partial(_attn_kernel, L=L, KB=KB, pad=pad),
        grid=(B, S // L),
        in_specs=[pl.BlockSpec((1, L, ATT_W), lambda b, c: (b, c, 0)),
                  pl.BlockSpec((1, ktot, ATT_W), lambda b, c: (b, 0, 0)),
                  pl.BlockSpec((1, ktot, ATT_W), lambda b, c: (b, 0, 0)),
                  pl.BlockSpec(bias.shape, lambda b, c: (0, 0, 0))],
        out_specs=pl.BlockSpec((1, L, ATT_W), lambda b, c: (b, c, 0)),
        out_shape=jax.ShapeDtypeStruct((B, S, ATT_W), BF16),
        compiler_params=_params("parallel", "arbitrary"),
        name="attention",
    )(q, k, v, bias)


def _rel_bias(table, L, KB):
    d = np.arange(L)[:, None] + ATT_CTX - np.arange(KB)[None, :]
    return jnp.transpose(table[np.clip(d, -MAX_REL, MAX_REL) + MAX_REL], (2, 0, 1))


def _t128(x):
    rows = x.shape[0]
    if rows < LANES:
        x = jnp.concatenate([x, jnp.zeros((LANES - rows, LANES), x.dtype)], axis=0)
    return x.T[:, :rows]


def _mlstm_kernel(q_ref, k_ref, v_ref, g_ref, c0_ref, n0_ref, m0_ref,
                  h_ref, c_ref, n_ref, m_ref, cs, ns, ms, *, L):
    c = pl.program_id(1)

    @pl.when(c == 0)
    def _():
        cs[...] = c0_ref[0]
        ns[...] = n0_ref[0]
        ms[...] = m0_ref[0]

    gates = g_ref[0]
    row = lax.broadcasted_iota(jnp.int32, gates.shape, 0)
    csum = gates
    sh = 1
    while sh < L:
        csum = csum + jnp.where(row >= sh, pltpu.roll(csum, sh, axis=0), 0.0)
        sh *= 2
    gates_t = _t128(gates)
    csum_t = _t128(csum)
    tril = lax.broadcasted_iota(jnp.int32, (L, L), 0) >= lax.broadcasted_iota(jnp.int32, (L, L), 1)
    for h in range(ML_HEADS):
        sl = slice(h * ML_DK, (h + 1) * ML_DK)
        q, k, v = q_ref[0, :, sl], k_ref[0, :, sl], v_ref[0, :, sl]
        b_col = csum[:, ML_HEADS + h:ML_HEADS + h + 1]
        b_row = csum_t[ML_HEADS + h:ML_HEADS + h + 1, :]
        ig_col = gates[:, h:h + 1]
        ig_row = gates_t[h:h + 1, :]
        m_prev = ms[h:h + 1, 0:1]
        a_col = b_col + m_prev
        dmat = jnp.where(tril, b_col - b_row + ig_row, NEG)
        m_t = jnp.maximum(a_col, jnp.max(dmat, axis=-1, keepdims=True))
        w_inter = jnp.exp(a_col - m_t)
        wmat = jnp.exp(dmat - m_t)
        qk = lax.dot_general(q, k, (((1,), (1,)), ((), ())), preferred_element_type=F32)
        wqk = wmat * qk
        c_old = cs[h]
        n_old = ns[h:h + 1, :]
        num = (w_inter * jnp.dot(q, c_old.astype(BF16), preferred_element_type=F32)
               + jnp.dot(wqk.astype(BF16), v, preferred_element_type=F32))
        den = (w_inter * jnp.sum(q.astype(F32) * n_old, axis=-1, keepdims=True)
               + jnp.sum(wqk, axis=-1, keepdims=True))
        h_ref[0, :, sl] = num / jnp.maximum(jnp.abs(den), jnp.exp(-m_t))
        m_new = m_t[L - 1:L, :]
        decay = jnp.exp(a_col[L - 1:L, :] - m_new)
        wl_col = jnp.exp(b_col[L - 1:L, :] - b_col + ig_col - m_new)
        kw = k.astype(F32) * wl_col
        cs[h] = decay * c_old + jnp.dot(_t128(kw).astype(BF16), v, preferred_element_type=F32)
        ns[h:h + 1, :] = decay * n_old + jnp.sum(kw, axis=0, keepdims=True)
        ms[h:h + 1, :] = jnp.broadcast_to(m_new, (1, ML_DK))

    @pl.when(c == pl.num_programs(1) - 1)
    def _():
        c_ref[0] = cs[...]
        n_ref[0] = ns[...]
        m_ref[0] = ms[...]


def _mlstm(q, k, v, gates, c0, n0, m0, *, L):
    B, S, _ = q.shape
    seq = lambda w: pl.BlockSpec((1, L, w), lambda b, c: (b, c, 0))
    st4 = pl.BlockSpec((1, ML_HEADS, ML_DK, ML_DK), lambda b, c: (b, 0, 0, 0))
    st3 = pl.BlockSpec((1, ML_HEADS, ML_DK), lambda b, c: (b, 0, 0))
    return pl.pallas_call(
        functools.partial(_mlstm_kernel, L=L),
        grid=(B, S // L),
        in_specs=[seq(ML_W), seq(ML_W), seq(ML_W), seq(GATE_COLS), st4, st3, st3],
        out_specs=[seq(ML_W), st4, st3, st3],
        out_shape=[jax.ShapeDtypeStruct((B, S, ML_W), F32),
                   jax.ShapeDtypeStruct((B, ML_HEADS, ML_DK, ML_DK), F32),
                   jax.ShapeDtypeStruct((B, ML_HEADS, ML_DK), F32),
                   jax.ShapeDtypeStruct((B, ML_HEADS, ML_DK), F32)],
        scratch_shapes=[pltpu.VMEM((ML_HEADS, ML_DK, ML_DK), F32),
                        pltpu.VMEM((ML_HEADS, ML_DK), F32),
                        pltpu.VMEM((ML_HEADS, ML_DK), F32)],
        compiler_params=_params("parallel", "arbitrary"),
        name="mlstm",
    )(q, k, v, gates, c0, n0, m0)


def _merge_kernel(x_ref, att_ref, h_ref, mo_ref, ga_ref, gb_ref, wa_ref, wm_ref, wo_ref, o_ref):
    ml = (jax.nn.sigmoid(mo_ref[...]) * h_ref[...]).astype(BF16)
    a = jnp.dot(att_ref[...], wa_ref[...], preferred_element_type=F32)
    b = jnp.dot(ml, wm_ref[...], preferred_element_type=F32)
    merged = jax.nn.sigmoid(ga_ref[...]) * a + jax.nn.sigmoid(gb_ref[...]) * b
    o_ref[...] = x_ref[...] + jnp.dot(merged.astype(BF16), wo_ref[...], preferred_element_type=F32)


def _merge(x, att, h, mo, ga, gb, wa, wm, wo):
    n = x.shape[0]
    tm = min(ROW_TILE, n)
    row = lambda w: pl.BlockSpec((tm, w), lambda i: (i, 0))
    full = lambda a: pl.BlockSpec(a.shape, lambda i: (0, 0))
    return pl.pallas_call(
        _merge_kernel,
        grid=(n // tm,),
        in_specs=[row(D_MODEL), row(ATT_W), row(ML_W), row(ML_W), row(D_MODEL), row(D_MODEL),
                  full(wa), full(wm), full(wo)],
        out_specs=row(D_MODEL),
        out_shape=jax.ShapeDtypeStruct((n, D_MODEL), F32),
        compiler_params=_params("parallel"),
        name="merge",
    )(x, att, h, mo, ga, gb, wa, wm, wo)


def _top16(s, n_rows):
    rows = lax.broadcasted_iota(jnp.int32, s.shape, 0)
    vals, ids = [], []
    for _ in range(PEER_TOPK):
        m = jnp.max(s, axis=0, keepdims=True)
        i = jnp.min(jnp.where(s == m, rows, n_rows), axis=0, keepdims=True)
        vals.append(m)
        ids.append(i)
        s = jnp.where(rows == i, -jnp.inf, s)
    return jnp.concatenate(vals, axis=0), jnp.concatenate(ids, axis=0)


def _route_kernel(x_ref, g_ref, wq_ref, keys_ref, xn_o, idx_o, gw_o, st_ref):
    xn = _rms(x_ref[...], g_ref[...])
    xn_o[...] = xn
    q = jnp.dot(xn.astype(BF16), wq_ref[...], preferred_element_type=F32)
    st_ref[...] = lax.dot_general(keys_ref[...], q.astype(BF16), (((1,), (1,)), ((), ())),
                                  preferred_element_type=F32)
    t = st_ref.shape[1]

    def head(h, carry):
        r0 = pl.multiple_of(h * 2 * PEER_NKEYS, 2 * PEER_NKEYS)
        sv0, si0 = _top16(st_ref[pl.ds(r0, PEER_NKEYS), :], PEER_NKEYS)
        sv1, si1 = _top16(st_ref[pl.ds(r0 + PEER_NKEYS, PEER_NKEYS), :], PEER_NKEYS)
        cand = jnp.concatenate([jnp.broadcast_to(sv0[a:a + 1], (PEER_TOPK, t)) + sv1
                                for a in range(PEER_TOPK)], axis=0)
        cidx = jnp.concatenate([jnp.broadcast_to(si0[a:a + 1], (PEER_TOPK, t)) * PEER_NKEYS + si1
                                for a in range(PEER_TOPK)], axis=0)
        fv, fi = _top16(cand, PEER_TOPK * PEER_TOPK)
        pos = lax.broadcasted_iota(jnp.int32, cand.shape, 0)
        eidx = jnp.concatenate([jnp.max(jnp.where(pos == fi[r:r + 1], cidx, -1), axis=0, keepdims=True)
                                for r in range(PEER_TOPK)], axis=0)
        e = jnp.exp(fv - fv[0:1])
        o0 = pl.multiple_of(h * PEER_TOPK, PEER_TOPK)
        gw_o[pl.ds(o0, PEER_TOPK), :] = e / jnp.sum(e, axis=0, keepdims=True)
        idx_o[pl.ds(o0, PEER_TOPK), :] = eidx
        return carry

    lax.fori_loop(0, PEER_HEADS, head, 0)


def _route(x, g2, wq, keys_bd):
    n = x.shape[0]
    tm = min(ROW_TILE, n)
    full = lambda a: pl.BlockSpec(a.shape, lambda i: (0, 0))
    return pl.pallas_call(
        _route_kernel,
        grid=(n // tm,),
        scratch_shapes=[pltpu.VMEM((PEER_HEADS * 2 * PEER_NKEYS, tm), F32)],
        in_specs=[pl.BlockSpec((tm, D_MODEL), lambda i: (i, 0)), full(g2), full(wq), full(keys_bd)],
        out_specs=[pl.BlockSpec((tm, D_MODEL), lambda i: (i, 0)),
                   pl.BlockSpec((PEER_PICKS, tm), lambda i: (0, i)),
                   pl.BlockSpec((PEER_PICKS, tm), lambda i: (0, i))],
        out_shape=[jax.ShapeDtypeStruct((n, D_MODEL), F32),
                   jax.ShapeDtypeStruct((PEER_PICKS, n), jnp.int32),
                   jax.ShapeDtypeStruct((PEER_PICKS, n), F32)],
        compiler_params=_params("parallel"),
        name="peer_route",
    )(x, g2, wq, keys_bd)


def _apply_kernel(idx_hbm, gwt_ref, xn_ref, x_ref, fg_ref, tab_hbm, y_ref, idx_s, buf, gsem, isem, *, nblk):
    i = pl.program_id(0)
    ngroups = PEER_BLOCK // GROUP
    rows = GROUP * PEER_PICKS
    blk_words = PEER_BLOCK * PEER_PICKS
    assert ngroups % GATHER_SLOTS == 0 and GATHER_SLOTS == 2

    def idx_copy(blk, islot):
        return pltpu.make_async_copy(idx_hbm.at[pl.ds(blk * blk_words, blk_words)],
                                     idx_s.at[pl.ds(islot * blk_words, blk_words)], isem.at[islot])

    def issue_group(word0, slot):
        def body(r, carry):
            e = idx_s[word0 + r]
            pltpu.make_async_copy(tab_hbm.at[pl.ds(e, 1), :], buf.at[slot, pl.ds(r, 1), :], gsem.at[slot]).start()
            return carry

        lax.fori_loop(0, rows, body, 0, unroll=8)

    def wait_group(slot):
        pltpu.make_async_copy(tab_hbm.at[pl.ds(0, rows), :], buf.at[slot], gsem.at[slot]).wait()

    islot = i % 2
    has_next = i + 1 < nblk

    @pl.when(i == 0)
    def _():
        idx_copy(0, 0).start()
        idx_copy(0, 0).wait()
        issue_group(0, 0)

    @pl.when(has_next)
    def _():
        idx_copy(i + 1, 1 - islot).start()

    lane = lax.broadcasted_iota(jnp.int32, (PEER_PICKS, LANES), 1)

    def group_body(g, carry):
        slot = g % GATHER_SLOTS
        last = g + 1 == ngroups

        @pl.when(jnp.logical_and(last, has_next))
        def _():
            idx_copy(i + 1, 1 - islot).wait()

        next_word0 = jnp.where(last, (1 - islot) * blk_words, islot * blk_words + (g + 1) * rows)

        @pl.when(jnp.logical_or(jnp.logical_not(last), has_next))
        def _():
            issue_group(next_word0, 1 - slot)

        wait_group(slot)
        t0 = pl.multiple_of(g * GROUP, GROUP)
        lane0 = pl.multiple_of((t0 // LANES) * LANES, LANES)
        x8 = xn_ref[pl.ds(t0, GROUP), :]
        res8 = x_ref[pl.ds(t0, GROUP), :]
        gw_tile = gwt_ref[:, pl.ds(lane0, LANES)]
        for j in range(GROUP):
            u = buf[slot, j * PEER_PICKS:(j + 1) * PEER_PICKS, 0:D_MODEL]
            v = buf[slot, j * PEER_PICKS:(j + 1) * PEER_PICKS, D_MODEL:2 * D_MODEL]
            act = jnp.sum(u * x8[j:j + 1, :], axis=-1, keepdims=True)
            gelu = 0.5 * act * (1.0 + lax.erf(act * (2.0 ** -0.5)))
            gw_col = jnp.sum(jnp.where(lane == t0 - lane0 + j, gw_tile, 0.0), axis=-1, keepdims=True)
            out = jnp.sum((gw_col * gelu) * v, axis=0, keepdims=True)
            y_ref[pl.ds(t0 + j, 1), :] = _rms(res8[j:j + 1, :] + out, fg_ref[...])
        return carry

    lax.fori_loop(0, ngroups, group_body, 0)


def _apply(idx, gwt, xn, x, fg, table):
    n = x.shape[0]
    nblk = n // PEER_BLOCK
    row = lambda w: pl.BlockSpec((PEER_BLOCK, w), lambda i: (i, 0))
    return pl.pallas_call(
        functools.partial(_apply_kernel, nblk=nblk),
        grid=(nblk,),
        in_specs=[pl.BlockSpec(memory_space=pl.ANY), pl.BlockSpec((PEER_PICKS, PEER_BLOCK), lambda i: (0, i)),
                  row(D_MODEL), row(D_MODEL),
                  pl.BlockSpec(fg.shape, lambda i: (0, 0)), pl.BlockSpec(memory_space=pl.ANY)],
        out_specs=row(D_MODEL),
        out_shape=jax.ShapeDtypeStruct((n, D_MODEL), F32),
        scratch_shapes=[pltpu.SMEM((2 * PEER_BLOCK * PEER_PICKS,), jnp.int32),
                        pltpu.VMEM((GATHER_SLOTS, GROUP * PEER_PICKS, 2 * D_MODEL), F32),
                        pltpu.SemaphoreType.DMA((GATHER_SLOTS,)),
                        pltpu.SemaphoreType.DMA((2,))],
        compiler_params=_params("arbitrary"),
        name="peer_apply",
    )(idx, gwt, xn, x, fg, table)


def _layer(x3, weights, att_kv, att_geom, ml_state, L):
    B, S, _ = x3.shape
    n = B * S
    x = x3.reshape(n, D_MODEL)
    (q, k, v, kf, vf, mq, mk, mv, mo, ga, gb, gates) = _inproj(
        x, weights["g1"], weights["w_main"], weights["b_main"], weights["wg_hi"], weights["wg_lo"],
        weights["bg"], weights["fb"])
    sh = lambda a: a.reshape(B, S, a.shape[-1])
    k_all, v_all = att_kv(sh(k), sh(v))
    att = _attention(sh(q), k_all, v_all, att_geom["bias"], L=L, KB=att_geom["KB"], pad=att_geom["pad"])
    h, c_new, n_new, m_new = _mlstm(sh(mq), sh(mk), sh(mv), sh(gates), *ml_state, L=L)
    x2 = _merge(x, att.reshape(n, ATT_W), h.reshape(n, ML_W), mo, ga, gb,
                weights["wa"], weights["wm"], weights["wo"])
    xn2, idx_t, gw_t = _route(x2, weights["g2"], weights["wq"], weights["keys_bd"])
    y = _apply(idx_t.T.reshape(n * PEER_PICKS), gw_t, xn2, x2, weights["fg"], weights["table"])
    return (y.reshape(B, S, D_MODEL), sh(kf).reshape(B, S, ATT_HEADS, ATT_HD), sh(vf).reshape(B, S, ATT_HEADS, ATT_HD),
            c_new, n_new, m_new[:, :, 0])


def kernel(x_prompt, x_sample, cache_att_k, cache_att_v, state_mlstm_C, state_mlstm_n, state_mlstm_m, norm1_g, w_in, b_in, ml_f_bias, att_rel_bias, w_att_branch, w_ml_branch, w_out, norm2_g, peer_wq, peer_sub_keys, peer_u, peer_v, final_g):
    depth = w_in.shape[0]
    assert depth == 1, "single-layer step"
    l = 0
    w = w_in[l]
    b = b_in[l]
    gate_lo, gate_hi = 7 * 512, 7 * 512 + 2 * ML_HEADS
    w_main = jnp.concatenate([w[:, :gate_lo], w[:, gate_hi:]], axis=1).astype(BF16)
    b_main = jnp.concatenate([b[:gate_lo], b[gate_hi:]])[None, :]
    wg = jnp.pad(w[:, gate_lo:gate_hi], ((0, 0), (0, GATE_COLS - 2 * ML_HEADS)))
    wg_hi = wg.astype(BF16)
    wg_lo = (wg - wg_hi.astype(F32)).astype(BF16)
    bg = jnp.pad(b[gate_lo:gate_hi], (0, GATE_COLS - 2 * ML_HEADS))[None, :]
    fb = jnp.pad(ml_f_bias[l], (ML_HEADS, GATE_COLS - 2 * ML_HEADS))[None, :]
    sk = peer_sub_keys[l].reshape(PEER_HEADS * 2, PEER_NKEYS, PEER_DKEY // 2)
    eye = jnp.eye(PEER_HEADS * 2, dtype=F32)
    keys_bd = (sk[:, :, None, :] * eye[:, None, :, None]).reshape(PEER_HEADS * 2 * PEER_NKEYS, D_MODEL).astype(BF16)
    weights = dict(
        g1=norm1_g[l][None, :], w_main=w_main, b_main=b_main, wg_hi=wg_hi, wg_lo=wg_lo, bg=bg, fb=fb,
        wa=w_att_branch[l].astype(BF16), wm=w_ml_branch[l].astype(BF16), wo=w_out[l].astype(BF16),
        g2=norm2_g[l][None, :], wq=peer_wq[l].astype(BF16), keys_bd=keys_bd, fg=final_g[None, :],
        table=jnp.concatenate([peer_u[l], peer_v[l]], axis=1))

    Bs, T, _ = x_sample.shape
    P = cache_att_k.shape[2]
    assert P == ATT_CTX
    ck = cache_att_k[l].reshape(Bs, P, ATT_W).astype(BF16)
    cv = cache_att_v[l].reshape(Bs, P, ATT_W).astype(BF16)
    cat_kv = lambda k, v: (jnp.concatenate([ck, k], axis=1), jnp.concatenate([cv, v], axis=1))
    geom_s = dict(bias=_rel_bias(att_rel_bias[l], T, P + T), KB=P + T, pad=0)
    state_s = (state_mlstm_C[l], state_mlstm_n[l],
               jnp.broadcast_to(state_mlstm_m[l][:, :, None], (Bs, ML_HEADS, ML_DK)))
    y_s, k_s, v_s, c_s, n_s, m_s = _layer(x_sample, weights, cat_kv, geom_s, state_s, T)

    B, S, _ = x_prompt.shape
    pad_kv = lambda k, v: (jnp.pad(k, ((0, 0), (ATT_CTX, 0), (0, 0))), jnp.pad(v, ((0, 0), (ATT_CTX, 0), (0, 0))))
    geom_p = dict(bias=_rel_bias(att_rel_bias[l], CHUNK, ATT_CTX + CHUNK), KB=ATT_CTX + CHUNK, pad=ATT_CTX)
    state_p = (jnp.zeros((B, ML_HEADS, ML_DK, ML_DK), F32), jnp.zeros((B, ML_HEADS, ML_DK), F32),
               jnp.full((B, ML_HEADS, ML_DK), NEG, F32))
    y_p, k_p, v_p, c_p, n_p, m_p = _layer(x_prompt, weights, pad_kv, geom_p, state_p, CHUNK)
    keep = min(ATT_CTX, S)

    st = lambda a: a[None]
    return (y_p, y_s, st(k_p[:, -keep:]), st(v_p[:, -keep:]), st(c_p), st(n_p), st(m_p),
            st(k_s), st(v_s), st(c_s), st(n_s), st(m_s))
```

```python
import functools

import jax
import jax.numpy as jnp
import numpy as np
from jax import lax
from jax.experimental import pallas as pl
from jax.experimental.pallas import tpu as pltpu

F32 = jnp.float32
BF16 = jnp.bfloat16

D_MODEL = 1024
CHUNK = 64
ATT_HEADS = 8
ATT_HD = 64
ATT_W = ATT_HEADS * ATT_HD
ATT_CTX = 8 * CHUNK
MAX_REL = 128
ML_HEADS = 4
ML_DK = 128
ML_W = ML_HEADS * ML_DK
PEER_HEADS = 8
PEER_NKEYS = 128
PEER_DKEY = 128
PEER_TOPK = 16
PEER_PICKS = PEER_HEADS * PEER_TOPK
EPS = 1e-6
NEG = -1e30

LANES = 128
SUBLANES = 8
ROW_TILE = 256
GATE_COLS = LANES
MAIN_COLS = 7 * 512 + 2 * D_MODEL
GROUP = SUBLANES
PEER_BLOCK = 256
GATHER_SLOTS = 2
EXPERT_ROWS = 2 * D_MODEL // LANES
VMEM_LIMIT = 56 * 1024 * 1024


def _params(*sem):
    return pltpu.CompilerParams(dimension_semantics=sem, vmem_limit_bytes=VMEM_LIMIT)


def _rms(x, g):
    return x * lax.rsqrt(jnp.mean(x * x, axis=-1, keepdims=True) + EPS) * g


def _inproj_kernel(x_ref, g_ref, w_ref, b_ref, wgh_ref, wgl_ref, bg_ref, fb_ref,
                   q_o, k_o, v_o, kf_o, vf_o, mq_o, mk_o, mv_o, mo_o, ga_o, gb_o, gt_o):
    xn = _rms(x_ref[...], g_ref[...])
    xh = xn.astype(BF16)

    def proj(off, width):
        return jnp.dot(xh, w_ref[:, off:off + width], preferred_element_type=F32) + b_ref[:, off:off + width]

    q_o[...] = (proj(0, 512) * (ATT_HD ** -0.5)).astype(BF16)
    k = proj(512, 512)
    kf_o[...] = k
    k_o[...] = k.astype(BF16)
    v = proj(1024, 512)
    vf_o[...] = v
    v_o[...] = v.astype(BF16)
    mq_o[...] = proj(1536, 512).astype(BF16)
    mk_o[...] = (proj(2048, 512) * (ML_DK ** -0.5)).astype(BF16)
    mv_o[...] = proj(2560, 512).astype(BF16)
    mo_o[...] = proj(3072, 512)
    ga_o[...] = proj(3584, D_MODEL)
    gb_o[...] = proj(3584 + D_MODEL, D_MODEL)
    xl = (xn - xh.astype(F32)).astype(BF16)
    zg = (jnp.dot(xh, wgh_ref[...], preferred_element_type=F32)
          + jnp.dot(xh, wgl_ref[...], preferred_element_type=F32)
          + jnp.dot(xl, wgh_ref[...], preferred_element_type=F32)) + bg_ref[...]
    t = zg + fb_ref[...]
    log_sig = jnp.minimum(t, 0.0) - jnp.log1p(jnp.exp(-jnp.abs(t)))
    lane = lax.broadcasted_iota(jnp.int32, zg.shape, 1)
    gt_o[...] = jnp.where(lane < ML_HEADS, zg, log_sig)


def _inproj(x, g1, w_main, b_main, wg_hi, wg_lo, bg, fb):
    n = x.shape[0]
    tm = min(ROW_TILE, n)
    row = lambda w: pl.BlockSpec((tm, w), lambda i: (i, 0))
    full = lambda a: pl.BlockSpec(a.shape, lambda i: (0, 0))
    widths = [(512, BF16)] * 3 + [(512, F32)] * 2 + [(512, BF16)] * 3 + [(512, F32), (D_MODEL, F32), (D_MODEL, F32),
                                                                     (GATE_COLS, F32)]
    return pl.pallas_call(
        _inproj_kernel,
        grid=(n // tm,),
        in_specs=[row(D_MODEL), full(g1), full(w_main), full(b_main), full(wg_hi), full(wg_lo), full(bg), full(fb)],
        out_specs=[row(w) for w, _ in widths],
        out_shape=[jax.ShapeDtypeStruct((n, w), dt) for w, dt in widths],
        compiler_params=_params("parallel"),
        name="inproj",
    )(x, g1, w_main, b_main, wg_hi, wg_lo, bg, fb)


def _attn_kernel(q_ref, k_ref, v_ref, bias_ref, o_ref, *, L, KB, pad):
    c = pl.program_id(1)
    start = pl.multiple_of(c * L, L)
    kb = k_ref[0, pl.ds(start, KB), :]
    vb = v_ref[0, pl.ds(start, KB), :]
    q = q_ref[0].astype(F32)
    col = lax.broadcasted_iota(jnp.int32, (1, KB), 1)
    valid = (start + col) >= pad
    lo = lax.broadcasted_iota(jnp.int32, (L, LANES), 1) < ATT_HD
    for hp in range(ATT_HEADS // 2):
        sl = slice(hp * LANES, (hp + 1) * LANES)
        qp, kp, vp = q[:, sl], kb[:, sl], vb[:, sl]
        outs = []
        for half in range(2):
            qm = jnp.where(lo if half == 0 else jnp.logical_not(lo), qp, 0.0).astype(BF16)
            s = lax.dot_general(qm, kp, (((1,), (1,)), ((), ())), preferred_element_type=F32)
            s = s + bias_ref[hp * 2 + half]
            s = jnp.where(valid, s, NEG)
            e = jnp.exp(s - jnp.max(s, axis=-1, keepdims=True))
            p = e / jnp.sum(e, axis=-1, keepdims=True)
            outs.append(jnp.dot(p.astype(BF16), vp, preferred_element_type=F32))
        o_ref[0, :, sl] = jnp.where(lo, outs[0], outs[1]).astype(BF16)


def _attention(q, k, v, bias, *, L, KB, pad):
    B, S, _ = q.shape
    ktot = k.shape[1]
    return pl.pallas_call(
        functools.partial(_attn_kernel, L=L, KB=KB, pad=pad),
        grid=(B, S // L),
        in_specs=[pl.BlockSpec((1, L, ATT_W), lambda b, c: (b, c, 0)),
                  pl.BlockSpec((1, ktot, ATT_W), lambda b, c: (b, 0, 0)),
                  pl.BlockSpec((1, ktot, ATT_W), lambda b, c: (b, 0, 0)),
                  pl.BlockSpec(bias.shape, lambda b, c: (0, 0, 0))],
        out_specs=pl.BlockSpec((1, L, ATT_W), lambda b, c: (b, c, 0)),
        out_shape=jax.ShapeDtypeStruct((B, S, ATT_W), BF16),
        compiler_params=_params("parallel", "arbitrary"),
        name="attention",
    )(q, k, v, bias)


def _rel_bias(table, L, KB):
    d = np.arange(L)[:, None] + ATT_CTX - np.arange(KB)[None, :]
    return jnp.transpose(table[np.clip(d, -MAX_REL, MAX_REL) + MAX_REL], (2, 0, 1))


def _t128(x):
    rows = x.shape[0]
    if rows < LANES:
        x = jnp.concatenate([x, jnp.zeros((LANES - rows, LANES), x.dtype)], axis=0)
    return x.T[:, :rows]


def _mlstm_kernel(q_ref, k_ref, v_ref, g_ref, c0_ref, n0_ref, m0_ref,
                  h_ref, c_ref, n_ref, m_ref, cs, ns, ms, *, L):
    c = pl.program_id(1)

    @pl.when(c == 0)
    def _():
        cs[...] = c0_ref[0]
        ns[...] = n0_ref[0]
        ms[...] = m0_ref[0]

    gates = g_ref[0]
    row = lax.broadcasted_iota(jnp.int32, gates.shape, 0)
    csum = gates
    sh = 1
    while sh < L:
        csum = csum + jnp.where(row >= sh, pltpu.roll(csum, sh, axis=0), 0.0)
        sh *= 2
    gates_t = _t128(gates)
    csum_t = _t128(csum)
    tril = lax.broadcasted_iota(jnp.int32, (L, L), 0) >= lax.broadcasted_iota(jnp.int32, (L, L), 1)
    for h in range(ML_HEADS):
        sl = slice(h * ML_DK, (h + 1) * ML_DK)
        q, k, v = q_ref[0, :, sl], k_ref[0, :, sl], v_ref[0, :, sl]
        b_col = csum[:, ML_HEADS + h:ML_HEADS + h + 1]
        b_row = csum_t[ML_HEADS + h:ML_HEADS + h + 1, :]
        ig_col = gates[:, h:h + 1]
        ig_row = gates_t[h:h + 1, :]
        m_prev = ms[h:h + 1, 0:1]
        a_col = b_col + m_prev
        dmat = jnp.where(tril, b_col - b_row + ig_row, NEG)
        m_t = jnp.maximum(a_col, jnp.max(dmat, axis=-1, keepdims=True))
        w_inter = jnp.exp(a_col - m_t)
        wmat = jnp.exp(dmat - m_t)
        qk = lax.dot_general(q, k, (((1,), (1,)), ((), ())), preferred_element_type=F32)
        wqk = wmat * qk
        c_old = cs[h]
        n_old = ns[h:h + 1, :]
        num = (w_inter * jnp.dot(q, c_old.astype(BF16), preferred_element_type=F32)
               + jnp.dot(wqk.astype(BF16), v, preferred_element_type=F32))
        den = (w_inter * jnp.sum(q.astype(F32) * n_old, axis=-1, keepdims=True)
               + jnp.sum(wqk, axis=-1, keepdims=True))
        h_ref[0, :, sl] = num / jnp.maximum(jnp.abs(den), jnp.exp(-m_t))
        m_new = m_t[L - 1:L, :]
        decay = jnp.exp(a_col[L - 1:L, :] - m_new)
        wl_col = jnp.exp(b_col[L - 1:L, :] - b_col + ig_col - m_new)
        kw = k.astype(F32) * wl_col
        cs[h] = decay * c_old + jnp.dot(_t128(kw).astype(BF16), v, preferred_element_type=F32)
        ns[h:h + 1, :] = decay * n_old + jnp.sum(kw, axis=0, keepdims=True)
        ms[h:h + 1, :] = jnp.broadcast_to(m_new, (1, ML_DK))

    @pl.when(c == pl.num_programs(1) - 1)
    def _():
        c_ref[0] = cs[...]
        n_ref[0] = ns[...]
        m_ref[0] = ms[...]


def _mlstm(q, k, v, gates, c0, n0, m0, *, L):
    B, S, _ = q.shape
    seq = lambda w: pl.BlockSpec((1, L, w), lambda b, c: (b, c, 0))
    st4 = pl.BlockSpec((1, ML_HEADS, ML_DK, ML_DK), lambda b, c: (b, 0, 0, 0))
    st3 = pl.BlockSpec((1, ML_HEADS, ML_DK), lambda b, c: (b, 0, 0))
    return pl.pallas_call(
        functools.partial(_mlstm_kernel, L=L),
        grid=(B, S // L),
        in_specs=[seq(ML_W), seq(ML_W), seq(ML_W), seq(GATE_COLS), st4, st3, st3],
        out_specs=[seq(ML_W), st4, st3, st3],
        out_shape=[jax.ShapeDtypeStruct((B, S, ML_W), F32),
                   jax.ShapeDtypeStruct((B, ML_HEADS, ML_DK, ML_DK), F32),
                   jax.ShapeDtypeStruct((B, ML_HEADS, ML_DK), F32),
                   jax.ShapeDtypeStruct((B, ML_HEADS, ML_DK), F32)],
        scratch_shapes=[pltpu.VMEM((ML_HEADS, ML_DK, ML_DK), F32),
                        pltpu.VMEM((ML_HEADS, ML_DK), F32),
                        pltpu.VMEM((ML_HEADS, ML_DK), F32)],
        compiler_params=_params("parallel", "arbitrary"),
        name="mlstm",
    )(q, k, v, gates, c0, n0, m0)


def _merge_kernel(x_ref, att_ref, h_ref, mo_ref, ga_ref, gb_ref, wa_ref, wm_ref, wo_ref, o_ref):
    ml = (jax.nn.sigmoid(mo_ref[...]) * h_ref[...]).astype(BF16)
    a = jnp.dot(att_ref[...], wa_ref[...], preferred_element_type=F32)
    b = jnp.dot(ml, wm_ref[...], preferred_element_type=F32)
    merged = jax.nn.sigmoid(ga_ref[...]) * a + jax.nn.sigmoid(gb_ref[...]) * b
    o_ref[...] = x_ref[...] + jnp.dot(merged.astype(BF16), wo_ref[...], preferred_element_type=F32)


def _merge(x, att, h, mo, ga, gb, wa, wm, wo):
    n = x.shape[0]
    tm = min(ROW_TILE, n)
    row = lambda w: pl.BlockSpec((tm, w), lambda i: (i, 0))
    full = lambda a: pl.BlockSpec(a.shape, lambda i: (0, 0))
    return pl.pallas_call(
        _merge_kernel,
        grid=(n // tm,),
        in_specs=[row(D_MODEL), row(ATT_W), row(ML_W), row(ML_W), row(D_MODEL), row(D_MODEL),
                  full(wa), full(wm), full(wo)],
        out_specs=row(D_MODEL),
        out_shape=jax.ShapeDtypeStruct((n, D_MODEL), F32),
        compiler_params=_params("parallel"),
        name="merge",
    )(x, att, h, mo, ga, gb, wa, wm, wo)


def _top16(s, n_rows):
    rows = lax.broadcasted_iota(jnp.int32, s.shape, 0)
    vals, ids = [], []
    for _ in range(PEER_TOPK):
        m = jnp.max(s, axis=0, keepdims=True)
        i = jnp.min(jnp.where(s == m, rows, n_rows), axis=0, keepdims=True)
        vals.append(m)
        ids.append(i)
        s = jnp.where(rows == i, -jnp.inf, s)
    return jnp.concatenate(vals, axis=0), jnp.concatenate(ids, axis=0)


def _route_kernel(x_ref, g_ref, wq_ref, keys_ref, xn_o, idx_o, gw_o, st_ref):
    xn = _rms(x_ref[...], g_ref[...])
    xn_o[...] = xn
    q = jnp.dot(xn.astype(BF16), wq_ref[...], preferred_element_type=F32)
    st_ref[...] = lax.dot_general(keys_ref[...], q.astype(BF16), (((1,), (1,)), ((), ())),
                                  preferred_element_type=F32)
    t = st_ref.shape[1]

    def head(h, carry):
        r0 = pl.multiple_of(h * 2 * PEER_NKEYS, 2 * PEER_NKEYS)
        sv0, si0 = _top16(st_ref[pl.ds(r0, PEER_NKEYS), :], PEER_NKEYS)
        sv1, si1 = _top16(st_ref[pl.ds(r0 + PEER_NKEYS, PEER_NKEYS), :], PEER_NKEYS)
        cand = jnp.concatenate([jnp.broadcast_to(sv0[a:a + 1], (PEER_TOPK, t)) + sv1
                                for a in range(PEER_TOPK)], axis=0)
        cidx = jnp.concatenate([jnp.broadcast_to(si0[a:a + 1], (PEER_TOPK, t)) * PEER_NKEYS + si1
                                for a in range(PEER_TOPK)], axis=0)
        fv, fi = _top16(cand, PEER_TOPK * PEER_TOPK)
        pos = lax.broadcasted_iota(jnp.int32, cand.shape, 0)
        eidx = jnp.concatenate([jnp.max(jnp.where(pos == fi[r:r + 1], cidx, -1), axis=0, keepdims=True)
                                for r in range(PEER_TOPK)], axis=0)
        e = jnp.exp(fv - fv[0:1])
        o0 = pl.multiple_of(h * PEER_TOPK, PEER_TOPK)
        gw_o[pl.ds(o0, PEER_TOPK), :] = e / jnp.sum(e, axis=0, keepdims=True)
        idx_o[pl.ds(o0, PEER_TOPK), :] = eidx
        return carry

    lax.fori_loop(0, PEER_HEADS, head, 0)


def _route(x, g2, wq, keys_bd):
    n = x.shape[0]
    tm = min(ROW_TILE, n)
    full = lambda a: pl.BlockSpec(a.shape, lambda i: (0, 0))
    return pl.pallas_call(
        _route_kernel,
        grid=(n // tm,),
        scratch_shapes=[pltpu.VMEM((PEER_HEADS * 2 * PEER_NKEYS, tm), F32)],
        in_specs=[pl.BlockSpec((tm, D_MODEL), lambda i: (i, 0)), full(g2), full(wq), full(keys_bd)],
        out_specs=[pl.BlockSpec((tm, D_MODEL), lambda i: (i, 0)),
                   pl.BlockSpec((PEER_PICKS, tm), lambda i: (0, i)),
                   pl.BlockSpec((PEER_PICKS, tm), lambda i: (0, i))],
        out_shape=[jax.ShapeDtypeStruct((n, D_MODEL), F32),
                   jax.ShapeDtypeStruct((PEER_PICKS, n), jnp.int32),
                   jax.ShapeDtypeStruct((PEER_PICKS, n), F32)],
        compiler_params=_params("parallel"),
        name="peer_route",
    )(x, g2, wq, keys_bd)


def _apply_kernel(idx_hbm, gwt_ref, xn_ref, x_ref, fg_ref, tab_hbm, y_ref, idx_s, buf, gsem, isem, coef_ref, *, nblk):
    i = pl.program_id(0)
    ngroups = PEER_BLOCK // GROUP
    rows = GROUP * PEER_PICKS
    blk_words = PEER_BLOCK * PEER_PICKS
    assert ngroups % GATHER_SLOTS == 0 and GATHER_SLOTS == 2

    def idx_copy(blk, islot):
        return pltpu.make_async_copy(idx_hbm.at[pl.ds(blk * blk_words, blk_words)],
                                     idx_s.at[pl.ds(islot * blk_words, blk_words)], isem.at[islot])

    def row_copy(word, slot, r):
        e = idx_s[word]
        return pltpu.make_async_copy(tab_hbm.at[pl.ds(pl.multiple_of(e * EXPERT_ROWS, EXPERT_ROWS), EXPERT_ROWS), :],
                                     buf.at[slot, pl.ds(r * EXPERT_ROWS, EXPERT_ROWS), :], gsem.at[slot])

    def wait_group(slot):
        pltpu.make_async_copy(tab_hbm.at[pl.ds(0, rows * EXPERT_ROWS), :], buf.at[slot], gsem.at[slot]).wait()

    islot = i % 2
    has_next = i + 1 < nblk

    @pl.when(i == 0)
    def _():
        idx_copy(0, 0).start()
        idx_copy(0, 0).wait()

        def body(r, carry):
            row_copy(r, 0, r).start()
            return carry

        lax.fori_loop(0, rows, body, 0, unroll=8)

    @pl.when(has_next)
    def _():
        idx_copy(i + 1, 1 - islot).start()

    lane = lax.broadcasted_iota(jnp.int32, (PEER_PICKS, LANES), 1)
    sub = lax.broadcasted_iota(jnp.int32, (SUBLANES, LANES), 0)
    masks = {1: (sub & 1) == 0, 2: (sub & 2) == 0, 4: (sub & 4) == 0}

    def fold(a, b, h):
        return jnp.where(masks[h], a, b) + pltpu.roll(jnp.where(masks[h], b, a), h, axis=0)

    def row_sums(ps):
        c = [fold(ps[2 * m], ps[2 * m + 1], 1) for m in range(4)]
        d = [fold(c[0], c[1], 2), fold(c[2], c[3], 2)]
        return fold(d[0], d[1], 4)

    def group_body(g, carry):
        slot = g % GATHER_SLOTS
        nslot = 1 - slot
        last = g + 1 == ngroups

        @pl.when(jnp.logical_and(last, has_next))
        def _():
            idx_copy(i + 1, 1 - islot).wait()

        next_word0 = jnp.where(last, jnp.where(has_next, 1 - islot, islot) * blk_words,
                               islot * blk_words + (g + 1) * rows)
        wait_group(slot)
        t0 = pl.multiple_of(g * GROUP, GROUP)
        lane0 = pl.multiple_of((t0 // LANES) * LANES, LANES)
        gw_tile = gwt_ref[:, pl.ds(lane0, LANES)]
        for j in range(GROUP):
            for k in range(PEER_PICKS):
                row_copy(next_word0 + j * PEER_PICKS + k, nslot, j * PEER_PICKS + k).start()
            tok = pl.multiple_of((t0 + j) * SUBLANES, SUBLANES)
            x = xn_ref[pl.ds(tok, SUBLANES), :]
            sums = []
            for c in range(PEER_PICKS // SUBLANES):
                r0 = (j * PEER_PICKS + c * SUBLANES) * EXPERT_ROWS
                sums.append(row_sums([buf[slot, r0 + k * EXPERT_ROWS:r0 + k * EXPERT_ROWS + SUBLANES, :] * x
                                      for k in range(SUBLANES)]))
            act = jnp.sum(jnp.concatenate(sums, axis=0), axis=-1, keepdims=True)
            gelu = 0.5 * act * (1.0 + lax.erf(act * (2.0 ** -0.5)))
            gw_col = jnp.sum(jnp.where(lane == t0 - lane0 + j, gw_tile, 0.0), axis=-1, keepdims=True)
            coef_ref[...] = jnp.broadcast_to(gw_col * gelu, (PEER_PICKS, LANES))
            accs = [jnp.zeros((SUBLANES, LANES), F32) for _ in range(4)]
            for k in range(PEER_PICKS):
                r0 = (j * PEER_PICKS + k) * EXPERT_ROWS + SUBLANES
                accs[k % 4] = accs[k % 4] + coef_ref[k:k + 1, :] * buf[slot, r0:r0 + SUBLANES, :]
            y = x_ref[pl.ds(tok, SUBLANES), :] + ((accs[0] + accs[1]) + (accs[2] + accs[3]))
            ms = jnp.sum(jnp.sum(y * y, axis=0, keepdims=True), axis=-1, keepdims=True) * (1.0 / D_MODEL)
            y_ref[pl.ds(tok, SUBLANES), :] = y * lax.rsqrt(ms + EPS) * fg_ref[...]
        return carry

    lax.fori_loop(0, ngroups, group_body, 0)

    @pl.when(jnp.logical_not(has_next))
    def _():
        wait_group(0)


def _apply(idx, gwt, xn, x, fg, table):
    n = x.shape[0] // SUBLANES
    nblk = n // PEER_BLOCK
    tile = pl.BlockSpec((PEER_BLOCK * SUBLANES, LANES), lambda i: (i, 0))
    return pl.pallas_call(
        functools.partial(_apply_kernel, nblk=nblk),
        grid=(nblk,),
        in_specs=[pl.BlockSpec(memory_space=pl.ANY), pl.BlockSpec((PEER_PICKS, PEER_BLOCK), lambda i: (0, i)),
                  tile, tile,
                  pl.BlockSpec(fg.shape, lambda i: (0, 0)), pl.BlockSpec(memory_space=pl.ANY)],
        out_specs=tile,
        out_shape=jax.ShapeDtypeStruct((n * SUBLANES, LANES), F32),
        scratch_shapes=[pltpu.SMEM((2 * PEER_BLOCK * PEER_PICKS,), jnp.int32),
                        pltpu.VMEM((GATHER_SLOTS, GROUP * PEER_PICKS * EXPERT_ROWS, LANES), F32),
                        pltpu.SemaphoreType.DMA((GATHER_SLOTS,)),
                        pltpu.SemaphoreType.DMA((2,)),
                        pltpu.VMEM((PEER_PICKS, LANES), F32)],
        compiler_params=_params("arbitrary"),
        name="peer_apply",
    )(idx, gwt, xn, x, fg, table)


def _layer(x3, weights, att_kv, att_geom, ml_state, L):
    B, S, _ = x3.shape
    n = B * S
    x = x3.reshape(n, D_MODEL)
    (q, k, v, kf, vf, mq, mk, mv, mo, ga, gb, gates) = _inproj(
        x, weights["g1"], weights["w_main"], weights["b_main"], weights["wg_hi"], weights["wg_lo"],
        weights["bg"], weights["fb"])
    sh = lambda a: a.reshape(B, S, a.shape[-1])
    k_all, v_all = att_kv(sh(k), sh(v))
    att = _attention(sh(q), k_all, v_all, att_geom["bias"], L=L, KB=att_geom["KB"], pad=att_geom["pad"])
    h, c_new, n_new, m_new = _mlstm(sh(mq), sh(mk), sh(mv), sh(gates), *ml_state, L=L)
    x2 = _merge(x, att.reshape(n, ATT_W), h.reshape(n, ML_W), mo, ga, gb,
                weights["wa"], weights["wm"], weights["wo"])
    xn2, idx_t, gw_t = _route(x2, weights["g2"], weights["wq"], weights["keys_bd"])
    tiles = lambda a: a.reshape(n * SUBLANES, LANES)
    y = _apply(idx_t.T.reshape(n * PEER_PICKS), gw_t, tiles(xn2), tiles(x2), weights["fg"].reshape(SUBLANES, LANES),
               weights["table"])
    return (y.reshape(B, S, D_MODEL), sh(kf).reshape(B, S, ATT_HEADS, ATT_HD), sh(vf).reshape(B, S, ATT_HEADS, ATT_HD),
            c_new, n_new, m_new[:, :, 0])


def kernel(x_prompt, x_sample, cache_att_k, cache_att_v, state_mlstm_C, state_mlstm_n, state_mlstm_m, norm1_g, w_in, b_in, ml_f_bias, att_rel_bias, w_att_branch, w_ml_branch, w_out, norm2_g, peer_wq, peer_sub_keys, peer_u, peer_v, final_g):
    depth = w_in.shape[0]
    assert depth == 1, "single-layer step"
    l = 0
    w = w_in[l]
    b = b_in[l]
    gate_lo, gate_hi = 7 * 512, 7 * 512 + 2 * ML_HEADS
    w_main = jnp.concatenate([w[:, :gate_lo], w[:, gate_hi:]], axis=1).astype(BF16)
    b_main = jnp.concatenate([b[:gate_lo], b[gate_hi:]])[None, :]
    wg = jnp.pad(w[:, gate_lo:gate_hi], ((0, 0), (0, GATE_COLS - 2 * ML_HEADS)))
    wg_hi = wg.astype(BF16)
    wg_lo = (wg - wg_hi.astype(F32)).astype(BF16)
    bg = jnp.pad(b[gate_lo:gate_hi], (0, GATE_COLS - 2 * ML_HEADS))[None, :]
    fb = jnp.pad(ml_f_bias[l], (ML_HEADS, GATE_COLS - 2 * ML_HEADS))[None, :]
    sk = peer_sub_keys[l].reshape(PEER_HEADS * 2, PEER_NKEYS, PEER_DKEY // 2)
    eye = jnp.eye(PEER_HEADS * 2, dtype=F32)
    keys_bd = (sk[:, :, None, :] * eye[:, None, :, None]).reshape(PEER_HEADS * 2 * PEER_NKEYS, D_MODEL).astype(BF16)
    weights = dict(
        g1=norm1_g[l][None, :], w_main=w_main, b_main=b_main, wg_hi=wg_hi, wg_lo=wg_lo, bg=bg, fb=fb,
        wa=w_att_branch[l].astype(BF16), wm=w_ml_branch[l].astype(BF16), wo=w_out[l].astype(BF16),
        g2=norm2_g[l][None, :], wq=peer_wq[l].astype(BF16), keys_bd=keys_bd, fg=final_g[None, :],
        table=jnp.concatenate([peer_u[l].reshape(-1, SUBLANES, LANES), peer_v[l].reshape(-1, SUBLANES, LANES)],
                              axis=1).reshape(-1, LANES))

    Bs, T, _ = x_sample.shape
    P = cache_att_k.shape[2]
    assert P == ATT_CTX
    ck = cache_att_k[l].reshape(Bs, P, ATT_W).astype(BF16)
    cv = cache_att_v[l].reshape(Bs, P, ATT_W).astype(BF16)
    cat_kv = lambda k, v: (jnp.concatenate([ck, k], axis=1), jnp.concatenate([cv, v], axis=1))
    geom_s = dict(bias=_rel_bias(att_rel_bias[l], T, P + T), KB=P + T, pad=0)
    state_s = (state_mlstm_C[l], state_mlstm_n[l],
               jnp.broadcast_to(state_mlstm_m[l][:, :, None], (Bs, ML_HEADS, ML_DK)))
    y_s, k_s, v_s, c_s, n_s, m_s = _layer(x_sample, weights, cat_kv, geom_s, state_s, T)

    B, S, _ = x_prompt.shape
    pad_kv = lambda k, v: (jnp.pad(k, ((0, 0), (ATT_CTX, 0), (0, 0))), jnp.pad(v, ((0, 0), (ATT_CTX, 0), (0, 0))))
    geom_p = dict(bias=_rel_bias(att_rel_bias[l], CHUNK, ATT_CTX + CHUNK), KB=ATT_CTX + CHUNK, pad=ATT_CTX)
    state_p = (jnp.zeros((B, ML_HEADS, ML_DK, ML_DK), F32), jnp.zeros((B, ML_HEADS, ML_DK), F32),
               jnp.full((B, ML_HEADS, ML_DK), NEG, F32))
    y_p, k_p, v_p, c_p, n_p, m_p = _layer(x_prompt, weights, pad_kv, geom_p, state_p, CHUNK)
    keep = min(ATT_CTX, S)

    st = lambda a: a[None]
    return (y_p, y_s, st(k_p[:, -keep:]), st(v_p[:, -keep:]), st(c_p), st(n_p), st(m_p),
            st(k_s), st(v_s), st(c_s), st(n_s), st(m_s))
```

```python
import functools

import jax
import jax.numpy as jnp
import numpy as np
from jax import lax
from jax.experimental import pallas as pl
from jax.experimental.pallas import tpu as pltpu

F32 = jnp.float32
BF16 = jnp.bfloat16

D_MODEL = 1024
CHUNK = 64
ATT_HEADS = 8
ATT_HD = 64
ATT_W = ATT_HEADS * ATT_HD
ATT_CTX = 8 * CHUNK
MAX_REL = 128
ML_HEADS = 4
ML_DK = 128
ML_W = ML_HEADS * ML_DK
PEER_HEADS = 8
PEER_NKEYS = 128
PEER_DKEY = 128
PEER_TOPK = 16
PEER_PICKS = PEER_HEADS * PEER_TOPK
EPS = 1e-6
NEG = -1e30

LANES = 128
SUBLANES = 8
ROW_TILE = 256
GATE_COLS = LANES
MAIN_COLS = 7 * 512 + 2 * D_MODEL
GROUP = SUBLANES
PEER_BLOCK = 256
GATHER_SLOTS = 2
EXPERT_ROWS = 2 * D_MODEL // LANES
VMEM_LIMIT = 56 * 1024 * 1024


def _params(*sem):
    return pltpu.CompilerParams(dimension_semantics=sem, vmem_limit_bytes=VMEM_LIMIT)


def _rms(x, g):
    return x * lax.rsqrt(jnp.mean(x * x, axis=-1, keepdims=True) + EPS) * g


def _inproj_kernel(x_ref, g_ref, w_ref, b_ref, wgh_ref, wgl_ref, bg_ref, fb_ref,
                   q_o, k_o, v_o, kf_o, vf_o, mq_o, mk_o, mv_o, mo_o, ga_o, gb_o, gt_o):
    xn = _rms(x_ref[...], g_ref[...])
    xh = xn.astype(BF16)

    def proj(off, width):
        return jnp.dot(xh, w_ref[:, off:off + width], preferred_element_type=F32) + b_ref[:, off:off + width]

    q_o[...] = (proj(0, 512) * (ATT_HD ** -0.5)).astype(BF16)
    k = proj(512, 512)
    kf_o[...] = k
    k_o[...] = k.astype(BF16)
    v = proj(1024, 512)
    vf_o[...] = v
    v_o[...] = v.astype(BF16)
    mq_o[...] = proj(1536, 512).astype(BF16)
    mk_o[...] = (proj(2048, 512) * (ML_DK ** -0.5)).astype(BF16)
    mv_o[...] = proj(2560, 512).astype(BF16)
    mo_o[...] = proj(3072, 512)
    ga_o[...] = proj(3584, D_MODEL)
    gb_o[...] = proj(3584 + D_MODEL, D_MODEL)
    xl = (xn - xh.astype(F32)).astype(BF16)
    zg = (jnp.dot(xh, wgh_ref[...], preferred_element_type=F32)
          + jnp.dot(xh, wgl_ref[...], preferred_element_type=F32)
          + jnp.dot(xl, wgh_ref[...], preferred_element_type=F32)) + bg_ref[...]
    t = zg + fb_ref[...]
    log_sig = jnp.minimum(t, 0.0) - jnp.log1p(jnp.exp(-jnp.abs(t)))
    lane = lax.broadcasted_iota(jnp.int32, zg.shape, 1)
    gt_o[...] = jnp.where(lane < ML_HEADS, zg, log_sig)


def _inproj(x, g1, w_main, b_main, wg_hi, wg_lo, bg, fb):
    n = x.shape[0]
    tm = min(ROW_TILE, n)
    row = lambda w: pl.BlockSpec((tm, w), lambda i: (i, 0))
    full = lambda a: pl.BlockSpec(a.shape, lambda i: (0, 0))
    widths = [(512, BF16)] * 3 + [(512, F32)] * 2 + [(512, BF16)] * 3 + [(512, F32), (D_MODEL, F32), (D_MODEL, F32),
                                                                     (GATE_COLS, F32)]
    return pl.pallas_call(
        _inproj_kernel,
        grid=(n // tm,),
        in_specs=[row(D_MODEL), full(g1), full(w_main), full(b_main), full(wg_hi), full(wg_lo), full(bg), full(fb)],
        out_specs=[row(w) for w, _ in widths],
        out_shape=[jax.ShapeDtypeStruct((n, w), dt) for w, dt in widths],
        compiler_params=_params("parallel"),
        name="inproj",
    )(x, g1, w_main, b_main, wg_hi, wg_lo, bg, fb)


def _attn_kernel(q_ref, k_ref, v_ref, bias_ref, o_ref, *, L, KB, pad):
    c = pl.program_id(1)
    start = pl.multiple_of(c * L, L)
    kb = k_ref[0, pl.ds(start, KB), :]
    vb = v_ref[0, pl.ds(start, KB), :]
    q = q_ref[0].astype(F32)
    col = lax.broadcasted_iota(jnp.int32, (1, KB), 1)
    valid = (start + col) >= pad
    lo = lax.broadcasted_iota(jnp.int32, (L, LANES), 1) < ATT_HD
    for hp in range(ATT_HEADS // 2):
        sl = slice(hp * LANES, (hp + 1) * LANES)
        qp, kp, vp = q[:, sl], kb[:, sl], vb[:, sl]
        outs = []
        for half in range(2):
            qm = jnp.where(lo if half == 0 else jnp.logical_not(lo), qp, 0.0).astype(BF16)
            s = lax.dot_general(qm, kp, (((1,), (1,)), ((), ())), preferred_element_type=F32)
            s = s + bias_ref[hp * 2 + half]
            s = jnp.where(valid, s, NEG)
            e = jnp.exp(s - jnp.max(s, axis=-1, keepdims=True))
            p = e / jnp.sum(e, axis=-1, keepdims=True)
            outs.append(jnp.dot(p.astype(BF16), vp, preferred_element_type=F32))
        o_ref[0, :, sl] = jnp.where(lo, outs[0], outs[1]).astype(BF16)


def _attention(q, k, v, bias, *, L, KB, pad):
    B, S, _ = q.shape
    ktot = k.shape[1]
    return pl.pallas_call(
        functools.partial(_attn_kernel, L=L, KB=KB, pad=pad),
        grid=(B, S // L),
        in_specs=[pl.BlockSpec((1, L, ATT_W), lambda b, c: (b, c, 0)),
                  pl.BlockSpec((1, ktot, ATT_W), lambda b, c: (b, 0, 0)),
                  pl.BlockSpec((1, ktot, ATT_W), lambda b, c: (b, 0, 0)),
                  pl.BlockSpec(bias.shape, lambda b, c: (0, 0, 0))],
        out_specs=pl.BlockSpec((1, L, ATT_W), lambda b, c: (b, c, 0)),
        out_shape=jax.ShapeDtypeStruct((B, S, ATT_W), BF16),
        compiler_params=_params("parallel", "arbitrary"),
        name="attention",
    )(q, k, v, bias)


def _rel_bias(table, L, KB):
    d = np.arange(L)[:, None] + ATT_CTX - np.arange(KB)[None, :]
    return jnp.transpose(table[np.clip(d, -MAX_REL, MAX_REL) + MAX_REL], (2, 0, 1))


def _t128(x):
    rows = x.shape[0]
    if rows < LANES:
        x = jnp.concatenate([x, jnp.zeros((LANES - rows, LANES), x.dtype)], axis=0)
    return x.T[:, :rows]


def _mlstm_kernel(q_ref, k_ref, v_ref, g_ref, c0_ref, n0_ref, m0_ref,
                  h_ref, c_ref, n_ref, m_ref, cs, ns, ms, *, L):
    c = pl.program_id(1)

    @pl.when(c == 0)
    def _():
        cs[...] = c0_ref[0]
        ns[...] = n0_ref[0]
        ms[...] = m0_ref[0]

    gates = g_ref[0]
    row = lax.broadcasted_iota(jnp.int32, gates.shape, 0)
    csum = gates
    sh = 1
    while sh < L:
        csum = csum + jnp.where(row >= sh, pltpu.roll(csum, sh, axis=0), 0.0)
        sh *= 2
    gates_t = _t128(gates)
    csum_t = _t128(csum)
    tril = lax.broadcasted_iota(jnp.int32, (L, L), 0) >= lax.broadcasted_iota(jnp.int32, (L, L), 1)
    for h in range(ML_HEADS):
        sl = slice(h * ML_DK, (h + 1) * ML_DK)
        q, k, v = q_ref[0, :, sl], k_ref[0, :, sl], v_ref[0, :, sl]
        b_col = csum[:, ML_HEADS + h:ML_HEADS + h + 1]
        b_row = csum_t[ML_HEADS + h:ML_HEADS + h + 1, :]
        ig_col = gates[:, h:h + 1]
        ig_row = gates_t[h:h + 1, :]
        m_prev = ms[h:h + 1, 0:1]
        a_col = b_col + m_prev
        dmat = jnp.where(tril, b_col - b_row + ig_row, NEG)
        m_t = jnp.maximum(a_col, jnp.max(dmat, axis=-1, keepdims=True))
        w_inter = jnp.exp(a_col - m_t)
        wmat = jnp.exp(dmat - m_t)
        qk = lax.dot_general(q, k, (((1,), (1,)), ((), ())), preferred_element_type=F32)
        wqk = wmat * qk
        c_old = cs[h]
        n_old = ns[h:h + 1, :]
        num = (w_inter * jnp.dot(q, c_old.astype(BF16), preferred_element_type=F32)
               + jnp.dot(wqk.astype(BF16), v, preferred_element_type=F32))
        den = (w_inter * jnp.sum(q.astype(F32) * n_old, axis=-1, keepdims=True)
               + jnp.sum(wqk, axis=-1, keepdims=True))
        h_ref[0, :, sl] = num / jnp.maximum(jnp.abs(den), jnp.exp(-m_t))
        m_new = m_t[L - 1:L, :]
        decay = jnp.exp(a_col[L - 1:L, :] - m_new)
        wl_col = jnp.exp(b_col[L - 1:L, :] - b_col + ig_col - m_new)
        kw = k.astype(F32) * wl_col
        cs[h] = decay * c_old + jnp.dot(_t128(kw).astype(BF16), v, preferred_element_type=F32)
        ns[h:h + 1, :] = decay * n_old + jnp.sum(kw, axis=0, keepdims=True)
        ms[h:h + 1, :] = jnp.broadcast_to(m_new, (1, ML_DK))

    @pl.when(c == pl.num_programs(1) - 1)
    def _():
        c_ref[0] = cs[...]
        n_ref[0] = ns[...]
        m_ref[0] = ms[...]


def _mlstm(q, k, v, gates, c0, n0, m0, *, L):
    B, S, _ = q.shape
    seq = lambda w: pl.BlockSpec((1, L, w), lambda b, c: (b, c, 0))
    st4 = pl.BlockSpec((1, ML_HEADS, ML_DK, ML_DK), lambda b, c: (b, 0, 0, 0))
    st3 = pl.BlockSpec((1, ML_HEADS, ML_DK), lambda b, c: (b, 0, 0))
    return pl.pallas_call(
        functools.partial(_mlstm_kernel, L=L),
        grid=(B, S // L),
        in_specs=[seq(ML_W), seq(ML_W), seq(ML_W), seq(GATE_COLS), st4, st3, st3],
        out_specs=[seq(ML_W), st4, st3, st3],
        out_shape=[jax.ShapeDtypeStruct((B, S, ML_W), F32),
                   jax.ShapeDtypeStruct((B, ML_HEADS, ML_DK, ML_DK), F32),
                   jax.ShapeDtypeStruct((B, ML_HEADS, ML_DK), F32),
                   jax.ShapeDtypeStruct((B, ML_HEADS, ML_DK), F32)],
        scratch_shapes=[pltpu.VMEM((ML_HEADS, ML_DK, ML_DK), F32),
                        pltpu.VMEM((ML_HEADS, ML_DK), F32),
                        pltpu.VMEM((ML_HEADS, ML_DK), F32)],
        compiler_params=_params("parallel", "arbitrary"),
        name="mlstm",
    )(q, k, v, gates, c0, n0, m0)


def _merge_kernel(x_ref, att_ref, h_ref, mo_ref, ga_ref, gb_ref, wa_ref, wm_ref, wo_ref, o_ref):
    ml = (jax.nn.sigmoid(mo_ref[...]) * h_ref[...]).astype(BF16)
    a = jnp.dot(att_ref[...], wa_ref[...], preferred_element_type=F32)
    b = jnp.dot(ml, wm_ref[...], preferred_element_type=F32)
    merged = jax.nn.sigmoid(ga_ref[...]) * a + jax.nn.sigmoid(gb_ref[...]) * b
    o_ref[...] = x_ref[...] + jnp.dot(merged.astype(BF16), wo_ref[...], preferred_element_type=F32)


def _merge(x, att, h, mo, ga, gb, wa, wm, wo):
    n = x.shape[0]
    tm = min(ROW_TILE, n)
    row = lambda w: pl.BlockSpec((tm, w), lambda i: (i, 0))
    full = lambda a: pl.BlockSpec(a.shape, lambda i: (0, 0))
    return pl.pallas_call(
        _merge_kernel,
        grid=(n // tm,),
        in_specs=[row(D_MODEL), row(ATT_W), row(ML_W), row(ML_W), row(D_MODEL), row(D_MODEL),
                  full(wa), full(wm), full(wo)],
        out_specs=row(D_MODEL),
        out_shape=jax.ShapeDtypeStruct((n, D_MODEL), F32),
        compiler_params=_params("parallel"),
        name="merge",
    )(x, att, h, mo, ga, gb, wa, wm, wo)


def _top16(s, n_rows):
    rows = lax.broadcasted_iota(jnp.int32, s.shape, 0)
    vals, ids = [], []
    for _ in range(PEER_TOPK):
        m = jnp.max(s, axis=0, keepdims=True)
        i = jnp.min(jnp.where(s == m, rows, n_rows), axis=0, keepdims=True)
        vals.append(m)
        ids.append(i)
        s = jnp.where(rows == i, -jnp.inf, s)
    return jnp.concatenate(vals, axis=0), jnp.concatenate(ids, axis=0)


def _route_kernel(x_ref, g_ref, wq_ref, keys_ref, xn_o, idx_o, gw_o, st_ref):
    xn = _rms(x_ref[...], g_ref[...])
    xn_o[...] = xn
    q = jnp.dot(xn.astype(BF16), wq_ref[...], preferred_element_type=F32)
    st_ref[...] = lax.dot_general(keys_ref[...], q.astype(BF16), (((1,), (1,)), ((), ())),
                                  preferred_element_type=F32)
    t = st_ref.shape[1]

    def head(h, carry):
        r0 = pl.multiple_of(h * 2 * PEER_NKEYS, 2 * PEER_NKEYS)
        sv0, si0 = _top16(st_ref[pl.ds(r0, PEER_NKEYS), :], PEER_NKEYS)
        sv1, si1 = _top16(st_ref[pl.ds(r0 + PEER_NKEYS, PEER_NKEYS), :], PEER_NKEYS)
        cand = jnp.concatenate([jnp.broadcast_to(sv0[a:a + 1], (PEER_TOPK, t)) + sv1
                                for a in range(PEER_TOPK)], axis=0)
        cidx = jnp.concatenate([jnp.broadcast_to(si0[a:a + 1], (PEER_TOPK, t)) * PEER_NKEYS + si1
                                for a in range(PEER_TOPK)], axis=0)
        fv, fi = _top16(cand, PEER_TOPK * PEER_TOPK)
        pos = lax.broadcasted_iota(jnp.int32, cand.shape, 0)
        eidx = jnp.concatenate([jnp.max(jnp.where(pos == fi[r:r + 1], cidx, -1), axis=0, keepdims=True)
                                for r in range(PEER_TOPK)], axis=0)
        e = jnp.exp(fv - fv[0:1])
        o0 = pl.multiple_of(h * PEER_TOPK, PEER_TOPK)
        gw_o[pl.ds(o0, PEER_TOPK), :] = e / jnp.sum(e, axis=0, keepdims=True)
        idx_o[pl.ds(o0, PEER_TOPK), :] = eidx
        return carry

    lax.fori_loop(0, PEER_HEADS, head, 0)


def _route(x, g2, wq, keys_bd):
    n = x.shape[0]
    tm = min(ROW_TILE, n)
    full = lambda a: pl.BlockSpec(a.shape, lambda i: (0, 0))
    return pl.pallas_call(
        _route_kernel,
        grid=(n // tm,),
        scratch_shapes=[pltpu.VMEM((PEER_HEADS * 2 * PEER_NKEYS, tm), F32)],
        in_specs=[pl.BlockSpec((tm, D_MODEL), lambda i: (i, 0)), full(g2), full(wq), full(keys_bd)],
        out_specs=[pl.BlockSpec((tm, D_MODEL), lambda i: (i, 0)),
                   pl.BlockSpec((PEER_PICKS, tm), lambda i: (0, i)),
                   pl.BlockSpec((PEER_PICKS, tm), lambda i: (0, i))],
        out_shape=[jax.ShapeDtypeStruct((n, D_MODEL), F32),
                   jax.ShapeDtypeStruct((PEER_PICKS, n), jnp.int32),
                   jax.ShapeDtypeStruct((PEER_PICKS, n), F32)],
        compiler_params=_params("parallel"),
        name="peer_route",
    )(x, g2, wq, keys_bd)


def _apply_kernel(idx_hbm, gwt_ref, xn_ref, x_ref, fg_ref, tab_hbm, y_ref, idx_s, buf, gsem, isem, coef_ref, *, nblk):
    i = pl.program_id(0)
    ngroups = PEER_BLOCK // GROUP
    rows = GROUP * PEER_PICKS
    blk_words = PEER_BLOCK * PEER_PICKS
    assert ngroups % GATHER_SLOTS == 0 and GATHER_SLOTS == 2

    def idx_copy(blk, islot):
        return pltpu.make_async_copy(idx_hbm.at[pl.ds(blk * blk_words, blk_words)],
                                     idx_s.at[pl.ds(islot * blk_words, blk_words)], isem.at[islot])

    def row_copy(word, slot, r):
        e = idx_s[word]
        return pltpu.make_async_copy(tab_hbm.at[pl.ds(pl.multiple_of(e * EXPERT_ROWS, EXPERT_ROWS), EXPERT_ROWS), :],
                                     buf.at[slot, pl.ds(r * EXPERT_ROWS, EXPERT_ROWS), :], gsem.at[slot])

    def wait_group(slot):
        pltpu.make_async_copy(tab_hbm.at[pl.ds(0, rows * EXPERT_ROWS), :], buf.at[slot], gsem.at[slot]).wait()

    islot = i % 2
    has_next = i + 1 < nblk

    @pl.when(i == 0)
    def _():
        idx_copy(0, 0).start()
        idx_copy(0, 0).wait()

        def body(r, carry):
            row_copy(r, 0, r).start()
            return carry

        lax.fori_loop(0, rows, body, 0, unroll=8)

    @pl.when(has_next)
    def _():
        idx_copy(i + 1, 1 - islot).start()

    lane = lax.broadcasted_iota(jnp.int32, (PEER_PICKS, LANES), 1)
    sub = lax.broadcasted_iota(jnp.int32, (SUBLANES, LANES), 0)
    masks = {1: (sub & 1) == 0, 2: (sub & 2) == 0, 4: (sub & 4) == 0}

    def fold(a, b, h):
        return jnp.where(masks[h], a, b) + pltpu.roll(jnp.where(masks[h], b, a), h, axis=0)

    def row_sums(ps):
        c = [fold(ps[2 * m], ps[2 * m + 1], 1) for m in range(4)]
        d = [fold(c[0], c[1], 2), fold(c[2], c[3], 2)]
        return fold(d[0], d[1], 4)

    def group_body(g, carry):
        slot = g % GATHER_SLOTS
        nslot = 1 - slot
        last = g + 1 == ngroups

        @pl.when(jnp.logical_and(last, has_next))
        def _():
            idx_copy(i + 1, 1 - islot).wait()

        next_word0 = jnp.where(last, jnp.where(has_next, 1 - islot, islot) * blk_words,
                               islot * blk_words + (g + 1) * rows)
        wait_group(slot)
        t0 = pl.multiple_of(g * GROUP, GROUP)
        lane0 = pl.multiple_of((t0 // LANES) * LANES, LANES)
        gw_tile = gwt_ref[:, pl.ds(lane0, LANES)]
        for j in range(GROUP):
            for k in range(PEER_PICKS):
                row_copy(next_word0 + j * PEER_PICKS + k, nslot, j * PEER_PICKS + k).start(priority=k % 2)
            tok = pl.multiple_of((t0 + j) * SUBLANES, SUBLANES)
            x = xn_ref[pl.ds(tok, SUBLANES), :]
            sums = []
            for c in range(PEER_PICKS // SUBLANES):
                r0 = (j * PEER_PICKS + c * SUBLANES) * EXPERT_ROWS
                sums.append(row_sums([buf[slot, r0 + k * EXPERT_ROWS:r0 + k * EXPERT_ROWS + SUBLANES, :] * x
                                      for k in range(SUBLANES)]))
            act = jnp.sum(jnp.concatenate(sums, axis=0), axis=-1, keepdims=True)
            gelu = 0.5 * act * (1.0 + lax.erf(act * (2.0 ** -0.5)))
            gw_col = jnp.sum(jnp.where(lane == t0 - lane0 + j, gw_tile, 0.0), axis=-1, keepdims=True)
            coef_ref[...] = jnp.broadcast_to(gw_col * gelu, (PEER_PICKS, LANES))
            accs = [jnp.zeros((SUBLANES, LANES), F32) for _ in range(4)]
            for k in range(PEER_PICKS):
                r0 = (j * PEER_PICKS + k) * EXPERT_ROWS + SUBLANES
                accs[k % 4] = accs[k % 4] + coef_ref[k:k + 1, :] * buf[slot, r0:r0 + SUBLANES, :]
            y = x_ref[pl.ds(tok, SUBLANES), :] + ((accs[0] + accs[1]) + (accs[2] + accs[3]))
            ms = jnp.sum(jnp.sum(y * y, axis=0, keepdims=True), axis=-1, keepdims=True) * (1.0 / D_MODEL)
            y_ref[pl.ds(tok, SUBLANES), :] = y * lax.rsqrt(ms + EPS) * fg_ref[...]
        return carry

    lax.fori_loop(0, ngroups, group_body, 0)

    @pl.when(jnp.logical_not(has_next))
    def _():
        wait_group(0)


def _apply(idx, gwt, xn, x, fg, table):
    n = x.shape[0] // SUBLANES
    nblk = n // PEER_BLOCK
    tile = pl.BlockSpec((PEER_BLOCK * SUBLANES, LANES), lambda i: (i, 0))
    return pl.pallas_call(
        functools.partial(_apply_kernel, nblk=nblk),
        grid=(nblk,),
        in_specs=[pl.BlockSpec(memory_space=pl.ANY), pl.BlockSpec((PEER_PICKS, PEER_BLOCK), lambda i: (0, i)),
                  tile, tile,
                  pl.BlockSpec(fg.shape, lambda i: (0, 0)), pl.BlockSpec(memory_space=pl.ANY)],
        out_specs=tile,
        out_shape=jax.ShapeDtypeStruct((n * SUBLANES, LANES), F32),
        scratch_shapes=[pltpu.SMEM((2 * PEER_BLOCK * PEER_PICKS,), jnp.int32),
                        pltpu.VMEM((GATHER_SLOTS, GROUP * PEER_PICKS * EXPERT_ROWS, LANES), F32),
                        pltpu.SemaphoreType.DMA((GATHER_SLOTS,)),
                        pltpu.SemaphoreType.DMA((2,)),
                        pltpu.VMEM((PEER_PICKS, LANES), F32)],
        compiler_params=_params("arbitrary"),
        name="peer_apply",
    )(idx, gwt, xn, x, fg, table)


def _layer(x3, weights, att_kv, att_geom, ml_state, L):
    B, S, _ = x3.shape
    n = B * S
    x = x3.reshape(n, D_MODEL)
    (q, k, v, kf, vf, mq, mk, mv, mo, ga, gb, gates) = _inproj(
        x, weights["g1"], weights["w_main"], weights["b_main"], weights["wg_hi"], weights["wg_lo"],
        weights["bg"], weights["fb"])
    sh = lambda a: a.reshape(B, S, a.shape[-1])
    k_all, v_all = att_kv(sh(k), sh(v))
    att = _attention(sh(q), k_all, v_all, att_geom["bias"], L=L, KB=att_geom["KB"], pad=att_geom["pad"])
    h, c_new, n_new, m_new = _mlstm(sh(mq), sh(mk), sh(mv), sh(gates), *ml_state, L=L)
    x2 = _merge(x, att.reshape(n, ATT_W), h.reshape(n, ML_W), mo, ga, gb,
                weights["wa"], weights["wm"], weights["wo"])
    xn2, idx_t, gw_t = _route(x2, weights["g2"], weights["wq"], weights["keys_bd"])
    tiles = lambda a: a.reshape(n * SUBLANES, LANES)
    y = _apply(idx_t.T.reshape(n * PEER_PICKS), gw_t, tiles(xn2), tiles(x2), weights["fg"].reshape(SUBLANES, LANES),
               weights["table"])
    return (y.reshape(B, S, D_MODEL), sh(kf).reshape(B, S, ATT_HEADS, ATT_HD), sh(vf).reshape(B, S, ATT_HEADS, ATT_HD),
            c_new, n_new, m_new[:, :, 0])


def kernel(x_prompt, x_sample, cache_att_k, cache_att_v, state_mlstm_C, state_mlstm_n, state_mlstm_m, norm1_g, w_in, b_in, ml_f_bias, att_rel_bias, w_att_branch, w_ml_branch, w_out, norm2_g, peer_wq, peer_sub_keys, peer_u, peer_v, final_g):
    depth = w_in.shape[0]
    assert depth == 1, "single-layer step"
    l = 0
    w = w_in[l]
    b = b_in[l]
    gate_lo, gate_hi = 7 * 512, 7 * 512 + 2 * ML_HEADS
    w_main = jnp.concatenate([w[:, :gate_lo], w[:, gate_hi:]], axis=1).astype(BF16)
    b_main = jnp.concatenate([b[:gate_lo], b[gate_hi:]])[None, :]
    wg = jnp.pad(w[:, gate_lo:gate_hi], ((0, 0), (0, GATE_COLS - 2 * ML_HEADS)))
    wg_hi = wg.astype(BF16)
    wg_lo = (wg - wg_hi.astype(F32)).astype(BF16)
    bg = jnp.pad(b[gate_lo:gate_hi], (0, GATE_COLS - 2 * ML_HEADS))[None, :]
    fb = jnp.pad(ml_f_bias[l], (ML_HEADS, GATE_COLS - 2 * ML_HEADS))[None, :]
    sk = peer_sub_keys[l].reshape(PEER_HEADS * 2, PEER_NKEYS, PEER_DKEY // 2)
    eye = jnp.eye(PEER_HEADS * 2, dtype=F32)
    keys_bd = (sk[:, :, None, :] * eye[:, None, :, None]).reshape(PEER_HEADS * 2 * PEER_NKEYS, D_MODEL).astype(BF16)
    weights = dict(
        g1=norm1_g[l][None, :], w_main=w_main, b_main=b_main, wg_hi=wg_hi, wg_lo=wg_lo, bg=bg, fb=fb,
        wa=w_att_branch[l].astype(BF16), wm=w_ml_branch[l].astype(BF16), wo=w_out[l].astype(BF16),
        g2=norm2_g[l][None, :], wq=peer_wq[l].astype(BF16), keys_bd=keys_bd, fg=final_g[None, :],
        table=jnp.concatenate([peer_u[l].reshape(-1, SUBLANES, LANES), peer_v[l].reshape(-1, SUBLANES, LANES)],
                              axis=1).reshape(-1, LANES))

    Bs, T, _ = x_sample.shape
    P = cache_att_k.shape[2]
    assert P == ATT_CTX
    ck = cache_att_k[l].reshape(Bs, P, ATT_W).astype(BF16)
    cv = cache_att_v[l].reshape(Bs, P, ATT_W).astype(BF16)
    cat_kv = lambda k, v: (jnp.concatenate([ck, k], axis=1), jnp.concatenate([cv, v], axis=1))
    geom_s = dict(bias=_rel_bias(att_rel_bias[l], T, P + T), KB=P + T, pad=0)
    state_s = (state_mlstm_C[l], state_mlstm_n[l],
               jnp.broadcast_to(state_mlstm_m[l][:, :, None], (Bs, ML_HEADS, ML_DK)))
    y_s, k_s, v_s, c_s, n_s, m_s = _layer(x_sample, weights, cat_kv, geom_s, state_s, T)

    B, S, _ = x_prompt.shape
    pad_kv = lambda k, v: (jnp.pad(k, ((0, 0), (ATT_CTX, 0), (0, 0))), jnp.pad(v, ((0, 0), (ATT_CTX, 0), (0, 0))))
    geom_p = dict(bias=_rel_bias(att_rel_bias[l], CHUNK, ATT_CTX + CHUNK), KB=ATT_CTX + CHUNK, pad=ATT_CTX)
    state_p = (jnp.zeros((B, ML_HEADS, ML_DK, ML_DK), F32), jnp.zeros((B, ML_HEADS, ML_DK), F32),
               jnp.full((B, ML_HEADS, ML_DK), NEG, F32))
    y_p, k_p, v_p, c_p, n_p, m_p = _layer(x_prompt, weights, pad_kv, geom_p, state_p, CHUNK)
    keep = min(ATT_CTX, S)

    st = lambda a: a[None]
    return (y_p, y_s, st(k_p[:, -keep:]), st(v_p[:, -keep:]), st(c_p), st(n_p), st(m_p),
            st(k_s), st(v_s), st(c_s), st(n_s), st(m_s))
```

```python
import functools

import jax
import jax.numpy as jnp
import numpy as np
from jax import lax
from jax.experimental import pallas as pl
from jax.experimental.pallas import tpu as pltpu

F32 = jnp.float32
BF16 = jnp.bfloat16

D_MODEL = 1024
CHUNK = 64
ATT_HEADS = 8
ATT_HD = 64
ATT_W = ATT_HEADS * ATT_HD
ATT_CTX = 8 * CHUNK
MAX_REL = 128
ML_HEADS = 4
ML_DK = 128
ML_W = ML_HEADS * ML_DK
PEER_HEADS = 8
PEER_NKEYS = 128
PEER_DKEY = 128
PEER_TOPK = 16
PEER_PICKS = PEER_HEADS * PEER_TOPK
EPS = 1e-6
NEG = -1e30

LANES = 128
SUBLANES = 8
ROW_TILE = 256
GATE_COLS = LANES
MAIN_COLS = 7 * 512 + 2 * D_MODEL
GROUP = SUBLANES
PEER_BLOCK = 256
GATHER_SLOTS = 2
EXPERT_ROWS = 2 * D_MODEL // LANES
VMEM_LIMIT = 56 * 1024 * 1024


def _params(*sem):
    return pltpu.CompilerParams(dimension_semantics=sem, vmem_limit_bytes=VMEM_LIMIT)


def _rms(x, g):
    return x * lax.rsqrt(jnp.mean(x * x, axis=-1, keepdims=True) + EPS) * g


def _inproj_kernel(x_ref, g_ref, w_ref, b_ref, wgh_ref, wgl_ref, bg_ref, fb_ref,
                   q_o, k_o, v_o, kf_o, vf_o, mq_o, mk_o, mv_o, mo_o, ga_o, gb_o, gt_o):
    xn = _rms(x_ref[...], g_ref[...])
    xh = xn.astype(BF16)

    def proj(off, width):
        return jnp.dot(xh, w_ref[:, off:off + width], preferred_element_type=F32) + b_ref[:, off:off + width]

    q_o[...] = (proj(0, 512) * (ATT_HD ** -0.5)).astype(BF16)
    k = proj(512, 512)
    kf_o[...] = k
    k_o[...] = k.astype(BF16)
    v = proj(1024, 512)
    vf_o[...] = v
    v_o[...] = v.astype(BF16)
    mq_o[...] = proj(1536, 512).astype(BF16)
    mk_o[...] = (proj(2048, 512) * (ML_DK ** -0.5)).astype(BF16)
    mv_o[...] = proj(2560, 512).astype(BF16)
    mo_o[...] = proj(3072, 512)
    ga_o[...] = proj(3584, D_MODEL)
    gb_o[...] = proj(3584 + D_MODEL, D_MODEL)
    xl = (xn - xh.astype(F32)).astype(BF16)
    zg = (jnp.dot(xh, wgh_ref[...], preferred_element_type=F32)
          + jnp.dot(xh, wgl_ref[...], preferred_element_type=F32)
          + jnp.dot(xl, wgh_ref[...], preferred_element_type=F32)) + bg_ref[...]
    t = zg + fb_ref[...]
    log_sig = jnp.minimum(t, 0.0) - jnp.log1p(jnp.exp(-jnp.abs(t)))
    lane = lax.broadcasted_iota(jnp.int32, zg.shape, 1)
    gt_o[...] = jnp.where(lane < ML_HEADS, zg, log_sig)


def _inproj(x, g1, w_main, b_main, wg_hi, wg_lo, bg, fb):
    n = x.shape[0]
    tm = min(ROW_TILE, n)
    row = lambda w: pl.BlockSpec((tm, w), lambda i: (i, 0))
    full = lambda a: pl.BlockSpec(a.shape, lambda i: (0, 0))
    widths = [(512, BF16)] * 3 + [(512, F32)] * 2 + [(512, BF16)] * 3 + [(512, F32), (D_MODEL, F32), (D_MODEL, F32),
                                                                     (GATE_COLS, F32)]
    return pl.pallas_call(
        _inproj_kernel,
        grid=(n // tm,),
        in_specs=[row(D_MODEL), full(g1), full(w_main), full(b_main), full(wg_hi), full(wg_lo), full(bg), full(fb)],
        out_specs=[row(w) for w, _ in widths],
        out_shape=[jax.ShapeDtypeStruct((n, w), dt) for w, dt in widths],
        compiler_params=_params("parallel"),
        name="inproj",
    )(x, g1, w_main, b_main, wg_hi, wg_lo, bg, fb)


def _attn_kernel(q_ref, k_ref, v_ref, bias_ref, o_ref, *, L, KB, pad):
    c = pl.program_id(1)
    start = pl.multiple_of(c * L, L)
    kb = k_ref[0, pl.ds(start, KB), :]
    vb = v_ref[0, pl.ds(start, KB), :]
    q = q_ref[0].astype(F32)
    col = lax.broadcasted_iota(jnp.int32, (1, KB), 1)
    valid = (start + col) >= pad
    lo = lax.broadcasted_iota(jnp.int32, (L, LANES), 1) < ATT_HD
    for hp in range(ATT_HEADS // 2):
        sl = slice(hp * LANES, (hp + 1) * LANES)
        qp, kp, vp = q[:, sl], kb[:, sl], vb[:, sl]
        outs = []
        for half in range(2):
            qm = jnp.where(lo if half == 0 else jnp.logical_not(lo), qp, 0.0).astype(BF16)
            s = lax.dot_general(qm, kp, (((1,), (1,)), ((), ())), preferred_element_type=F32)
            s = s + bias_ref[hp * 2 + half]
            s = jnp.where(valid, s, NEG)
            e = jnp.exp(s - jnp.max(s, axis=-1, keepdims=True))
            p = e / jnp.sum(e, axis=-1, keepdims=True)
            outs.append(jnp.dot(p.astype(BF16), vp, preferred_element_type=F32))
        o_ref[0, :, sl] = jnp.where(lo, outs[0], outs[1]).astype(BF16)


def _attention(q, k, v, bias, *, L, KB, pad):
    B, S, _ = q.shape
    ktot = k.shape[1]
    return pl.pallas_call(
        functools.partial(_attn_kernel, L=L, KB=KB, pad=pad),
        grid=(B, S // L),
        in_specs=[pl.BlockSpec((1, L, ATT_W), lambda b, c: (b, c, 0)),
                  pl.BlockSpec((1, ktot, ATT_W), lambda b, c: (b, 0, 0)),
                  pl.BlockSpec((1, ktot, ATT_W), lambda b, c: (b, 0, 0)),
                  pl.BlockSpec(bias.shape, lambda b, c: (0, 0, 0))],
        out_specs=pl.BlockSpec((1, L, ATT_W), lambda b, c: (b, c, 0)),
        out_shape=jax.ShapeDtypeStruct((B, S, ATT_W), BF16),
        compiler_params=_params("parallel", "arbitrary"),
        name="attention",
    )(q, k, v, bias)


def _rel_bias(table, L, KB):
    d = np.arange(L)[:, None] + ATT_CTX - np.arange(KB)[None, :]
    return jnp.transpose(table[np.clip(d, -MAX_REL, MAX_REL) + MAX_REL], (2, 0, 1))


def _t128(x):
    rows = x.shape[0]
    if rows < LANES:
        x = jnp.concatenate([x, jnp.zeros((LANES - rows, LANES), x.dtype)], axis=0)
    return x.T[:, :rows]


def _mlstm_kernel(q_ref, k_ref, v_ref, g_ref, c0_ref, n0_ref, m0_ref,
                  h_ref, c_ref, n_ref, m_ref, cs, ns, ms, *, L):
    c = pl.program_id(1)

    @pl.when(c == 0)
    def _():
        cs[...] = c0_ref[0]
        ns[...] = n0_ref[0]
        ms[...] = m0_ref[0]

    gates = g_ref[0]
    row = lax.broadcasted_iota(jnp.int32, gates.shape, 0)
    csum = gates
    sh = 1
    while sh < L:
        csum = csum + jnp.where(row >= sh, pltpu.roll(csum, sh, axis=0), 0.0)
        sh *= 2
    gates_t = _t128(gates)
    csum_t = _t128(csum)
    tril = lax.broadcasted_iota(jnp.int32, (L, L), 0) >= lax.broadcasted_iota(jnp.int32, (L, L), 1)
    for h in range(ML_HEADS):
        sl = slice(h * ML_DK, (h + 1) * ML_DK)
        q, k, v = q_ref[0, :, sl], k_ref[0, :, sl], v_ref[0, :, sl]
        b_col = csum[:, ML_HEADS + h:ML_HEADS + h + 1]
        b_row = csum_t[ML_HEADS + h:ML_HEADS + h + 1, :]
        ig_col = gates[:, h:h + 1]
        ig_row = gates_t[h:h + 1, :]
        m_prev = ms[h:h + 1, 0:1]
        a_col = b_col + m_prev
        dmat = jnp.where(tril, b_col - b_row + ig_row, NEG)
        m_t = jnp.maximum(a_col, jnp.max(dmat, axis=-1, keepdims=True))
        w_inter = jnp.exp(a_col - m_t)
        wmat = jnp.exp(dmat - m_t)
        qk = lax.dot_general(q, k, (((1,), (1,)), ((), ())), preferred_element_type=F32)
        wqk = wmat * qk
        c_old = cs[h]
        n_old = ns[h:h + 1, :]
        num = (w_inter * jnp.dot(q, c_old.astype(BF16), preferred_element_type=F32)
               + jnp.dot(wqk.astype(BF16), v, preferred_element_type=F32))
        den = (w_inter * jnp.sum(q.astype(F32) * n_old, axis=-1, keepdims=True)
               + jnp.sum(wqk, axis=-1, keepdims=True))
        h_ref[0, :, sl] = num / jnp.maximum(jnp.abs(den), jnp.exp(-m_t))
        m_new = m_t[L - 1:L, :]
        decay = jnp.exp(a_col[L - 1:L, :] - m_new)
        wl_col = jnp.exp(b_col[L - 1:L, :] - b_col + ig_col - m_new)
        kw = k.astype(F32) * wl_col
        cs[h] = decay * c_old + jnp.dot(_t128(kw).astype(BF16), v, preferred_element_type=F32)
        ns[h:h + 1, :] = decay * n_old + jnp.sum(kw, axis=0, keepdims=True)
        ms[h:h + 1, :] = jnp.broadcast_to(m_new, (1, ML_DK))

    @pl.when(c == pl.num_programs(1) - 1)
    def _():
        c_ref[0] = cs[...]
        n_ref[0] = ns[...]
        m_ref[0] = ms[...]


def _mlstm(q, k, v, gates, c0, n0, m0, *, L):
    B, S, _ = q.shape
    seq = lambda w: pl.BlockSpec((1, L, w), lambda b, c: (b, c, 0))
    st4 = pl.BlockSpec((1, ML_HEADS, ML_DK, ML_DK), lambda b, c: (b, 0, 0, 0))
    st3 = pl.BlockSpec((1, ML_HEADS, ML_DK), lambda b, c: (b, 0, 0))
    return pl.pallas_call(
        functools.partial(_mlstm_kernel, L=L),
        grid=(B, S // L),
        in_specs=[seq(ML_W), seq(ML_W), seq(ML_W), seq(GATE_COLS), st4, st3, st3],
        out_specs=[seq(ML_W), st4, st3, st3],
        out_shape=[jax.ShapeDtypeStruct((B, S, ML_W), F32),
                   jax.ShapeDtypeStruct((B, ML_HEADS, ML_DK, ML_DK), F32),
                   jax.ShapeDtypeStruct((B, ML_HEADS, ML_DK), F32),
                   jax.ShapeDtypeStruct((B, ML_HEADS, ML_DK), F32)],
        scratch_shapes=[pltpu.VMEM((ML_HEADS, ML_DK, ML_DK), F32),
                        pltpu.VMEM((ML_HEADS, ML_DK), F32),
                        pltpu.VMEM((ML_HEADS, ML_DK), F32)],
        compiler_params=_params("parallel", "arbitrary"),
        name="mlstm",
    )(q, k, v, gates, c0, n0, m0)


def _merge_kernel(x_ref, att_ref, h_ref, mo_ref, ga_ref, gb_ref, wa_ref, wm_ref, wo_ref, o_ref):
    ml = (jax.nn.sigmoid(mo_ref[...]) * h_ref[...]).astype(BF16)
    a = jnp.dot(att_ref[...], wa_ref[...], preferred_element_type=F32)
    b = jnp.dot(ml, wm_ref[...], preferred_element_type=F32)
    merged = jax.nn.sigmoid(ga_ref[...]) * a + jax.nn.sigmoid(gb_ref[...]) * b
    o_ref[...] = x_ref[...] + jnp.dot(merged.astype(BF16), wo_ref[...], preferred_element_type=F32)


def _merge(x, att, h, mo, ga, gb, wa, wm, wo):
    n = x.shape[0]
    tm = min(ROW_TILE, n)
    row = lambda w: pl.BlockSpec((tm, w), lambda i: (i, 0))
    full = lambda a: pl.BlockSpec(a.shape, lambda i: (0, 0))
    return pl.pallas_call(
        _merge_kernel,
        grid=(n // tm,),
        in_specs=[row(D_MODEL), row(ATT_W), row(ML_W), row(ML_W), row(D_MODEL), row(D_MODEL),
                  full(wa), full(wm), full(wo)],
        out_specs=row(D_MODEL),
        out_shape=jax.ShapeDtypeStruct((n, D_MODEL), F32),
        compiler_params=_params("parallel"),
        name="merge",
    )(x, att, h, mo, ga, gb, wa, wm, wo)


def _top16(s, ids, big):
    vals, win = [], []
    for _ in range(PEER_TOPK):
        m = jnp.max(s, axis=0, keepdims=True)
        i = jnp.min(jnp.where(s == m, ids, big), axis=0, keepdims=True)
        vals.append(m)
        win.append(i)
        s = jnp.where(ids == i, -jnp.inf, s)
    return jnp.concatenate(vals, axis=0), jnp.concatenate(win, axis=0)


_PAIR_IDS = np.array([b for b in range(16)] + [a * 16 + b for a in range(1, 8) for b in range(8)]
                     + [a * 16 for a in range(8, 16)], np.int32)


def _select(rank, table):
    out = jnp.zeros_like(table)
    for a in range(PEER_TOPK):
        out = jnp.where(rank == a, table[a:a + 1], out)
    return out


def _route_kernel(x_ref, g_ref, wq_ref, keys_ref, pair_ref, xn_o, idx_o, gw_o, st_ref):
    xn = _rms(x_ref[...], g_ref[...])
    xn_o[...] = xn
    q = jnp.dot(xn.astype(BF16), wq_ref[...], preferred_element_type=F32)
    st_ref[...] = lax.dot_general(keys_ref[...], q.astype(BF16), (((1,), (1,)), ((), ())),
                                  preferred_element_type=F32)
    lane_tiles = st_ref.shape[1] // LANES
    key_ids = lax.broadcasted_iota(jnp.int32, (PEER_NKEYS, LANES), 0)

    def head_tile(it, carry):
        h = it // lane_tiles
        lanes = pl.ds(pl.multiple_of((it % lane_tiles) * LANES, LANES), LANES)
        r0 = pl.multiple_of(h * 2 * PEER_NKEYS, 2 * PEER_NKEYS)
        sv0, si0 = _top16(st_ref[pl.ds(r0, PEER_NKEYS), lanes], key_ids, PEER_NKEYS)
        sv1, si1 = _top16(st_ref[pl.ds(r0 + PEER_NKEYS, PEER_NKEYS), lanes], key_ids, PEER_NKEYS)
        half = PEER_TOPK // 2
        cand = jnp.concatenate(
            [jnp.broadcast_to(sv0[0:1], (PEER_TOPK, LANES)) + sv1]
            + [jnp.broadcast_to(sv0[a:a + 1], (half, LANES)) + sv1[0:half] for a in range(1, half)]
            + [sv0[half:] + jnp.broadcast_to(sv1[0:1], (half, LANES))], axis=0)
        fv, fi = _top16(cand, pair_ref[...], PEER_TOPK * PEER_TOPK)
        eidx = _select(fi >> 4, si0) * PEER_NKEYS + _select(fi & (PEER_TOPK - 1), si1)
        e = jnp.exp(fv - fv[0:1])
        o0 = pl.multiple_of(h * PEER_TOPK, PEER_TOPK)
        gw_o[pl.ds(o0, PEER_TOPK), lanes] = e / jnp.sum(e, axis=0, keepdims=True)
        idx_o[pl.ds(o0, PEER_TOPK), lanes] = eidx
        return carry

    lax.fori_loop(0, PEER_HEADS * lane_tiles, head_tile, 0)


def _route(x, g2, wq, keys_bd):
    n = x.shape[0]
    tm = min(ROW_TILE, n)
    full = lambda a: pl.BlockSpec(a.shape, lambda i: (0, 0))
    pair_ids = jnp.asarray(np.broadcast_to(_PAIR_IDS[:, None], (_PAIR_IDS.size, LANES)))
    return pl.pallas_call(
        _route_kernel,
        grid=(n // tm,),
        scratch_shapes=[pltpu.VMEM((PEER_HEADS * 2 * PEER_NKEYS, tm), F32)],
        in_specs=[pl.BlockSpec((tm, D_MODEL), lambda i: (i, 0)), full(g2), full(wq), full(keys_bd), full(pair_ids)],
        out_specs=[pl.BlockSpec((tm, D_MODEL), lambda i: (i, 0)),
                   pl.BlockSpec((PEER_PICKS, tm), lambda i: (0, i)),
                   pl.BlockSpec((PEER_PICKS, tm), lambda i: (0, i))],
        out_shape=[jax.ShapeDtypeStruct((n, D_MODEL), F32),
                   jax.ShapeDtypeStruct((PEER_PICKS, n), jnp.int32),
                   jax.ShapeDtypeStruct((PEER_PICKS, n), F32)],
        compiler_params=_params("parallel"),
        name="peer_route",
    )(x, g2, wq, keys_bd, pair_ids)


def _apply_kernel(idx_hbm, gwt_ref, xn_ref, x_ref, fg_ref, tab_hbm, y_ref, idx_s, buf, gsem, isem, coef_ref, *, nblk):
    i = pl.program_id(0)
    ngroups = PEER_BLOCK // GROUP
    rows = GROUP * PEER_PICKS
    blk_words = PEER_BLOCK * PEER_PICKS
    assert ngroups % GATHER_SLOTS == 0 and GATHER_SLOTS == 2

    def idx_copy(blk, islot):
        return pltpu.make_async_copy(idx_hbm.at[pl.ds(blk * blk_words, blk_words)],
                                     idx_s.at[pl.ds(islot * blk_words, blk_words)], isem.at[islot])

    def row_copy(e, slot, r):
        return pltpu.make_async_copy(tab_hbm.at[e], buf.at[slot, r], gsem.at[slot])

    def wait_group(slot):
        pltpu.make_async_copy(tab_hbm.at[pl.ds(0, rows)], buf.at[slot], gsem.at[slot]).wait()

    islot = i % 2
    has_next = i + 1 < nblk

    @pl.when(i == 0)
    def _():
        idx_copy(0, 0).start()
        idx_copy(0, 0).wait()

        def body(r, carry):
            row_copy(idx_s[r], 0, r).start()
            return carry

        lax.fori_loop(0, rows, body, 0, unroll=8)

    @pl.when(has_next)
    def _():
        idx_copy(i + 1, 1 - islot).start()

    lane = lax.broadcasted_iota(jnp.int32, (PEER_PICKS, LANES), 1)
    sub = lax.broadcasted_iota(jnp.int32, (SUBLANES, LANES), 0)
    masks = {1: (sub & 1) == 0, 2: (sub & 2) == 0, 4: (sub & 4) == 0}

    def fold(a, b, h):
        return jnp.where(masks[h], a, b) + pltpu.roll(jnp.where(masks[h], b, a), h, axis=0)

    def row_sums(ps):
        c = [fold(ps[2 * m], ps[2 * m + 1], 1) for m in range(4)]
        d = [fold(c[0], c[1], 2), fold(c[2], c[3], 2)]
        return fold(d[0], d[1], 4)

    def do_group(g, slot, next_word0):
        nslot = 1 - slot
        wait_group(slot)
        t0 = pl.multiple_of(g * GROUP, GROUP)
        lane0 = pl.multiple_of((t0 // LANES) * LANES, LANES)
        gw_tile = gwt_ref[:, pl.ds(lane0, LANES)]
        for j in range(GROUP):
            for k in range(PEER_PICKS):
                r = j * PEER_PICKS + k
                row_copy(idx_s[next_word0 + r], nslot, r).start(priority=k % 2)
            tok = pl.multiple_of((t0 + j) * SUBLANES, SUBLANES)
            x = xn_ref[pl.ds(tok, SUBLANES), :]
            sums = []
            for c in range(PEER_PICKS // SUBLANES):
                r0 = j * PEER_PICKS + c * SUBLANES
                sums.append(row_sums([buf[slot, r0 + k, 0:SUBLANES, :] * x for k in range(SUBLANES)]))
            act = jnp.sum(jnp.concatenate(sums, axis=0), axis=-1, keepdims=True)
            gelu = 0.5 * act * (1.0 + lax.erf(act * (2.0 ** -0.5)))
            gw_col = jnp.sum(jnp.where(lane == t0 - lane0 + j, gw_tile, 0.0), axis=-1, keepdims=True)
            coef_ref[...] = jnp.broadcast_to(gw_col * gelu, (PEER_PICKS, LANES))
            accs = [jnp.zeros((SUBLANES, LANES), F32) for _ in range(4)]
            for k in range(PEER_PICKS):
                accs[k % 4] = accs[k % 4] + coef_ref[k:k + 1, :] * buf[slot, j * PEER_PICKS + k, SUBLANES:, :]
            y = x_ref[pl.ds(tok, SUBLANES), :] + ((accs[0] + accs[1]) + (accs[2] + accs[3]))
            ms = jnp.sum(jnp.sum(y * y, axis=0, keepdims=True), axis=-1, keepdims=True) * (1.0 / D_MODEL)
            y_ref[pl.ds(tok, SUBLANES), :] = y * lax.rsqrt(ms + EPS) * fg_ref[...]

    def pair_body(gg, carry):
        g = 2 * gg
        do_group(g, 0, islot * blk_words + (g + 1) * rows)
        last = gg + 1 == ngroups // 2

        @pl.when(jnp.logical_and(last, has_next))
        def _():
            idx_copy(i + 1, 1 - islot).wait()

        do_group(g + 1, 1, jnp.where(last, jnp.where(has_next, 1 - islot, islot) * blk_words,
                                     islot * blk_words + (g + 2) * rows))
        return carry

    lax.fori_loop(0, ngroups // 2, pair_body, 0)

    @pl.when(jnp.logical_not(has_next))
    def _():
        wait_group(0)


def _apply(idx, gwt, xn, x, fg, table):
    n = x.shape[0] // SUBLANES
    nblk = n // PEER_BLOCK
    tile = pl.BlockSpec((PEER_BLOCK * SUBLANES, LANES), lambda i: (i, 0))
    return pl.pallas_call(
        functools.partial(_apply_kernel, nblk=nblk),
        grid=(nblk,),
        in_specs=[pl.BlockSpec(memory_space=pl.ANY), pl.BlockSpec((PEER_PICKS, PEER_BLOCK), lambda i: (0, i)),
                  tile, tile,
                  pl.BlockSpec(fg.shape, lambda i: (0, 0)), pl.BlockSpec(memory_space=pl.ANY)],
        out_specs=tile,
        out_shape=jax.ShapeDtypeStruct((n * SUBLANES, LANES), F32),
        scratch_shapes=[pltpu.SMEM((2 * PEER_BLOCK * PEER_PICKS,), jnp.int32),
                        pltpu.VMEM((GATHER_SLOTS, GROUP * PEER_PICKS, EXPERT_ROWS, LANES), F32),
                        pltpu.SemaphoreType.DMA((GATHER_SLOTS,)),
                        pltpu.SemaphoreType.DMA((2,)),
                        pltpu.VMEM((PEER_PICKS, LANES), F32)],
        compiler_params=_params("arbitrary"),
        name="peer_apply",
    )(idx, gwt, xn, x, fg, table)


def _layer(x3, weights, att_kv, att_geom, ml_state, L):
    B, S, _ = x3.shape
    n = B * S
    x = x3.reshape(n, D_MODEL)
    (q, k, v, kf, vf, mq, mk, mv, mo, ga, gb, gates) = _inproj(
        x, weights["g1"], weights["w_main"], weights["b_main"], weights["wg_hi"], weights["wg_lo"],
        weights["bg"], weights["fb"])
    sh = lambda a: a.reshape(B, S, a.shape[-1])
    k_all, v_all = att_kv(sh(k), sh(v))
    att = _attention(sh(q), k_all, v_all, att_geom["bias"], L=L, KB=att_geom["KB"], pad=att_geom["pad"])
    h, c_new, n_new, m_new = _mlstm(sh(mq), sh(mk), sh(mv), sh(gates), *ml_state, L=L)
    x2 = _merge(x, att.reshape(n, ATT_W), h.reshape(n, ML_W), mo, ga, gb,
                weights["wa"], weights["wm"], weights["wo"])
    xn2, idx_t, gw_t = _route(x2, weights["g2"], weights["wq"], weights["keys_bd"])
    tiles = lambda a: a.reshape(n * SUBLANES, LANES)
    y = _apply(idx_t.T.reshape(n * PEER_PICKS), gw_t, tiles(xn2), tiles(x2), weights["fg"].reshape(SUBLANES, LANES),
               weights["table"])
    return (y.reshape(B, S, D_MODEL), sh(kf).reshape(B, S, ATT_HEADS, ATT_HD), sh(vf).reshape(B, S, ATT_HEADS, ATT_HD),
            c_new, n_new, m_new[:, :, 0])


def kernel(x_prompt, x_sample, cache_att_k, cache_att_v, state_mlstm_C, state_mlstm_n, state_mlstm_m, norm1_g, w_in, b_in, ml_f_bias, att_rel_bias, w_att_branch, w_ml_branch, w_out, norm2_g, peer_wq, peer_sub_keys, peer_u, peer_v, final_g):
    depth = w_in.shape[0]
    assert depth == 1, "single-layer step"
    l = 0
    w = w_in[l]
    b = b_in[l]
    gate_lo, gate_hi = 7 * 512, 7 * 512 + 2 * ML_HEADS
    w_main = jnp.concatenate([w[:, :gate_lo], w[:, gate_hi:]], axis=1).astype(BF16)
    b_main = jnp.concatenate([b[:gate_lo], b[gate_hi:]])[None, :]
    wg = jnp.pad(w[:, gate_lo:gate_hi], ((0, 0), (0, GATE_COLS - 2 * ML_HEADS)))
    wg_hi = wg.astype(BF16)
    wg_lo = (wg - wg_hi.astype(F32)).astype(BF16)
    bg = jnp.pad(b[gate_lo:gate_hi], (0, GATE_COLS - 2 * ML_HEADS))[None, :]
    fb = jnp.pad(ml_f_bias[l], (ML_HEADS, GATE_COLS - 2 * ML_HEADS))[None, :]
    sk = peer_sub_keys[l].reshape(PEER_HEADS * 2, PEER_NKEYS, PEER_DKEY // 2)
    eye = jnp.eye(PEER_HEADS * 2, dtype=F32)
    keys_bd = (sk[:, :, None, :] * eye[:, None, :, None]).reshape(PEER_HEADS * 2 * PEER_NKEYS, D_MODEL).astype(BF16)
    weights = dict(
        g1=norm1_g[l][None, :], w_main=w_main, b_main=b_main, wg_hi=wg_hi, wg_lo=wg_lo, bg=bg, fb=fb,
        wa=w_att_branch[l].astype(BF16), wm=w_ml_branch[l].astype(BF16), wo=w_out[l].astype(BF16),
        g2=norm2_g[l][None, :], wq=peer_wq[l].astype(BF16), keys_bd=keys_bd, fg=final_g[None, :],
        table=jnp.concatenate([peer_u[l].reshape(-1, SUBLANES, LANES), peer_v[l].reshape(-1, SUBLANES, LANES)],
                              axis=1))

    Bs, T, _ = x_sample.shape
    P = cache_att_k.shape[2]
    assert P == ATT_CTX
    ck = cache_att_k[l].reshape(Bs, P, ATT_W).astype(BF16)
    cv = cache_att_v[l].reshape(Bs, P, ATT_W).astype(BF16)
    cat_kv = lambda k, v: (jnp.concatenate([ck, k], axis=1), jnp.concatenate([cv, v], axis=1))
    geom_s = dict(bias=_rel_bias(att_rel_bias[l], T, P + T), KB=P + T, pad=0)
    state_s = (state_mlstm_C[l], state_mlstm_n[l],
               jnp.broadcast_to(state_mlstm_m[l][:, :, None], (Bs, ML_HEADS, ML_DK)))
    y_s, k_s, v_s, c_s, n_s, m_s = _layer(x_sample, weights, cat_kv, geom_s, state_s, T)

    B, S, _ = x_prompt.shape
    pad_kv = lambda k, v: (jnp.pad(k, ((0, 0), (ATT_CTX, 0), (0, 0))), jnp.pad(v, ((0, 0), (ATT_CTX, 0), (0, 0))))
    geom_p = dict(bias=_rel_bias(att_rel_bias[l], CHUNK, ATT_CTX + CHUNK), KB=ATT_CTX + CHUNK, pad=ATT_CTX)
    state_p = (jnp.zeros((B, ML_HEADS, ML_DK, ML_DK), F32), jnp.zeros((B, ML_HEADS, ML_DK), F32),
               jnp.full((B, ML_HEADS, ML_DK), NEG, F32))
    y_p, k_p, v_p, c_p, n_p, m_p = _layer(x_prompt, weights, pad_kv, geom_p, state_p, CHUNK)
    keep = min(ATT_CTX, S)

    st = lambda a: a[None]
    return (y_p, y_s, st(k_p[:, -keep:]), st(v_p[:, -keep:]), st(c_p), st(n_p), st(m_p),
            st(k_s), st(v_s), st(c_s), st(n_s), st(m_s))
```

```python
import functools

import jax
import jax.numpy as jnp
import numpy as np
from jax import lax
from jax.experimental import pallas as pl
from jax.experimental.pallas import tpu as pltpu

F32 = jnp.float32
BF16 = jnp.bfloat16

D_MODEL = 1024
CHUNK = 64
ATT_HEADS = 8
ATT_HD = 64
ATT_W = ATT_HEADS * ATT_HD
ATT_CTX = 8 * CHUNK
MAX_REL = 128
ML_HEADS = 4
ML_DK = 128
ML_W = ML_HEADS * ML_DK
PEER_HEADS = 8
PEER_NKEYS = 128
PEER_DKEY = 128
PEER_TOPK = 16
PEER_PICKS = PEER_HEADS * PEER_TOPK
EPS = 1e-6
NEG = -1e30

LANES = 128
SUBLANES = 8
ROW_TILE = 256
GATE_COLS = LANES
MAIN_COLS = 7 * 512 + 2 * D_MODEL
GROUP = SUBLANES
PEER_BLOCK = 256
GATHER_SLOTS = 2
EXPERT_ROWS = 2 * D_MODEL // LANES
VMEM_LIMIT = 56 * 1024 * 1024


def _params(*sem):
    return pltpu.CompilerParams(dimension_semantics=sem, vmem_limit_bytes=VMEM_LIMIT)


def _rms(x, g):
    return x * lax.rsqrt(jnp.mean(x * x, axis=-1, keepdims=True) + EPS) * g


def _inproj_kernel(x_ref, g_ref, w_ref, b_ref, wgh_ref, wgl_ref, bg_ref, fb_ref,
                   q_o, k_o, v_o, kf_o, vf_o, mq_o, mk_o, mv_o, mo_o, ga_o, gb_o, gt_o):
    xn = _rms(x_ref[...], g_ref[...])
    xh = xn.astype(BF16)

    def proj(off, width):
        return jnp.dot(xh, w_ref[:, off:off + width], preferred_element_type=F32) + b_ref[:, off:off + width]

    q_o[...] = (proj(0, 512) * (ATT_HD ** -0.5)).astype(BF16)
    k = proj(512, 512)
    kf_o[...] = k
    k_o[...] = k.astype(BF16)
    v = proj(1024, 512)
    vf_o[...] = v
    v_o[...] = v.astype(BF16)
    mq_o[...] = proj(1536, 512).astype(BF16)
    mk_o[...] = (proj(2048, 512) * (ML_DK ** -0.5)).astype(BF16)
    mv_o[...] = proj(2560, 512).astype(BF16)
    mo_o[...] = proj(3072, 512)
    ga_o[...] = proj(3584, D_MODEL)
    gb_o[...] = proj(3584 + D_MODEL, D_MODEL)
    xl = (xn - xh.astype(F32)).astype(BF16)
    zg = (jnp.dot(xh, wgh_ref[...], preferred_element_type=F32)
          + jnp.dot(xh, wgl_ref[...], preferred_element_type=F32)
          + jnp.dot(xl, wgh_ref[...], preferred_element_type=F32)) + bg_ref[...]
    t = zg + fb_ref[...]
    log_sig = jnp.minimum(t, 0.0) - jnp.log1p(jnp.exp(-jnp.abs(t)))
    lane = lax.broadcasted_iota(jnp.int32, zg.shape, 1)
    gt_o[...] = jnp.where(lane < ML_HEADS, zg, log_sig)


def _inproj(x, g1, w_main, b_main, wg_hi, wg_lo, bg, fb):
    n = x.shape[0]
    tm = min(ROW_TILE, n)
    row = lambda w: pl.BlockSpec((tm, w), lambda i: (i, 0))
    full = lambda a: pl.BlockSpec(a.shape, lambda i: (0, 0))
    widths = [(512, BF16)] * 3 + [(512, F32)] * 2 + [(512, BF16)] * 3 + [(512, F32), (D_MODEL, F32), (D_MODEL, F32),
                                                                     (GATE_COLS, F32)]
    return pl.pallas_call(
        _inproj_kernel,
        grid=(n // tm,),
        in_specs=[row(D_MODEL), full(g1), full(w_main), full(b_main), full(wg_hi), full(wg_lo), full(bg), full(fb)],
        out_specs=[row(w) for w, _ in widths],
        out_shape=[jax.ShapeDtypeStruct((n, w), dt) for w, dt in widths],
        compiler_params=_params("parallel"),
        name="inproj",
    )(x, g1, w_main, b_main, wg_hi, wg_lo, bg, fb)


def _attn_kernel(q_ref, k_ref, v_ref, bias_ref, o_ref, *, L, KB, pad):
    c = pl.program_id(1)
    start = pl.multiple_of(c * L, L)
    kb = k_ref[0, pl.ds(start, KB), :]
    vb = v_ref[0, pl.ds(start, KB), :]
    q = q_ref[0].astype(F32)
    col = lax.broadcasted_iota(jnp.int32, (1, KB), 1)
    valid = (start + col) >= pad
    lo = lax.broadcasted_iota(jnp.int32, (L, LANES), 1) < ATT_HD
    for hp in range(ATT_HEADS // 2):
        sl = slice(hp * LANES, (hp + 1) * LANES)
        qp, kp, vp = q[:, sl], kb[:, sl], vb[:, sl]
        outs = []
        for half in range(2):
            qm = jnp.where(lo if half == 0 else jnp.logical_not(lo), qp, 0.0).astype(BF16)
            s = lax.dot_general(qm, kp, (((1,), (1,)), ((), ())), preferred_element_type=F32)
            s = s + bias_ref[hp * 2 + half]
            s = jnp.where(valid, s, NEG)
            e = jnp.exp(s - jnp.max(s, axis=-1, keepdims=True))
            outs.append(jnp.dot(e.astype(BF16), vp, preferred_element_type=F32)
                        / jnp.sum(e, axis=-1, keepdims=True))
        o_ref[0, :, sl] = jnp.where(lo, outs[0], outs[1]).astype(BF16)


def _attention(q, k, v, bias, *, L, KB, pad):
    B, S, _ = q.shape
    ktot = k.shape[1]
    return pl.pallas_call(
        functools.partial(_attn_kernel, L=L, KB=KB, pad=pad),
        grid=(B, S // L),
        in_specs=[pl.BlockSpec((1, L, ATT_W), lambda b, c: (b, c, 0)),
                  pl.BlockSpec((1, ktot, ATT_W), lambda b, c: (b, 0, 0)),
                  pl.BlockSpec((1, ktot, ATT_W), lambda b, c: (b, 0, 0)),
                  pl.BlockSpec(bias.shape, lambda b, c: (0, 0, 0))],
        out_specs=pl.BlockSpec((1, L, ATT_W), lambda b, c: (b, c, 0)),
        out_shape=jax.ShapeDtypeStruct((B, S, ATT_W), BF16),
        compiler_params=_params("parallel", "arbitrary"),
        name="attention",
    )(q, k, v, bias)


def _rel_bias(table, L, KB):
    d = np.arange(L)[:, None] + ATT_CTX - np.arange(KB)[None, :]
    return jnp.transpose(table[np.clip(d, -MAX_REL, MAX_REL) + MAX_REL], (2, 0, 1))


def _pair_bias(bias):
    h, l, _ = bias.shape
    neg = jnp.full((h, l, l), NEG, bias.dtype)
    return jnp.concatenate([jnp.concatenate([bias, neg], axis=2), jnp.concatenate([neg, bias], axis=2)], axis=1)


def _t128(x):
    rows = x.shape[0]
    if rows < LANES:
        x = jnp.concatenate([x, jnp.zeros((LANES - rows, LANES), x.dtype)], axis=0)
    return x.T[:, :rows]


def _mlstm_kernel(q_ref, k_ref, v_ref, g_ref, c0_ref, n0_ref, m0_ref,
                  h_ref, c_ref, n_ref, m_ref, cs, ns, ms, *, L):
    c = pl.program_id(1)

    @pl.when(c == 0)
    def _():
        cs[...] = c0_ref[0]
        ns[...] = n0_ref[0]
        ms[...] = m0_ref[0]

    gates = g_ref[0]
    row = lax.broadcasted_iota(jnp.int32, gates.shape, 0)
    csum = gates
    sh = 1
    while sh < L:
        csum = csum + jnp.where(row >= sh, pltpu.roll(csum, sh, axis=0), 0.0)
        sh *= 2
    gates_t = _t128(gates)
    csum_t = _t128(csum)
    tril = lax.broadcasted_iota(jnp.int32, (L, L), 0) >= lax.broadcasted_iota(jnp.int32, (L, L), 1)
    for h in range(ML_HEADS):
        sl = slice(h * ML_DK, (h + 1) * ML_DK)
        q, k, v = q_ref[0, :, sl], k_ref[0, :, sl], v_ref[0, :, sl]
        b_col = csum[:, ML_HEADS + h:ML_HEADS + h + 1]
        b_row = csum_t[ML_HEADS + h:ML_HEADS + h + 1, :]
        ig_col = gates[:, h:h + 1]
        ig_row = gates_t[h:h + 1, :]
        m_prev = ms[h:h + 1, 0:1]
        a_col = b_col + m_prev
        dmat = jnp.where(tril, b_col - b_row + ig_row, NEG)
        m_t = jnp.maximum(a_col, jnp.max(dmat, axis=-1, keepdims=True))
        w_inter = jnp.exp(a_col - m_t)
        wmat = jnp.exp(dmat - m_t)
        qk = lax.dot_general(q, k, (((1,), (1,)), ((), ())), preferred_element_type=F32)
        wqk = wmat * qk
        c_old = cs[h]
        n_old = ns[h:h + 1, :]
        num = (w_inter * jnp.dot(q, c_old.astype(BF16), preferred_element_type=F32)
               + jnp.dot(wqk.astype(BF16), v, preferred_element_type=F32))
        den = (w_inter * jnp.sum(q.astype(F32) * n_old, axis=-1, keepdims=True)
               + jnp.sum(wqk, axis=-1, keepdims=True))
        h_ref[0, :, sl] = num / jnp.maximum(jnp.abs(den), jnp.exp(-m_t))
        m_new = m_t[L - 1:L, :]
        decay = jnp.exp(a_col[L - 1:L, :] - m_new)
        wl_col = jnp.exp(b_col[L - 1:L, :] - b_col + ig_col - m_new)
        kw = k.astype(F32) * wl_col
        cs[h] = decay * c_old + jnp.dot(_t128(kw).astype(BF16), v, preferred_element_type=F32)
        ns[h:h + 1, :] = decay * n_old + jnp.sum(kw, axis=0, keepdims=True)
        ms[h:h + 1, :] = jnp.broadcast_to(m_new, (1, ML_DK))

    @pl.when(c == pl.num_programs(1) - 1)
    def _():
        c_ref[0] = cs[...]
        n_ref[0] = ns[...]
        m_ref[0] = ms[...]


def _mlstm(q, k, v, gates, c0, n0, m0, *, L):
    B, S, _ = q.shape
    seq = lambda w: pl.BlockSpec((1, L, w), lambda b, c: (b, c, 0))
    st4 = pl.BlockSpec((1, ML_HEADS, ML_DK, ML_DK), lambda b, c: (b, 0, 0, 0))
    st3 = pl.BlockSpec((1, ML_HEADS, ML_DK), lambda b, c: (b, 0, 0))
    return pl.pallas_call(
        functools.partial(_mlstm_kernel, L=L),
        grid=(B, S // L),
        in_specs=[seq(ML_W), seq(ML_W), seq(ML_W), seq(GATE_COLS), st4, st3, st3],
        out_specs=[seq(ML_W), st4, st3, st3],
        out_shape=[jax.ShapeDtypeStruct((B, S, ML_W), F32),
                   jax.ShapeDtypeStruct((B, ML_HEADS, ML_DK, ML_DK), F32),
                   jax.ShapeDtypeStruct((B, ML_HEADS, ML_DK), F32),
                   jax.ShapeDtypeStruct((B, ML_HEADS, ML_DK), F32)],
        scratch_shapes=[pltpu.VMEM((ML_HEADS, ML_DK, ML_DK), F32),
                        pltpu.VMEM((ML_HEADS, ML_DK), F32),
                        pltpu.VMEM((ML_HEADS, ML_DK), F32)],
        compiler_params=_params("parallel", "arbitrary"),
        name="mlstm",
    )(q, k, v, gates, c0, n0, m0)


def _merge_kernel(x_ref, att_ref, h_ref, mo_ref, ga_ref, gb_ref, wa_ref, wm_ref, wo_ref, o_ref):
    ml = (jax.nn.sigmoid(mo_ref[...]) * h_ref[...]).astype(BF16)
    a = jnp.dot(att_ref[...], wa_ref[...], preferred_element_type=F32)
    b = jnp.dot(ml, wm_ref[...], preferred_element_type=F32)
    merged = jax.nn.sigmoid(ga_ref[...]) * a + jax.nn.sigmoid(gb_ref[...]) * b
    o_ref[...] = x_ref[...] + jnp.dot(merged.astype(BF16), wo_ref[...], preferred_element_type=F32)


def _merge(x, att, h, mo, ga, gb, wa, wm, wo):
    n = x.shape[0]
    tm = min(ROW_TILE, n)
    row = lambda w: pl.BlockSpec((tm, w), lambda i: (i, 0))
    full = lambda a: pl.BlockSpec(a.shape, lambda i: (0, 0))
    return pl.pallas_call(
        _merge_kernel,
        grid=(n // tm,),
        in_specs=[row(D_MODEL), row(ATT_W), row(ML_W), row(ML_W), row(D_MODEL), row(D_MODEL),
                  full(wa), full(wm), full(wo)],
        out_specs=row(D_MODEL),
        out_shape=jax.ShapeDtypeStruct((n, D_MODEL), F32),
        compiler_params=_params("parallel"),
        name="merge",
    )(x, att, h, mo, ga, gb, wa, wm, wo)


def _top16(s, ids, big):
    vals, win = [], []
    for _ in range(PEER_TOPK):
        m = jnp.max(s, axis=0, keepdims=True)
        i = jnp.min(jnp.where(s == m, ids, big), axis=0, keepdims=True)
        vals.append(m)
        win.append(i)
        s = jnp.where(ids == i, -jnp.inf, s)
    return jnp.concatenate(vals, axis=0), jnp.concatenate(win, axis=0)


_PAIR_IDS = np.array([b for b in range(16)] + [a * 16 + b for a in range(1, 8) for b in range(8)]
                     + [a * 16 for a in range(8, 16)], np.int32)


def _select(rank, table):
    out = jnp.zeros_like(table)
    for a in range(PEER_TOPK):
        out = jnp.where(rank == a, table[a:a + 1], out)
    return out


def _route_kernel(x_ref, g_ref, wq_ref, keys_ref, pair_ref, xn_o, xt_o, idx_o, gw_o, st_ref):
    x = x_ref[...]
    xn = _rms(x, g_ref[...])
    xn_o[...] = xn.reshape(xn_o.shape)
    xt_o[...] = x.reshape(xt_o.shape)
    q = jnp.dot(xn.astype(BF16), wq_ref[...], preferred_element_type=F32)
    st_ref[...] = lax.dot_general(keys_ref[...], q.astype(BF16), (((1,), (1,)), ((), ())),
                                  preferred_element_type=F32)
    lane_tiles = st_ref.shape[1] // LANES
    key_ids = lax.broadcasted_iota(jnp.int32, (PEER_NKEYS, LANES), 0)

    def head_tile(it, carry):
        h = it // lane_tiles
        lanes = pl.ds(pl.multiple_of((it % lane_tiles) * LANES, LANES), LANES)
        r0 = pl.multiple_of(h * 2 * PEER_NKEYS, 2 * PEER_NKEYS)
        sv0, si0 = _top16(st_ref[pl.ds(r0, PEER_NKEYS), lanes], key_ids, PEER_NKEYS)
        sv1, si1 = _top16(st_ref[pl.ds(r0 + PEER_NKEYS, PEER_NKEYS), lanes], key_ids, PEER_NKEYS)
        half = PEER_TOPK // 2
        cand = jnp.concatenate(
            [jnp.broadcast_to(sv0[0:1], (PEER_TOPK, LANES)) + sv1]
            + [jnp.broadcast_to(sv0[a:a + 1], (half, LANES)) + sv1[0:half] for a in range(1, half)]
            + [sv0[half:] + jnp.broadcast_to(sv1[0:1], (half, LANES))], axis=0)
        fv, fi = _top16(cand, pair_ref[...], PEER_TOPK * PEER_TOPK)
        eidx = _select(fi >> 4, si0) * PEER_NKEYS + _select(fi & (PEER_TOPK - 1), si1)
        e = jnp.exp(fv - fv[0:1])
        o0 = pl.multiple_of(h * PEER_TOPK, PEER_TOPK)
        gw_o[pl.ds(o0, PEER_TOPK), lanes] = e / jnp.sum(e, axis=0, keepdims=True)
        idx_o[pl.ds(o0, PEER_TOPK), lanes] = eidx
        return carry

    lax.fori_loop(0, PEER_HEADS * lane_tiles, head_tile, 0)


def _route(x, g2, wq, keys_bd):
    n = x.shape[0]
    tm = min(ROW_TILE, n)
    full = lambda a: pl.BlockSpec(a.shape, lambda i: (0, 0))
    pair_ids = jnp.asarray(np.broadcast_to(_PAIR_IDS[:, None], (_PAIR_IDS.size, LANES)))
    return pl.pallas_call(
        _route_kernel,
        grid=(n // tm,),
        scratch_shapes=[pltpu.VMEM((PEER_HEADS * 2 * PEER_NKEYS, tm), F32)],
        in_specs=[pl.BlockSpec((tm, D_MODEL), lambda i: (i, 0)), full(g2), full(wq), full(keys_bd), full(pair_ids)],
        out_specs=[pl.BlockSpec((tm, SUBLANES, LANES), lambda i: (i, 0, 0)),
                   pl.BlockSpec((tm, SUBLANES, LANES), lambda i: (i, 0, 0)),
                   pl.BlockSpec((PEER_PICKS, tm), lambda i: (0, i)),
                   pl.BlockSpec((PEER_PICKS, tm), lambda i: (0, i))],
        out_shape=[jax.ShapeDtypeStruct((n, SUBLANES, LANES), F32),
                   jax.ShapeDtypeStruct((n, SUBLANES, LANES), F32),
                   jax.ShapeDtypeStruct((PEER_PICKS, n), jnp.int32),
                   jax.ShapeDtypeStruct((PEER_PICKS, n), F32)],
        compiler_params=_params("parallel"),
        name="peer_route",
    )(x, g2, wq, keys_bd, pair_ids)


def _apply_kernel(idx_hbm, gwt_ref, xn_ref, x_ref, fg_ref, tab_hbm, y_ref, idx_s, buf, gsem, isem, coef_ref, *, nblk):
    i = pl.program_id(0)
    ngroups = PEER_BLOCK // GROUP
    rows = GROUP * PEER_PICKS
    blk_words = PEER_BLOCK * PEER_PICKS
    assert ngroups % GATHER_SLOTS == 0 and GATHER_SLOTS == 2

    def idx_copy(blk, islot):
        return pltpu.make_async_copy(idx_hbm.at[pl.ds(blk * blk_words, blk_words)],
                                     idx_s.at[pl.ds(islot * blk_words, blk_words)], isem.at[islot])

    def row_copy(e, slot, r):
        return pltpu.make_async_copy(tab_hbm.at[e], buf.at[slot, r], gsem.at[slot])

    def wait_group(slot):
        pltpu.make_async_copy(tab_hbm.at[pl.ds(0, rows)], buf.at[slot], gsem.at[slot]).wait()

    islot = i % 2
    has_next = i + 1 < nblk

    @pl.when(i == 0)
    def _():
        idx_copy(0, 0).start()
        idx_copy(0, 0).wait()

        def body(r, carry):
            row_copy(idx_s[r], 0, r).start()
            return carry

        lax.fori_loop(0, rows, body, 0, unroll=8)

    @pl.when(has_next)
    def _():
        idx_copy(i + 1, 1 - islot).start()

    lane = lax.broadcasted_iota(jnp.int32, (PEER_PICKS, LANES), 1)
    sub = lax.broadcasted_iota(jnp.int32, (SUBLANES, LANES), 0)
    masks = {1: (sub & 1) == 0, 2: (sub & 2) == 0, 4: (sub & 4) == 0}

    def fold(a, b, h):
        return jnp.where(masks[h], a, b) + pltpu.roll(jnp.where(masks[h], b, a), h, axis=0)

    def row_sums(ps):
        c = [fold(ps[2 * m], ps[2 * m + 1], 1) for m in range(4)]
        d = [fold(c[0], c[1], 2), fold(c[2], c[3], 2)]
        return fold(d[0], d[1], 4)

    def do_group(g, slot, next_word0):
        nslot = 1 - slot
        wait_group(slot)
        t0 = pl.multiple_of(g * GROUP, GROUP)
        lane0 = pl.multiple_of((t0 // LANES) * LANES, LANES)
        gw_tile = gwt_ref[:, pl.ds(lane0, LANES)]
        ys = []
        for j in range(GROUP):
            for k in range(PEER_PICKS):
                r = j * PEER_PICKS + k
                row_copy(idx_s[next_word0 + r], nslot, r).start(priority=k % 2)
            x = xn_ref[t0 + j]
            sums = []
            for c in range(PEER_PICKS // SUBLANES):
                r0 = j * PEER_PICKS + c * SUBLANES
                sums.append(row_sums([buf[slot, r0 + k, 0:SUBLANES, :] * x for k in range(SUBLANES)]))
            act = jnp.sum(jnp.concatenate(sums, axis=0), axis=-1, keepdims=True)
            gelu = 0.5 * act * (1.0 + lax.erf(act * (2.0 ** -0.5)))
            gw_col = jnp.sum(jnp.where(lane == t0 - lane0 + j, gw_tile, 0.0), axis=-1, keepdims=True)
            coef_ref[...] = jnp.broadcast_to(gw_col * gelu, (PEER_PICKS, LANES))
            accs = [jnp.zeros((SUBLANES, LANES), F32) for _ in range(4)]
            for k in range(PEER_PICKS):
                accs[k % 4] = accs[k % 4] + coef_ref[k:k + 1, :] * buf[slot, j * PEER_PICKS + k, SUBLANES:, :]
            y = x_ref[t0 + j] + ((accs[0] + accs[1]) + (accs[2] + accs[3]))
            ms = jnp.sum(jnp.sum(y * y, axis=0, keepdims=True), axis=-1, keepdims=True) * (1.0 / D_MODEL)
            ys.append(y * lax.rsqrt(ms + EPS) * fg_ref[...])
        y_ref[pl.ds(t0, GROUP), :] = jnp.stack(ys, axis=0).reshape(GROUP, D_MODEL)

    def pair_body(gg, carry):
        g = 2 * gg
        do_group(g, 0, islot * blk_words + (g + 1) * rows)
        last = gg + 1 == ngroups // 2

        @pl.when(jnp.logical_and(last, has_next))
        def _():
            idx_copy(i + 1, 1 - islot).wait()

        do_group(g + 1, 1, jnp.where(last, jnp.where(has_next, 1 - islot, islot) * blk_words,
                                     islot * blk_words + (g + 2) * rows))
        return carry

    lax.fori_loop(0, ngroups // 2, pair_body, 0)

    @pl.when(jnp.logical_not(has_next))
    def _():
        wait_group(0)


def _apply(idx, gwt, xn, x, fg, table):
    n = x.shape[0]
    nblk = n // PEER_BLOCK
    tile = pl.BlockSpec((PEER_BLOCK, SUBLANES, LANES), lambda i: (i, 0, 0))
    return pl.pallas_call(
        functools.partial(_apply_kernel, nblk=nblk),
        grid=(nblk,),
        in_specs=[pl.BlockSpec(memory_space=pl.ANY), pl.BlockSpec((PEER_PICKS, PEER_BLOCK), lambda i: (0, i)),
                  tile, tile,
                  pl.BlockSpec(fg.shape, lambda i: (0, 0)), pl.BlockSpec(memory_space=pl.ANY)],
        out_specs=pl.BlockSpec((PEER_BLOCK, D_MODEL), lambda i: (i, 0)),
        out_shape=jax.ShapeDtypeStruct((n, D_MODEL), F32),
        scratch_shapes=[pltpu.SMEM((2 * PEER_BLOCK * PEER_PICKS,), jnp.int32),
                        pltpu.VMEM((GATHER_SLOTS, GROUP * PEER_PICKS, EXPERT_ROWS, LANES), F32),
                        pltpu.SemaphoreType.DMA((GATHER_SLOTS,)),
                        pltpu.SemaphoreType.DMA((2,)),
                        pltpu.VMEM((PEER_PICKS, LANES), F32)],
        compiler_params=_params("arbitrary"),
        name="peer_apply",
    )(idx, gwt, xn, x, fg, table)


def _layer(x3, weights, att_kv, att_geom, ml_state, L):
    B, S, _ = x3.shape
    n = B * S
    x = x3.reshape(n, D_MODEL)
    (q, k, v, kf, vf, mq, mk, mv, mo, ga, gb, gates) = _inproj(
        x, weights["g1"], weights["w_main"], weights["b_main"], weights["wg_hi"], weights["wg_lo"],
        weights["bg"], weights["fb"])
    sh = lambda a: a.reshape(B, S, a.shape[-1])
    k_all, v_all = att_kv(sh(k), sh(v))
    att = _attention(sh(q), k_all, v_all, att_geom["bias"], L=att_geom["L"], KB=att_geom["KB"], pad=att_geom["pad"])
    h, c_new, n_new, m_new = _mlstm(sh(mq), sh(mk), sh(mv), sh(gates), *ml_state, L=L)
    x2 = _merge(x, att.reshape(n, ATT_W), h.reshape(n, ML_W), mo, ga, gb,
                weights["wa"], weights["wm"], weights["wo"])
    xn2, x2_tiles, idx_t, gw_t = _route(x2, weights["g2"], weights["wq"], weights["keys_bd"])
    y = _apply(idx_t.T.reshape(n * PEER_PICKS), gw_t, xn2, x2_tiles, weights["fg"].reshape(SUBLANES, LANES),
               weights["table"])
    return (y.reshape(B, S, D_MODEL), sh(kf).reshape(B, S, ATT_HEADS, ATT_HD), sh(vf).reshape(B, S, ATT_HEADS, ATT_HD),
            c_new, n_new, m_new[:, :, 0])


def kernel(x_prompt, x_sample, cache_att_k, cache_att_v, state_mlstm_C, state_mlstm_n, state_mlstm_m, norm1_g, w_in, b_in, ml_f_bias, att_rel_bias, w_att_branch, w_ml_branch, w_out, norm2_g, peer_wq, peer_sub_keys, peer_u, peer_v, final_g):
    depth = w_in.shape[0]
    assert depth == 1, "single-layer step"
    l = 0
    w = w_in[l]
    b = b_in[l]
    gate_lo, gate_hi = 7 * 512, 7 * 512 + 2 * ML_HEADS
    w_main = jnp.concatenate([w[:, :gate_lo], w[:, gate_hi:]], axis=1).astype(BF16)
    b_main = jnp.concatenate([b[:gate_lo], b[gate_hi:]])[None, :]
    wg = jnp.pad(w[:, gate_lo:gate_hi], ((0, 0), (0, GATE_COLS - 2 * ML_HEADS)))
    wg_hi = wg.astype(BF16)
    wg_lo = (wg - wg_hi.astype(F32)).astype(BF16)
    bg = jnp.pad(b[gate_lo:gate_hi], (0, GATE_COLS - 2 * ML_HEADS))[None, :]
    fb = jnp.pad(ml_f_bias[l], (ML_HEADS, GATE_COLS - 2 * ML_HEADS))[None, :]
    sk = peer_sub_keys[l].reshape(PEER_HEADS * 2, PEER_NKEYS, PEER_DKEY // 2)
    eye = jnp.eye(PEER_HEADS * 2, dtype=F32)
    keys_bd = (sk[:, :, None, :] * eye[:, None, :, None]).reshape(PEER_HEADS * 2 * PEER_NKEYS, D_MODEL).astype(BF16)
    weights = dict(
        g1=norm1_g[l][None, :], w_main=w_main, b_main=b_main, wg_hi=wg_hi, wg_lo=wg_lo, bg=bg, fb=fb,
        wa=w_att_branch[l].astype(BF16), wm=w_ml_branch[l].astype(BF16), wo=w_out[l].astype(BF16),
        g2=norm2_g[l][None, :], wq=peer_wq[l].astype(BF16), keys_bd=keys_bd, fg=final_g[None, :],
        table=jnp.concatenate([peer_u[l].reshape(-1, SUBLANES, LANES), peer_v[l].reshape(-1, SUBLANES, LANES)],
                              axis=1))

    Bs, T, _ = x_sample.shape
    P = cache_att_k.shape[2]
    assert P == ATT_CTX
    ck = cache_att_k[l].reshape(Bs, P, ATT_W).astype(BF16)
    cv = cache_att_v[l].reshape(Bs, P, ATT_W).astype(BF16)
    cat_kv = lambda k, v: (jnp.concatenate([ck, k], axis=1), jnp.concatenate([cv, v], axis=1))
    geom_s = dict(bias=_rel_bias(att_rel_bias[l], T, P + T), L=T, KB=P + T, pad=0)
    state_s = (state_mlstm_C[l], state_mlstm_n[l],
               jnp.broadcast_to(state_mlstm_m[l][:, :, None], (Bs, ML_HEADS, ML_DK)))
    y_s, k_s, v_s, c_s, n_s, m_s = _layer(x_sample, weights, cat_kv, geom_s, state_s, T)

    B, S, _ = x_prompt.shape
    pad_kv = lambda k, v: (jnp.pad(k, ((0, 0), (ATT_CTX, 0), (0, 0))), jnp.pad(v, ((0, 0), (ATT_CTX, 0), (0, 0))))
    geom_p = dict(bias=_pair_bias(_rel_bias(att_rel_bias[l], CHUNK, ATT_CTX + CHUNK)), L=2 * CHUNK,
                  KB=ATT_CTX + 2 * CHUNK, pad=ATT_CTX)
    state_p = (jnp.zeros((B, ML_HEADS, ML_DK, ML_DK), F32), jnp.zeros((B, ML_HEADS, ML_DK), F32),
               jnp.full((B, ML_HEADS, ML_DK), NEG, F32))
    y_p, k_p, v_p, c_p, n_p, m_p = _layer(x_prompt, weights, pad_kv, geom_p, state_p, CHUNK)
    keep = min(ATT_CTX, S)

    st = lambda a: a[None]
    return (y_p, y_s, st(k_p[:, -keep:]), st(v_p[:, -keep:]), st(c_p), st(n_p), st(m_p),
            st(k_s), st(v_s), st(c_s), st(n_s), st(m_s))
```

```python
import functools

import jax
import jax.numpy as jnp
import numpy as np
from jax import lax
from jax.experimental import pallas as pl
from jax.experimental.pallas import tpu as pltpu

F32 = jnp.float32
BF16 = jnp.bfloat16

D_MODEL = 1024
CHUNK = 64
ATT_HEADS = 8
ATT_HD = 64
ATT_W = ATT_HEADS * ATT_HD
ATT_CTX = 8 * CHUNK
MAX_REL = 128
ML_HEADS = 4
ML_DK = 128
ML_W = ML_HEADS * ML_DK
PEER_HEADS = 8
PEER_NKEYS = 128
PEER_DKEY = 128
PEER_TOPK = 16
PEER_PICKS = PEER_HEADS * PEER_TOPK
EPS = 1e-6
NEG = -1e30

LANES = 128
SUBLANES = 8
ROW_TILE = 256
GATE_COLS = LANES
MAIN_COLS = 7 * 512 + 2 * D_MODEL
GROUP = SUBLANES
PEER_BLOCK = 256
GATHER_SLOTS = 2
EXPERT_ROWS = 2 * D_MODEL // LANES
VMEM_LIMIT = 56 * 1024 * 1024


def _params(*sem):
    return pltpu.CompilerParams(dimension_semantics=sem, vmem_limit_bytes=VMEM_LIMIT)


def _rms(x, g):
    return x * lax.rsqrt(jnp.mean(x * x, axis=-1, keepdims=True) + EPS) * g


def _inproj_kernel(x_ref, g_ref, w_ref, b_ref, wgh_ref, wgl_ref, bg_ref, fb_ref,
                   q_o, k_o, v_o, kf_o, vf_o, mq_o, mk_o, mv_o, mo_o, ga_o, gb_o, gt_o):
    xn = _rms(x_ref[...], g_ref[...])
    xh = xn.astype(BF16)

    def proj(off, width):
        return jnp.dot(xh, w_ref[:, off:off + width], preferred_element_type=F32) + b_ref[:, off:off + width]

    q_o[...] = (proj(0, 512) * (ATT_HD ** -0.5)).astype(BF16)
    k = proj(512, 512)
    kf_o[...] = k
    k_o[...] = k.astype(BF16)
    v = proj(1024, 512)
    vf_o[...] = v
    v_o[...] = v.astype(BF16)
    mq_o[...] = proj(1536, 512).astype(BF16)
    mk_o[...] = (proj(2048, 512) * (ML_DK ** -0.5)).astype(BF16)
    mv_o[...] = proj(2560, 512).astype(BF16)
    mo_o[...] = proj(3072, 512)
    ga_o[...] = proj(3584, D_MODEL)
    gb_o[...] = proj(3584 + D_MODEL, D_MODEL)
    xl = (xn - xh.astype(F32)).astype(BF16)
    zg = (jnp.dot(xh, wgh_ref[...], preferred_element_type=F32)
          + jnp.dot(xh, wgl_ref[...], preferred_element_type=F32)
          + jnp.dot(xl, wgh_ref[...], preferred_element_type=F32)) + bg_ref[...]
    t = zg + fb_ref[...]
    log_sig = jnp.minimum(t, 0.0) - jnp.log1p(jnp.exp(-jnp.abs(t)))
    lane = lax.broadcasted_iota(jnp.int32, zg.shape, 1)
    gt_o[...] = jnp.where(lane < ML_HEADS, zg, log_sig)


def _inproj(x, g1, w_main, b_main, wg_hi, wg_lo, bg, fb):
    n = x.shape[0]
    tm = min(ROW_TILE, n)
    row = lambda w: pl.BlockSpec((tm, w), lambda i: (i, 0))
    full = lambda a: pl.BlockSpec(a.shape, lambda i: (0, 0))
    widths = [(512, BF16)] * 3 + [(512, F32)] * 2 + [(512, BF16)] * 3 + [(512, F32), (D_MODEL, F32), (D_MODEL, F32),
                                                                     (GATE_COLS, F32)]
    return pl.pallas_call(
        _inproj_kernel,
        grid=(n // tm,),
        in_specs=[row(D_MODEL), full(g1), full(w_main), full(b_main), full(wg_hi), full(wg_lo), full(bg), full(fb)],
        out_specs=[row(w) for w, _ in widths],
        out_shape=[jax.ShapeDtypeStruct((n, w), dt) for w, dt in widths],
        compiler_params=_params("parallel"),
        name="inproj",
    )(x, g1, w_main, b_main, wg_hi, wg_lo, bg, fb)


def _attn_kernel(q_ref, k_ref, v_ref, bias_ref, o_ref, *, L, KB, pad):
    c = pl.program_id(1)
    start = pl.multiple_of(c * L, L)
    kb = k_ref[0, pl.ds(start, KB), :]
    vb = v_ref[0, pl.ds(start, KB), :]
    q = q_ref[0].astype(F32)
    col = lax.broadcasted_iota(jnp.int32, (1, KB), 1)
    valid = (start + col) >= pad
    lo = lax.broadcasted_iota(jnp.int32, (L, LANES), 1) < ATT_HD
    for hp in range(ATT_HEADS // 2):
        sl = slice(hp * LANES, (hp + 1) * LANES)
        qp, kp, vp = q[:, sl], kb[:, sl], vb[:, sl]
        outs = []
        for half in range(2):
            qm = jnp.where(lo if half == 0 else jnp.logical_not(lo), qp, 0.0).astype(BF16)
            s = lax.dot_general(qm, kp, (((1,), (1,)), ((), ())), preferred_element_type=F32)
            s = s + bias_ref[hp * 2 + half]
            s = jnp.where(valid, s, NEG)
            e = jnp.exp(s - jnp.max(s, axis=-1, keepdims=True))
            outs.append(jnp.dot(e.astype(BF16), vp, preferred_element_type=F32)
                        / jnp.sum(e, axis=-1, keepdims=True))
        o_ref[0, :, sl] = jnp.where(lo, outs[0], outs[1]).astype(BF16)


def _attention(q, k, v, bias, *, L, KB, pad):
    B, S, _ = q.shape
    ktot = k.shape[1]
    return pl.pallas_call(
        functools.partial(_attn_kernel, L=L, KB=KB, pad=pad),
        grid=(B, S // L),
        in_specs=[pl.BlockSpec((1, L, ATT_W), lambda b, c: (b, c, 0)),
                  pl.BlockSpec((1, ktot, ATT_W), lambda b, c: (b, 0, 0)),
                  pl.BlockSpec((1, ktot, ATT_W), lambda b, c: (b, 0, 0)),
                  pl.BlockSpec(bias.shape, lambda b, c: (0, 0, 0))],
        out_specs=pl.BlockSpec((1, L, ATT_W), lambda b, c: (b, c, 0)),
        out_shape=jax.ShapeDtypeStruct((B, S, ATT_W), BF16),
        compiler_params=_params("parallel", "arbitrary"),
        name="attention",
    )(q, k, v, bias)


def _rel_bias(table, L, KB):
    d = np.arange(L)[:, None] + ATT_CTX - np.arange(KB)[None, :]
    return jnp.transpose(table[np.clip(d, -MAX_REL, MAX_REL) + MAX_REL], (2, 0, 1))


def _pair_bias(bias):
    h, l, _ = bias.shape
    neg = jnp.full((h, l, l), NEG, bias.dtype)
    return jnp.concatenate([jnp.concatenate([bias, neg], axis=2), jnp.concatenate([neg, bias], axis=2)], axis=1)


def _t128(x):
    rows = x.shape[0]
    if rows < LANES:
        x = jnp.concatenate([x, jnp.zeros((LANES - rows, LANES), x.dtype)], axis=0)
    return x.T[:, :rows]


def _mlstm_kernel(q_ref, k_ref, v_ref, g_ref, c0_ref, n0_ref, m0_ref,
                  h_ref, c_ref, n_ref, m_ref, cs, ns, ms, *, L):
    c = pl.program_id(1)

    @pl.when(c == 0)
    def _():
        cs[...] = c0_ref[0]
        ns[...] = n0_ref[0]
        ms[...] = m0_ref[0]

    gates = g_ref[0]
    row = lax.broadcasted_iota(jnp.int32, gates.shape, 0)
    csum = gates
    sh = 1
    while sh < L:
        csum = csum + jnp.where(row >= sh, pltpu.roll(csum, sh, axis=0), 0.0)
        sh *= 2
    gates_t = _t128(gates)
    csum_t = _t128(csum)
    tril = lax.broadcasted_iota(jnp.int32, (L, L), 0) >= lax.broadcasted_iota(jnp.int32, (L, L), 1)
    for h in range(ML_HEADS):
        sl = slice(h * ML_DK, (h + 1) * ML_DK)
        q, k, v = q_ref[0, :, sl], k_ref[0, :, sl], v_ref[0, :, sl]
        b_col = csum[:, ML_HEADS + h:ML_HEADS + h + 1]
        b_row = csum_t[ML_HEADS + h:ML_HEADS + h + 1, :]
        ig_col = gates[:, h:h + 1]
        ig_row = gates_t[h:h + 1, :]
        m_prev = ms[h:h + 1, 0:1]
        a_col = b_col + m_prev
        dmat = jnp.where(tril, b_col - b_row + ig_row, NEG)
        m_t = jnp.maximum(a_col, jnp.max(dmat, axis=-1, keepdims=True))
        w_inter = jnp.exp(a_col - m_t)
        wmat = jnp.exp(dmat - m_t)
        qk = lax.dot_general(q, k, (((1,), (1,)), ((), ())), preferred_element_type=F32)
        wqk = wmat * qk
        c_old = cs[h]
        n_old = ns[h:h + 1, :]
        num = (w_inter * jnp.dot(q, c_old.astype(BF16), preferred_element_type=F32)
               + jnp.dot(wqk.astype(BF16), v, preferred_element_type=F32))
        den = (w_inter * jnp.sum(q.astype(F32) * n_old, axis=-1, keepdims=True)
               + jnp.sum(wqk, axis=-1, keepdims=True))
        h_ref[0, :, sl] = num / jnp.maximum(jnp.abs(den), jnp.exp(-m_t))
        m_new = m_t[L - 1:L, :]
        decay = jnp.exp(a_col[L - 1:L, :] - m_new)
        wl_col = jnp.exp(b_col[L - 1:L, :] - b_col + ig_col - m_new)
        kw = k.astype(F32) * wl_col
        cs[h] = decay * c_old + jnp.dot(_t128(kw).astype(BF16), v, preferred_element_type=F32)
        ns[h:h + 1, :] = decay * n_old + jnp.sum(kw, axis=0, keepdims=True)
        ms[h:h + 1, :] = jnp.broadcast_to(m_new, (1, ML_DK))

    @pl.when(c == pl.num_programs(1) - 1)
    def _():
        c_ref[0] = cs[...]
        n_ref[0] = ns[...]
        m_ref[0] = ms[...]


def _mlstm(q, k, v, gates, c0, n0, m0, *, L):
    B, S, _ = q.shape
    seq = lambda w: pl.BlockSpec((1, L, w), lambda b, c: (b, c, 0))
    st4 = pl.BlockSpec((1, ML_HEADS, ML_DK, ML_DK), lambda b, c: (b, 0, 0, 0))
    st3 = pl.BlockSpec((1, ML_HEADS, ML_DK), lambda b, c: (b, 0, 0))
    return pl.pallas_call(
        functools.partial(_mlstm_kernel, L=L),
        grid=(B, S // L),
        in_specs=[seq(ML_W), seq(ML_W), seq(ML_W), seq(GATE_COLS), st4, st3, st3],
        out_specs=[seq(ML_W), st4, st3, st3],
        out_shape=[jax.ShapeDtypeStruct((B, S, ML_W), F32),
                   jax.ShapeDtypeStruct((B, ML_HEADS, ML_DK, ML_DK), F32),
                   jax.ShapeDtypeStruct((B, ML_HEADS, ML_DK), F32),
                   jax.ShapeDtypeStruct((B, ML_HEADS, ML_DK), F32)],
        scratch_shapes=[pltpu.VMEM((ML_HEADS, ML_DK, ML_DK), F32),
                        pltpu.VMEM((ML_HEADS, ML_DK), F32),
                        pltpu.VMEM((ML_HEADS, ML_DK), F32)],
        compiler_params=_params("parallel", "arbitrary"),
        name="mlstm",
    )(q, k, v, gates, c0, n0, m0)


def _merge_kernel(x_ref, att_ref, h_ref, mo_ref, ga_ref, gb_ref, wa_ref, wm_ref, wo_ref, o_ref):
    ml = (jax.nn.sigmoid(mo_ref[...]) * h_ref[...]).astype(BF16)
    a = jnp.dot(att_ref[...], wa_ref[...], preferred_element_type=F32)
    b = jnp.dot(ml, wm_ref[...], preferred_element_type=F32)
    merged = jax.nn.sigmoid(ga_ref[...]) * a + jax.nn.sigmoid(gb_ref[...]) * b
    o_ref[...] = x_ref[...] + jnp.dot(merged.astype(BF16), wo_ref[...], preferred_element_type=F32)


def _merge(x, att, h, mo, ga, gb, wa, wm, wo):
    n = x.shape[0]
    tm = min(ROW_TILE, n)
    row = lambda w: pl.BlockSpec((tm, w), lambda i: (i, 0))
    full = lambda a: pl.BlockSpec(a.shape, lambda i: (0, 0))
    return pl.pallas_call(
        _merge_kernel,
        grid=(n // tm,),
        in_specs=[row(D_MODEL), row(ATT_W), row(ML_W), row(ML_W), row(D_MODEL), row(D_MODEL),
                  full(wa), full(wm), full(wo)],
        out_specs=row(D_MODEL),
        out_shape=jax.ShapeDtypeStruct((n, D_MODEL), F32),
        compiler_params=_params("parallel"),
        name="merge",
    )(x, att, h, mo, ga, gb, wa, wm, wo)


def _top16(s, ids, big):
    vals, win = [], []
    for _ in range(PEER_TOPK):
        m = jnp.max(s, axis=0, keepdims=True)
        i = jnp.min(jnp.where(s == m, ids, big), axis=0, keepdims=True)
        vals.append(m)
        win.append(i)
        s = jnp.where(ids == i, -jnp.inf, s)
    return jnp.concatenate(vals, axis=0), jnp.concatenate(win, axis=0)


_PAIR_IDS = np.array([b for b in range(16)] + [a * 16 + b for a in range(1, 8) for b in range(8)]
                     + [a * 16 for a in range(8, 16)], np.int32)


def _select(rank, table):
    out = jnp.zeros_like(table)
    for a in range(PEER_TOPK):
        out = jnp.where(rank == a, table[a:a + 1], out)
    return out


def _route_kernel(x_ref, g_ref, wq_ref, keys_ref, pair_ref, xn_o, xt_o, idx_o, gw_o, st_ref):
    x = x_ref[...]
    xn = _rms(x, g_ref[...])
    xn_o[...] = xn.reshape(xn_o.shape)
    xt_o[...] = x.reshape(xt_o.shape)
    q = jnp.dot(xn.astype(BF16), wq_ref[...], preferred_element_type=F32)
    st_ref[...] = lax.dot_general(keys_ref[...], q.astype(BF16), (((1,), (1,)), ((), ())),
                                  preferred_element_type=F32)
    lane_tiles = st_ref.shape[1] // LANES
    key_ids = lax.broadcasted_iota(jnp.int32, (PEER_NKEYS, LANES), 0)

    def head_tile(it, carry):
        h = it // lane_tiles
        lanes = pl.ds(pl.multiple_of((it % lane_tiles) * LANES, LANES), LANES)
        r0 = pl.multiple_of(h * 2 * PEER_NKEYS, 2 * PEER_NKEYS)
        sv0, si0 = _top16(st_ref[pl.ds(r0, PEER_NKEYS), lanes], key_ids, PEER_NKEYS)
        sv1, si1 = _top16(st_ref[pl.ds(r0 + PEER_NKEYS, PEER_NKEYS), lanes], key_ids, PEER_NKEYS)
        half = PEER_TOPK // 2
        cand = jnp.concatenate(
            [jnp.broadcast_to(sv0[0:1], (PEER_TOPK, LANES)) + sv1]
            + [jnp.broadcast_to(sv0[a:a + 1], (half, LANES)) + sv1[0:half] for a in range(1, half)]
            + [sv0[half:] + jnp.broadcast_to(sv1[0:1], (half, LANES))], axis=0)
        fv, fi = _top16(cand, pair_ref[...], PEER_TOPK * PEER_TOPK)
        eidx = _select(fi >> 4, si0) * PEER_NKEYS + _select(fi & (PEER_TOPK - 1), si1)
        e = jnp.exp(fv - fv[0:1])
        o0 = pl.multiple_of(h * PEER_TOPK, PEER_TOPK)
        gw_o[pl.ds(o0, PEER_TOPK), lanes] = e / jnp.sum(e, axis=0, keepdims=True)
        idx_o[pl.ds(o0, PEER_TOPK), lanes] = eidx
        return carry

    lax.fori_loop(0, PEER_HEADS * lane_tiles, head_tile, 0)


def _route(x, g2, wq, keys_bd):
    n = x.shape[0]
    tm = min(ROW_TILE, n)
    full = lambda a: pl.BlockSpec(a.shape, lambda i: (0, 0))
    pair_ids = jnp.asarray(np.broadcast_to(_PAIR_IDS[:, None], (_PAIR_IDS.size, LANES)))
    return pl.pallas_call(
        _route_kernel,
        grid=(n // tm,),
        scratch_shapes=[pltpu.VMEM((PEER_HEADS * 2 * PEER_NKEYS, tm), F32)],
        in_specs=[pl.BlockSpec((tm, D_MODEL), lambda i: (i, 0)), full(g2), full(wq), full(keys_bd), full(pair_ids)],
        out_specs=[pl.BlockSpec((tm, SUBLANES, LANES), lambda i: (i, 0, 0)),
                   pl.BlockSpec((tm, SUBLANES, LANES), lambda i: (i, 0, 0)),
                   pl.BlockSpec((PEER_PICKS, tm), lambda i: (0, i)),
                   pl.BlockSpec((PEER_PICKS, tm), lambda i: (0, i))],
        out_shape=[jax.ShapeDtypeStruct((n, SUBLANES, LANES), F32),
                   jax.ShapeDtypeStruct((n, SUBLANES, LANES), F32),
                   jax.ShapeDtypeStruct((PEER_PICKS, n), jnp.int32),
                   jax.ShapeDtypeStruct((PEER_PICKS, n), F32)],
        compiler_params=_params("parallel"),
        name="peer_route",
    )(x, g2, wq, keys_bd, pair_ids)


def _apply_kernel(idx_hbm, gwt_ref, xn_ref, x_ref, fg_ref, tab_hbm, y_ref, idx_s, buf, gsem, isem, coef_ref, *, nblk):
    i = pl.program_id(0)
    ngroups = PEER_BLOCK // GROUP
    rows = GROUP * PEER_PICKS
    blk_words = PEER_BLOCK * PEER_PICKS
    assert ngroups % GATHER_SLOTS == 0 and GATHER_SLOTS == 2

    def idx_copy(blk, islot):
        return pltpu.make_async_copy(idx_hbm.at[pl.ds(blk * blk_words, blk_words)],
                                     idx_s.at[pl.ds(islot * blk_words, blk_words)], isem.at[islot])

    def row_copy(e, slot, r):
        return pltpu.make_async_copy(tab_hbm.at[e], buf.at[slot, r], gsem.at[slot])

    def wait_group(slot):
        pltpu.make_async_copy(tab_hbm.at[pl.ds(0, rows)], buf.at[slot], gsem.at[slot]).wait()

    islot = i % 2
    has_next = i + 1 < nblk

    @pl.when(i == 0)
    def _():
        idx_copy(0, 0).start()
        idx_copy(0, 0).wait()

        def body(r, carry):
            row_copy(idx_s[r], 0, r).start()
            return carry

        lax.fori_loop(0, rows, body, 0, unroll=8)

    @pl.when(has_next)
    def _():
        idx_copy(i + 1, 1 - islot).start()

    lane = lax.broadcasted_iota(jnp.int32, (PEER_PICKS, LANES), 1)
    sub = lax.broadcasted_iota(jnp.int32, (SUBLANES, LANES), 0)
    masks = {1: (sub & 1) == 0, 2: (sub & 2) == 0, 4: (sub & 4) == 0}

    def fold(a, b, h):
        return jnp.where(masks[h], a, b) + pltpu.roll(jnp.where(masks[h], b, a), h, axis=0)

    def row_sums(ps):
        c = [fold(ps[2 * m], ps[2 * m + 1], 1) for m in range(4)]
        d = [fold(c[0], c[1], 2), fold(c[2], c[3], 2)]
        return fold(d[0], d[1], 4)

    def do_group(g, slot, next_word0):
        nslot = 1 - slot
        wait_group(slot)
        t0 = pl.multiple_of(g * GROUP, GROUP)
        lane0 = pl.multiple_of((t0 // LANES) * LANES, LANES)
        gw_tile = gwt_ref[:, pl.ds(lane0, LANES)]
        ys = []
        for j in range(GROUP):
            for k in range(PEER_PICKS):
                r = j * PEER_PICKS + k
                row_copy(idx_s[next_word0 + r], nslot, r).start(priority=k % 2)
            x = xn_ref[t0 + j]
            sums = []
            for c in range(PEER_PICKS // SUBLANES):
                r0 = j * PEER_PICKS + c * SUBLANES
                sums.append(row_sums([buf[slot, r0 + k, 0:SUBLANES, :] * x for k in range(SUBLANES)]))
            act = jnp.sum(jnp.concatenate(sums, axis=0), axis=-1, keepdims=True)
            gelu = 0.5 * act * (1.0 + lax.erf(act * (2.0 ** -0.5)))
            gw_col = jnp.sum(jnp.where(lane == t0 - lane0 + j, gw_tile, 0.0), axis=-1, keepdims=True)
            coef_ref[...] = jnp.broadcast_to(gw_col * gelu, (PEER_PICKS, LANES))
            accs = [jnp.zeros((SUBLANES, LANES), F32) for _ in range(4)]
            for k in range(PEER_PICKS):
                accs[k % 4] = accs[k % 4] + coef_ref[k:k + 1, :] * buf[slot, j * PEER_PICKS + k, SUBLANES:, :]
            y = x_ref[t0 + j] + ((accs[0] + accs[1]) + (accs[2] + accs[3]))
            ms = jnp.sum(jnp.sum(y * y, axis=0, keepdims=True), axis=-1, keepdims=True) * (1.0 / D_MODEL)
            ys.append(y * lax.rsqrt(ms + EPS) * fg_ref[...])
        y_ref[pl.ds(t0, GROUP), :] = jnp.stack(ys, axis=0).reshape(GROUP, D_MODEL)

    def pair_body(gg, carry):
        g = 2 * gg
        do_group(g, 0, islot * blk_words + (g + 1) * rows)
        last = gg + 1 == ngroups // 2

        @pl.when(jnp.logical_and(last, has_next))
        def _():
            idx_copy(i + 1, 1 - islot).wait()

        do_group(g + 1, 1, jnp.where(last, jnp.where(has_next, 1 - islot, islot) * blk_words,
                                     islot * blk_words + (g + 2) * rows))
        return carry

    lax.fori_loop(0, ngroups // 2, pair_body, 0)

    @pl.when(jnp.logical_not(has_next))
    def _():
        wait_group(0)


def _apply(idx, gwt, xn, x, fg, table):
    n = x.shape[0]
    nblk = n // PEER_BLOCK
    tile = pl.BlockSpec((PEER_BLOCK, SUBLANES, LANES), lambda i: (i, 0, 0))
    return pl.pallas_call(
        functools.partial(_apply_kernel, nblk=nblk),
        grid=(nblk,),
        in_specs=[pl.BlockSpec(memory_space=pl.ANY), pl.BlockSpec((PEER_PICKS, PEER_BLOCK), lambda i: (0, i)),
                  tile, tile,
                  pl.BlockSpec(fg.shape, lambda i: (0, 0)), pl.BlockSpec(memory_space=pl.ANY)],
        out_specs=pl.BlockSpec((PEER_BLOCK, D_MODEL), lambda i: (i, 0)),
        out_shape=jax.ShapeDtypeStruct((n, D_MODEL), F32),
        scratch_shapes=[pltpu.SMEM((2 * PEER_BLOCK * PEER_PICKS,), jnp.int32),
                        pltpu.VMEM((GATHER_SLOTS, GROUP * PEER_PICKS, EXPERT_ROWS, LANES), F32),
                        pltpu.SemaphoreType.DMA((GATHER_SLOTS,)),
                        pltpu.SemaphoreType.DMA((2,)),
                        pltpu.VMEM((PEER_PICKS, LANES), F32)],
        compiler_params=_params("arbitrary"),
        name="peer_apply",
    )(idx, gwt, xn, x, fg, table)


def _head_tile_steps(it, st_ref, pair_ids, gw_dst, idx_dst, lane_tiles):
    h = it // lane_tiles
    lanes = pl.ds(pl.multiple_of((it % lane_tiles) * LANES, LANES), LANES)
    r0 = pl.multiple_of(h * 2 * PEER_NKEYS, 2 * PEER_NKEYS)
    key_ids = lax.broadcasted_iota(jnp.int32, (PEER_NKEYS, LANES), 0)
    s = [st_ref[pl.ds(r0, PEER_NKEYS), lanes], st_ref[pl.ds(r0 + PEER_NKEYS, PEER_NKEYS), lanes]]
    vals, wins = ([], []), ([], [])

    def take_max(x, ids, big):
        m = jnp.max(x, axis=0, keepdims=True)
        w = jnp.min(jnp.where(x == m, ids, big), axis=0, keepdims=True)
        return m, w, jnp.where(ids == w, -jnp.inf, x)

    for _ in range(PEER_TOPK):
        for p in range(2):
            m, w, s[p] = take_max(s[p], key_ids, PEER_NKEYS)
            vals[p].append(m)
            wins[p].append(w)
        yield
    sv0, sv1 = jnp.concatenate(vals[0], axis=0), jnp.concatenate(vals[1], axis=0)
    si0, si1 = jnp.concatenate(wins[0], axis=0), jnp.concatenate(wins[1], axis=0)
    half = PEER_TOPK // 2
    cand = jnp.concatenate(
        [jnp.broadcast_to(sv0[0:1], (PEER_TOPK, LANES)) + sv1]
        + [jnp.broadcast_to(sv0[a:a + 1], (half, LANES)) + sv1[0:half] for a in range(1, half)]
        + [sv0[half:] + jnp.broadcast_to(sv1[0:1], (half, LANES))], axis=0)
    fvals, fwins = [], []
    for r in range(PEER_TOPK):
        m, w, cand = take_max(cand, pair_ids, PEER_TOPK * PEER_TOPK)
        fvals.append(m)
        fwins.append(w)
        if r % 2:
            yield
    fv, fi = jnp.concatenate(fvals, axis=0), jnp.concatenate(fwins, axis=0)
    eidx = _select(fi >> 4, si0) * PEER_NKEYS + _select(fi & (PEER_TOPK - 1), si1)
    e = jnp.exp(fv - fv[0:1])
    o0 = pl.multiple_of(h * PEER_TOPK, PEER_TOPK)
    gw_dst[pl.ds(o0, PEER_TOPK), lanes] = e / jnp.sum(e, axis=0, keepdims=True)
    idx_dst[pl.ds(o0, PEER_TOPK), lanes] = eidx
    yield


def _fused_kernel(x0_ref, x1_ref, xnext_ref, g_ref, wq_ref, keys_ref, pair_ref, fg_ref, tab_hbm, y_ref,
                  st_ref, xn_t, x_t, gw_v, idx_tv, idx_rows, idx_s, buf, gsem, isem, coef_ref, *, nblk):
    i = pl.program_id(0)
    ngroups = PEER_BLOCK // GROUP
    rows = GROUP * PEER_PICKS
    lane_tiles = PEER_BLOCK // LANES
    head_tiles = PEER_HEADS * lane_tiles
    assert ngroups // 2 == head_tiles and nblk >= 2
    islot = i % 2
    a3 = i % 3
    w3 = (i + 2) % 3
    has_next = i + 1 < nblk
    has_next2 = i + 2 < nblk
    pair_ids = pair_ref[...]

    def scores(x, slot):
        xn = _rms(x, g_ref[...])
        xn_t[slot] = xn.reshape(xn_t.shape[1:])
        x_t[slot] = x.reshape(x_t.shape[1:])
        q = jnp.dot(xn.astype(BF16), wq_ref[...], preferred_element_type=F32)
        st_ref[...] = lax.dot_general(keys_ref[...], q.astype(BF16), (((1,), (1,)), ((), ())),
                                      preferred_element_type=F32)

    def publish_copy(smem_slot):
        return pltpu.make_async_copy(idx_rows, idx_s.at[pl.ds(smem_slot * PEER_BLOCK, PEER_BLOCK), :],
                                     isem.at[smem_slot])

    def row_copy(e, slot, r):
        return pltpu.make_async_copy(tab_hbm.at[e], buf.at[slot, r], gsem.at[slot])

    def wait_group(slot):
        pltpu.make_async_copy(tab_hbm.at[pl.ds(0, rows)], buf.at[slot], gsem.at[slot]).wait()

    @pl.when(i == 0)
    def _():
        for blk, x_ref in ((0, x0_ref), (1, x1_ref)):
            scores(x_ref[...], blk)

            def body(it, carry):
                for _ in _head_tile_steps(it, st_ref, pair_ids, gw_v.at[blk], idx_tv, lane_tiles):
                    pass
                return carry

            lax.fori_loop(0, head_tiles, body, 0)
            idx_rows[...] = idx_tv[...].T
            publish_copy(blk).start()
            publish_copy(blk).wait()

        def issue(r, carry):
            row_copy(idx_s[r // PEER_PICKS, r % PEER_PICKS], 0, r).start()
            return carry

        lax.fori_loop(0, rows, issue, 0, unroll=8)

    @pl.when(has_next2)
    def _():
        scores(xnext_ref[...], w3)

    lane = lax.broadcasted_iota(jnp.int32, (PEER_PICKS, LANES), 1)
    sub = lax.broadcasted_iota(jnp.int32, (SUBLANES, LANES), 0)
    masks = {1: (sub & 1) == 0, 2: (sub & 2) == 0, 4: (sub & 4) == 0}

    def fold(a, b, h):
        return jnp.where(masks[h], a, b) + pltpu.roll(jnp.where(masks[h], b, a), h, axis=0)

    def row_sums(ps):
        c = [fold(ps[2 * m], ps[2 * m + 1], 1) for m in range(4)]
        d = [fold(c[0], c[1], 2), fold(c[2], c[3], 2)]
        return fold(d[0], d[1], 4)

    def do_group(g, slot, next_row0, between):
        nslot = 1 - slot
        wait_group(slot)
        t0 = pl.multiple_of(g * GROUP, GROUP)
        lane0 = pl.multiple_of((t0 // LANES) * LANES, LANES)
        gw_tile = gw_v[a3, :, pl.ds(lane0, LANES)]
        ys = []
        for j in range(GROUP):
            for k in range(PEER_PICKS):
                row_copy(idx_s[next_row0 + j, k], nslot, j * PEER_PICKS + k).start(priority=k % 2)
            x = xn_t[a3, t0 + j]
            sums = []
            for c in range(PEER_PICKS // SUBLANES):
                r0 = j * PEER_PICKS + c * SUBLANES
                sums.append(row_sums([buf[slot, r0 + k, 0:SUBLANES, :] * x for k in range(SUBLANES)]))
            act = jnp.sum(jnp.concatenate(sums, axis=0), axis=-1, keepdims=True)
            gelu = 0.5 * act * (1.0 + lax.erf(act * (2.0 ** -0.5)))
            gw_col = jnp.sum(jnp.where(lane == t0 - lane0 + j, gw_tile, 0.0), axis=-1, keepdims=True)
            coef_ref[...] = jnp.broadcast_to(gw_col * gelu, (PEER_PICKS, LANES))
            accs = [jnp.zeros((SUBLANES, LANES), F32) for _ in range(4)]
            for k in range(PEER_PICKS):
                accs[k % 4] = accs[k % 4] + coef_ref[k:k + 1, :] * buf[slot, j * PEER_PICKS + k, SUBLANES:, :]
            y = x_t[a3, t0 + j] + ((accs[0] + accs[1]) + (accs[2] + accs[3]))
            ms = jnp.sum(jnp.sum(y * y, axis=0, keepdims=True), axis=-1, keepdims=True) * (1.0 / D_MODEL)
            ys.append(y * lax.rsqrt(ms + EPS) * fg_ref[...])
            between()
        y_ref[pl.ds(t0, GROUP), :] = jnp.stack(ys, axis=0).reshape(GROUP, D_MODEL)

    def pair_body(gg, carry):
        g = 2 * gg
        last = gg + 1 == ngroups // 2

        @pl.when(jnp.logical_and(last, jnp.logical_and(has_next, i >= 1)))
        def _():
            publish_copy(1 - islot).wait()

        steps = _head_tile_steps(gg, st_ref, pair_ids, gw_v.at[w3], idx_tv, lane_tiles)

        def between():
            next(steps, None)
            next(steps, None)

        do_group(g, 0, islot * PEER_BLOCK + (g + 1) * GROUP, between)
        do_group(g + 1, 1, jnp.where(last, jnp.where(has_next, 1 - islot, islot) * PEER_BLOCK,
                                     islot * PEER_BLOCK + (g + 2) * GROUP), between)
        for _ in steps:
            pass
        return carry

    lax.fori_loop(0, ngroups // 2, pair_body, 0)

    @pl.when(has_next2)
    def _():
        idx_rows[...] = idx_tv[...].T
        publish_copy(islot).start()

    @pl.when(jnp.logical_not(has_next))
    def _():
        wait_group(0)


def _route_apply(x, g2, wq, keys_bd, fg, table):
    n = x.shape[0]
    nblk = n // PEER_BLOCK
    full = lambda a: pl.BlockSpec(a.shape, lambda i: (0,) * a.ndim)
    pair_ids = jnp.asarray(np.broadcast_to(_PAIR_IDS[:, None], (_PAIR_IDS.size, LANES)))
    blk = lambda f: pl.BlockSpec((PEER_BLOCK, D_MODEL), f)
    return pl.pallas_call(
        functools.partial(_fused_kernel, nblk=nblk),
        grid=(nblk,),
        in_specs=[blk(lambda i: (0, 0)), blk(lambda i: (1, 0)), blk(lambda i: (jnp.minimum(i + 2, nblk - 1), 0)),
                  full(g2), full(wq), full(keys_bd), full(pair_ids), full(fg), pl.BlockSpec(memory_space=pl.ANY)],
        out_specs=blk(lambda i: (i, 0)),
        out_shape=jax.ShapeDtypeStruct((n, D_MODEL), F32),
        scratch_shapes=[pltpu.VMEM((PEER_HEADS * 2 * PEER_NKEYS, PEER_BLOCK), F32),
                        pltpu.VMEM((3, PEER_BLOCK, SUBLANES, LANES), F32),
                        pltpu.VMEM((3, PEER_BLOCK, SUBLANES, LANES), F32),
                        pltpu.VMEM((3, PEER_PICKS, PEER_BLOCK), F32),
                        pltpu.VMEM((PEER_PICKS, PEER_BLOCK), jnp.int32),
                        pltpu.VMEM((PEER_BLOCK, PEER_PICKS), jnp.int32),
                        pltpu.SMEM((2 * PEER_BLOCK, PEER_PICKS), jnp.int32),
                        pltpu.VMEM((GATHER_SLOTS, GROUP * PEER_PICKS, EXPERT_ROWS, LANES), F32),
                        pltpu.SemaphoreType.DMA((GATHER_SLOTS,)),
                        pltpu.SemaphoreType.DMA((2,)),
                        pltpu.VMEM((PEER_PICKS, LANES), F32)],
        compiler_params=_params("arbitrary"),
        name="peer_route_apply",
    )(x, x, x, g2, wq, keys_bd, pair_ids, fg, table)


def _layer(x3, weights, att_kv, att_geom, ml_state, L):
    B, S, _ = x3.shape
    n = B * S
    x = x3.reshape(n, D_MODEL)
    (q, k, v, kf, vf, mq, mk, mv, mo, ga, gb, gates) = _inproj(
        x, weights["g1"], weights["w_main"], weights["b_main"], weights["wg_hi"], weights["wg_lo"],
        weights["bg"], weights["fb"])
    sh = lambda a: a.reshape(B, S, a.shape[-1])
    k_all, v_all = att_kv(sh(k), sh(v))
    att = _attention(sh(q), k_all, v_all, att_geom["bias"], L=att_geom["L"], KB=att_geom["KB"], pad=att_geom["pad"])
    h, c_new, n_new, m_new = _mlstm(sh(mq), sh(mk), sh(mv), sh(gates), *ml_state, L=L)
    x2 = _merge(x, att.reshape(n, ATT_W), h.reshape(n, ML_W), mo, ga, gb,
                weights["wa"], weights["wm"], weights["wo"])
    fg = weights["fg"].reshape(SUBLANES, LANES)
    if n >= 2 * PEER_BLOCK:
        y = _route_apply(x2, weights["g2"], weights["wq"], weights["keys_bd"], fg, weights["table"])
    else:
        xn2, x2_tiles, idx_t, gw_t = _route(x2, weights["g2"], weights["wq"], weights["keys_bd"])
        y = _apply(idx_t.T.reshape(n * PEER_PICKS), gw_t, xn2, x2_tiles, fg, weights["table"])
    return (y.reshape(B, S, D_MODEL), sh(kf).reshape(B, S, ATT_HEADS, ATT_HD), sh(vf).reshape(B, S, ATT_HEADS, ATT_HD),
            c_new, n_new, m_new[:, :, 0])


def kernel(x_prompt, x_sample, cache_att_k, cache_att_v, state_mlstm_C, state_mlstm_n, state_mlstm_m, norm1_g, w_in, b_in, ml_f_bias, att_rel_bias, w_att_branch, w_ml_branch, w_out, norm2_g, peer_wq, peer_sub_keys, peer_u, peer_v, final_g):
    depth = w_in.shape[0]
    assert depth == 1, "single-layer step"
    l = 0
    w = w_in[l]
    b = b_in[l]
    gate_lo, gate_hi = 7 * 512, 7 * 512 + 2 * ML_HEADS
    w_main = jnp.concatenate([w[:, :gate_lo], w[:, gate_hi:]], axis=1).astype(BF16)
    b_main = jnp.concatenate([b[:gate_lo], b[gate_hi:]])[None, :]
    wg = jnp.pad(w[:, gate_lo:gate_hi], ((0, 0), (0, GATE_COLS - 2 * ML_HEADS)))
    wg_hi = wg.astype(BF16)
    wg_lo = (wg - wg_hi.astype(F32)).astype(BF16)
    bg = jnp.pad(b[gate_lo:gate_hi], (0, GATE_COLS - 2 * ML_HEADS))[None, :]
    fb = jnp.pad(ml_f_bias[l], (ML_HEADS, GATE_COLS - 2 * ML_HEADS))[None, :]
    sk = peer_sub_keys[l].reshape(PEER_HEADS * 2, PEER_NKEYS, PEER_DKEY // 2)
    eye = jnp.eye(PEER_HEADS * 2, dtype=F32)
    keys_bd = (sk[:, :, None, :] * eye[:, None, :, None]).reshape(PEER_HEADS * 2 * PEER_NKEYS, D_MODEL).astype(BF16)
    weights = dict(
        g1=norm1_g[l][None, :], w_main=w_main, b_main=b_main, wg_hi=wg_hi, wg_lo=wg_lo, bg=bg, fb=fb,
        wa=w_att_branch[l].astype(BF16), wm=w_ml_branch[l].astype(BF16), wo=w_out[l].astype(BF16),
        g2=norm2_g[l][None, :], wq=peer_wq[l].astype(BF16), keys_bd=keys_bd, fg=final_g[None, :],
        table=jnp.concatenate([peer_u[l].reshape(-1, SUBLANES, LANES), peer_v[l].reshape(-1, SUBLANES, LANES)],
                              axis=1))

    Bs, T, _ = x_sample.shape
    P = cache_att_k.shape[2]
    assert P == ATT_CTX
    ck = cache_att_k[l].reshape(Bs, P, ATT_W).astype(BF16)
    cv = cache_att_v[l].reshape(Bs, P, ATT_W).astype(BF16)
    cat_kv = lambda k, v: (jnp.concatenate([ck, k], axis=1), jnp.concatenate([cv, v], axis=1))
    geom_s = dict(bias=_rel_bias(att_rel_bias[l], T, P + T), L=T, KB=P + T, pad=0)
    state_s = (state_mlstm_C[l], state_mlstm_n[l],
               jnp.broadcast_to(state_mlstm_m[l][:, :, None], (Bs, ML_HEADS, ML_DK)))
    y_s, k_s, v_s, c_s, n_s, m_s = _layer(x_sample, weights, cat_kv, geom_s, state_s, T)

    B, S, _ = x_prompt.shape
    pad_kv = lambda k, v: (jnp.pad(k, ((0, 0), (ATT_CTX, 0), (0, 0))), jnp.pad(v, ((0, 0), (ATT_CTX, 0), (0, 0))))
    geom_p = dict(bias=_pair_bias(_rel_bias(att_rel_bias[l], CHUNK, ATT_CTX + CHUNK)), L=2 * CHUNK,
                  KB=ATT_CTX + 2 * CHUNK, pad=ATT_CTX)
    state_p = (jnp.zeros((B, ML_HEADS, ML_DK, ML_DK), F32), jnp.zeros((B, ML_HEADS, ML_DK), F32),
               jnp.full((B, ML_HEADS, ML_DK), NEG, F32))
    y_p, k_p, v_p, c_p, n_p, m_p = _layer(x_prompt, weights, pad_kv, geom_p, state_p, CHUNK)
    keep = min(ATT_CTX, S)

    st = lambda a: a[None]
    return (y_p, y_s, st(k_p[:, -keep:]), st(v_p[:, -keep:]), st(c_p), st(n_p), st(m_p),
            st(k_s), st(v_s), st(c_s), st(n_s), st(m_s))
```

```python
import functools

import jax
import jax.numpy as jnp
import numpy as np
from jax import lax
from jax.experimental import pallas as pl
from jax.experimental.pallas import tpu as pltpu

F32 = jnp.float32
BF16 = jnp.bfloat16

D_MODEL = 1024
CHUNK = 64
ATT_HEADS = 8
ATT_HD = 64
ATT_W = ATT_HEADS * ATT_HD
ATT_CTX = 8 * CHUNK
MAX_REL = 128
ML_HEADS = 4
ML_DK = 128
ML_W = ML_HEADS * ML_DK
PEER_HEADS = 8
PEER_NKEYS = 128
PEER_DKEY = 128
PEER_TOPK = 16
PEER_PICKS = PEER_HEADS * PEER_TOPK
EPS = 1e-6
NEG = -1e30

LANES = 128
SUBLANES = 8
ROW_TILE = 256
GATE_COLS = LANES
MAIN_COLS = 7 * 512 + 2 * D_MODEL
ML_BATCH_ROWS = 2
GROUP = SUBLANES
PEER_BLOCK = 256
GATHER_SLOTS = 2
EXPERT_ROWS = 2 * D_MODEL // LANES
VMEM_LIMIT = 56 * 1024 * 1024


def _params(*sem):
    return pltpu.CompilerParams(dimension_semantics=sem, vmem_limit_bytes=VMEM_LIMIT)


def _rms(x, g):
    return x * lax.rsqrt(jnp.mean(x * x, axis=-1, keepdims=True) + EPS) * g


def _inproj_kernel(x_ref, g_ref, w_ref, b_ref, wgh_ref, wgl_ref, bg_ref, fb_ref,
                   q_o, k_o, v_o, kf_o, vf_o, mq_o, mk_o, mv_o, mo_o, ga_o, gb_o, gt_o):
    xn = _rms(x_ref[...], g_ref[...])
    xh = xn.astype(BF16)

    def proj(off, width):
        return jnp.dot(xh, w_ref[:, off:off + width], preferred_element_type=F32) + b_ref[:, off:off + width]

    q_o[...] = (proj(0, 512) * (ATT_HD ** -0.5)).astype(BF16)
    k = proj(512, 512)
    kf_o[...] = k
    k_o[...] = k.astype(BF16)
    v = proj(1024, 512)
    vf_o[...] = v
    v_o[...] = v.astype(BF16)
    mq_o[...] = proj(1536, 512).astype(BF16)
    mk_o[...] = (proj(2048, 512) * (ML_DK ** -0.5)).astype(BF16)
    mv_o[...] = proj(2560, 512).astype(BF16)
    mo_o[...] = proj(3072, 512)
    ga_o[...] = proj(3584, D_MODEL)
    gb_o[...] = proj(3584 + D_MODEL, D_MODEL)
    xl = (xn - xh.astype(F32)).astype(BF16)
    zg = (jnp.dot(xh, wgh_ref[...], preferred_element_type=F32)
          + jnp.dot(xh, wgl_ref[...], preferred_element_type=F32)
          + jnp.dot(xl, wgh_ref[...], preferred_element_type=F32)) + bg_ref[...]
    t = zg + fb_ref[...]
    log_sig = jnp.minimum(t, 0.0) - jnp.log1p(jnp.exp(-jnp.abs(t)))
    lane = lax.broadcasted_iota(jnp.int32, zg.shape, 1)
    gt_o[...] = jnp.where(lane < ML_HEADS, zg, log_sig)


def _inproj(x, g1, w_main, b_main, wg_hi, wg_lo, bg, fb):
    n = x.shape[0]
    tm = min(ROW_TILE, n)
    row = lambda w: pl.BlockSpec((tm, w), lambda i: (i, 0))
    full = lambda a: pl.BlockSpec(a.shape, lambda i: (0, 0))
    widths = [(512, BF16)] * 3 + [(512, F32)] * 2 + [(512, BF16)] * 3 + [(512, F32), (D_MODEL, F32), (D_MODEL, F32),
                                                                     (GATE_COLS, F32)]
    return pl.pallas_call(
        _inproj_kernel,
        grid=(n // tm,),
        in_specs=[row(D_MODEL), full(g1), full(w_main), full(b_main), full(wg_hi), full(wg_lo), full(bg), full(fb)],
        out_specs=[row(w) for w, _ in widths],
        out_shape=[jax.ShapeDtypeStruct((n, w), dt) for w, dt in widths],
        compiler_params=_params("parallel"),
        name="inproj",
    )(x, g1, w_main, b_main, wg_hi, wg_lo, bg, fb)


def _attn_kernel(q_ref, k_ref, v_ref, bias_ref, o_ref, *, L, KB, pad):
    c = pl.program_id(1)
    start = pl.multiple_of(c * L, L)
    kb = k_ref[0, pl.ds(start, KB), :]
    vb = v_ref[0, pl.ds(start, KB), :]
    q = q_ref[0].astype(F32)
    col = lax.broadcasted_iota(jnp.int32, (1, KB), 1)
    valid = (start + col) >= pad
    lo = lax.broadcasted_iota(jnp.int32, (L, LANES), 1) < ATT_HD
    for hp in range(ATT_HEADS // 2):
        sl = slice(hp * LANES, (hp + 1) * LANES)
        qp, kp, vp = q[:, sl], kb[:, sl], vb[:, sl]
        outs = []
        for half in range(2):
            qm = jnp.where(lo if half == 0 else jnp.logical_not(lo), qp, 0.0).astype(BF16)
            s = lax.dot_general(qm, kp, (((1,), (1,)), ((), ())), preferred_element_type=F32)
            s = s + bias_ref[hp * 2 + half]
            s = jnp.where(valid, s, NEG)
            e = jnp.exp(s - jnp.max(s, axis=-1, keepdims=True))
            outs.append(jnp.dot(e.astype(BF16), vp, preferred_element_type=F32)
                        / jnp.sum(e, axis=-1, keepdims=True))
        o_ref[0, :, sl] = jnp.where(lo, outs[0], outs[1]).astype(BF16)


def _attention(q, k, v, bias, *, L, KB, pad):
    B, S, _ = q.shape
    ktot = k.shape[1]
    return pl.pallas_call(
        functools.partial(_attn_kernel, L=L, KB=KB, pad=pad),
        grid=(B, S // L),
        in_specs=[pl.BlockSpec((1, L, ATT_W), lambda b, c: (b, c, 0)),
                  pl.BlockSpec((1, ktot, ATT_W), lambda b, c: (b, 0, 0)),
                  pl.BlockSpec((1, ktot, ATT_W), lambda b, c: (b, 0, 0)),
                  pl.BlockSpec(bias.shape, lambda b, c: (0, 0, 0))],
        out_specs=pl.BlockSpec((1, L, ATT_W), lambda b, c: (b, c, 0)),
        out_shape=jax.ShapeDtypeStruct((B, S, ATT_W), BF16),
        compiler_params=_params("parallel", "arbitrary"),
        name="attention",
    )(q, k, v, bias)


def _rel_bias(table, L, KB):
    d = np.arange(L)[:, None] + ATT_CTX - np.arange(KB)[None, :]
    return jnp.transpose(table[np.clip(d, -MAX_REL, MAX_REL) + MAX_REL], (2, 0, 1))


def _pair_bias(bias):
    h, l, _ = bias.shape
    neg = jnp.full((h, l, l), NEG, bias.dtype)
    return jnp.concatenate([jnp.concatenate([bias, neg], axis=2), jnp.concatenate([neg, bias], axis=2)], axis=1)


def _t128(x):
    rows = x.shape[0]
    if rows < LANES:
        x = jnp.concatenate([x, jnp.zeros((LANES - rows, LANES), x.dtype)], axis=0)
    return x.T[:, :rows]


def _mlstm_kernel(q_ref, k_ref, v_ref, g_ref, c0_ref, n0_ref, m0_ref,
                  h_ref, c_ref, n_ref, m_ref, cs, ns, ms, *, L):
    c = pl.program_id(1)

    @pl.when(c == 0)
    def _():
        cs[...] = c0_ref[...]
        ns[...] = n0_ref[...]
        ms[...] = m0_ref[...]

    tril = lax.broadcasted_iota(jnp.int32, (L, L), 0) >= lax.broadcasted_iota(jnp.int32, (L, L), 1)
    for bb in range(q_ref.shape[0]):
        gates = g_ref[bb]
        row = lax.broadcasted_iota(jnp.int32, gates.shape, 0)
        csum = gates
        sh = 1
        while sh < L:
            csum = csum + jnp.where(row >= sh, pltpu.roll(csum, sh, axis=0), 0.0)
            sh *= 2
        gates_t = _t128(gates)
        csum_t = _t128(csum)
        for h in range(ML_HEADS):
            sl = slice(h * ML_DK, (h + 1) * ML_DK)
            q, k, v = q_ref[bb, :, sl], k_ref[bb, :, sl], v_ref[bb, :, sl]
            b_col = csum[:, ML_HEADS + h:ML_HEADS + h + 1]
            b_row = csum_t[ML_HEADS + h:ML_HEADS + h + 1, :]
            ig_col = gates[:, h:h + 1]
            ig_row = gates_t[h:h + 1, :]
            m_prev = ms[bb, h:h + 1, 0:1]
            a_col = b_col + m_prev
            dmat = jnp.where(tril, b_col - b_row + ig_row, NEG)
            m_t = jnp.maximum(a_col, jnp.max(dmat, axis=-1, keepdims=True))
            w_inter = jnp.exp(a_col - m_t)
            wmat = jnp.exp(dmat - m_t)
            qk = lax.dot_general(q, k, (((1,), (1,)), ((), ())), preferred_element_type=F32)
            wqk = wmat * qk
            c_old = cs[bb, h]
            n_old = ns[bb, h:h + 1, :]
            num = (w_inter * jnp.dot(q, c_old.astype(BF16), preferred_element_type=F32)
                   + jnp.dot(wqk.astype(BF16), v, preferred_element_type=F32))
            den = (w_inter * jnp.sum(q.astype(F32) * n_old, axis=-1, keepdims=True)
                   + jnp.sum(wqk, axis=-1, keepdims=True))
            h_ref[bb, :, sl] = num / jnp.maximum(jnp.abs(den), jnp.exp(-m_t))
            m_new = m_t[L - 1:L, :]
            decay = jnp.exp(a_col[L - 1:L, :] - m_new)
            wl_col = jnp.exp(b_col[L - 1:L, :] - b_col + ig_col - m_new)
            kw = k.astype(F32) * wl_col
            cs[bb, h] = decay * c_old + jnp.dot(_t128(kw).astype(BF16), v, preferred_element_type=F32)
            ns[bb, h:h + 1, :] = decay * n_old + jnp.sum(kw, axis=0, keepdims=True)
            ms[bb, h:h + 1, :] = jnp.broadcast_to(m_new, (1, ML_DK))

    @pl.when(c == pl.num_programs(1) - 1)
    def _():
        c_ref[...] = cs[...]
        n_ref[...] = ns[...]
        m_ref[...] = ms[...]


def _mlstm(q, k, v, gates, c0, n0, m0, *, L):
    B, S, _ = q.shape
    bb = ML_BATCH_ROWS
    seq = lambda w: pl.BlockSpec((bb, L, w), lambda b, c: (b, c, 0))
    st4 = pl.BlockSpec((bb, ML_HEADS, ML_DK, ML_DK), lambda b, c: (b, 0, 0, 0))
    st3 = pl.BlockSpec((bb, ML_HEADS, ML_DK), lambda b, c: (b, 0, 0))
    return pl.pallas_call(
        functools.partial(_mlstm_kernel, L=L),
        grid=(B // bb, S // L),
        in_specs=[seq(ML_W), seq(ML_W), seq(ML_W), seq(GATE_COLS), st4, st3, st3],
        out_specs=[seq(ML_W), st4, st3, st3],
        out_shape=[jax.ShapeDtypeStruct((B, S, ML_W), F32),
                   jax.ShapeDtypeStruct((B, ML_HEADS, ML_DK, ML_DK), F32),
                   jax.ShapeDtypeStruct((B, ML_HEADS, ML_DK), F32),
                   jax.ShapeDtypeStruct((B, ML_HEADS, ML_DK), F32)],
        scratch_shapes=[pltpu.VMEM((bb, ML_HEADS, ML_DK, ML_DK), F32),
                        pltpu.VMEM((bb, ML_HEADS, ML_DK), F32),
                        pltpu.VMEM((bb, ML_HEADS, ML_DK), F32)],
        compiler_params=_params("parallel", "arbitrary"),
        name="mlstm",
    )(q, k, v, gates, c0, n0, m0)


def _merge_kernel(x_ref, att_ref, h_ref, mo_ref, ga_ref, gb_ref, wa_ref, wm_ref, wo_ref, o_ref):
    ml = (jax.nn.sigmoid(mo_ref[...]) * h_ref[...]).astype(BF16)
    a = jnp.dot(att_ref[...], wa_ref[...], preferred_element_type=F32)
    b = jnp.dot(ml, wm_ref[...], preferred_element_type=F32)
    merged = jax.nn.sigmoid(ga_ref[...]) * a + jax.nn.sigmoid(gb_ref[...]) * b
    o_ref[...] = x_ref[...] + jnp.dot(merged.astype(BF16), wo_ref[...], preferred_element_type=F32)


def _merge(x, att, h, mo, ga, gb, wa, wm, wo):
    n = x.shape[0]
    tm = min(ROW_TILE, n)
    row = lambda w: pl.BlockSpec((tm, w), lambda i: (i, 0))
    full = lambda a: pl.BlockSpec(a.shape, lambda i: (0, 0))
    return pl.pallas_call(
        _merge_kernel,
        grid=(n // tm,),
        in_specs=[row(D_MODEL), row(ATT_W), row(ML_W), row(ML_W), row(D_MODEL), row(D_MODEL),
                  full(wa), full(wm), full(wo)],
        out_specs=row(D_MODEL),
        out_shape=jax.ShapeDtypeStruct((n, D_MODEL), F32),
        compiler_params=_params("parallel"),
        name="merge",
    )(x, att, h, mo, ga, gb, wa, wm, wo)


def _top16(s, ids, big):
    vals, win = [], []
    for _ in range(PEER_TOPK):
        m = jnp.max(s, axis=0, keepdims=True)
        i = jnp.min(jnp.where(s == m, ids, big), axis=0, keepdims=True)
        vals.append(m)
        win.append(i)
        s = jnp.where(ids == i, -jnp.inf, s)
    return jnp.concatenate(vals, axis=0), jnp.concatenate(win, axis=0)


_PAIR_IDS = np.array([b for b in range(16)] + [a * 16 + b for a in range(1, 8) for b in range(8)]
                     + [a * 16 for a in range(8, 16)], np.int32)


def _select(rank, table):
    out = jnp.zeros_like(table)
    for a in range(PEER_TOPK):
        out = jnp.where(rank == a, table[a:a + 1], out)
    return out


def _route_kernel(x_ref, g_ref, wq_ref, keys_ref, pair_ref, xn_o, xt_o, idx_o, gw_o, st_ref):
    x = x_ref[...]
    xn = _rms(x, g_ref[...])
    xn_o[...] = xn.reshape(xn_o.shape)
    xt_o[...] = x.reshape(xt_o.shape)
    q = jnp.dot(xn.astype(BF16), wq_ref[...], preferred_element_type=F32)
    st_ref[...] = lax.dot_general(keys_ref[...], q.astype(BF16), (((1,), (1,)), ((), ())),
                                  preferred_element_type=F32)
    lane_tiles = st_ref.shape[1] // LANES
    key_ids = lax.broadcasted_iota(jnp.int32, (PEER_NKEYS, LANES), 0)

    def head_tile(it, carry):
        h = it // lane_tiles
        lanes = pl.ds(pl.multiple_of((it % lane_tiles) * LANES, LANES), LANES)
        r0 = pl.multiple_of(h * 2 * PEER_NKEYS, 2 * PEER_NKEYS)
        sv0, si0 = _top16(st_ref[pl.ds(r0, PEER_NKEYS), lanes], key_ids, PEER_NKEYS)
        sv1, si1 = _top16(st_ref[pl.ds(r0 + PEER_NKEYS, PEER_NKEYS), lanes], key_ids, PEER_NKEYS)
        half = PEER_TOPK // 2
        cand = jnp.concatenate(
            [jnp.broadcast_to(sv0[0:1], (PEER_TOPK, LANES)) + sv1]
            + [jnp.broadcast_to(sv0[a:a + 1], (half, LANES)) + sv1[0:half] for a in range(1, half)]
            + [sv0[half:] + jnp.broadcast_to(sv1[0:1], (half, LANES))], axis=0)
        fv, fi = _top16(cand, pair_ref[...], PEER_TOPK * PEER_TOPK)
        eidx = _select(fi >> 4, si0) * PEER_NKEYS + _select(fi & (PEER_TOPK - 1), si1)
        e = jnp.exp(fv - fv[0:1])
        o0 = pl.multiple_of(h * PEER_TOPK, PEER_TOPK)
        gw_o[pl.ds(o0, PEER_TOPK), lanes] = e / jnp.sum(e, axis=0, keepdims=True)
        idx_o[pl.ds(o0, PEER_TOPK), lanes] = eidx
        return carry

    lax.fori_loop(0, PEER_HEADS * lane_tiles, head_tile, 0)


def _route(x, g2, wq, keys_bd):
    n = x.shape[0]
    tm = min(ROW_TILE, n)
    full = lambda a: pl.BlockSpec(a.shape, lambda i: (0, 0))
    pair_ids = jnp.asarray(np.broadcast_to(_PAIR_IDS[:, None], (_PAIR_IDS.size, LANES)))
    return pl.pallas_call(
        _route_kernel,
        grid=(n // tm,),
        scratch_shapes=[pltpu.VMEM((PEER_HEADS * 2 * PEER_NKEYS, tm), F32)],
        in_specs=[pl.BlockSpec((tm, D_MODEL), lambda i: (i, 0)), full(g2), full(wq), full(keys_bd), full(pair_ids)],
        out_specs=[pl.BlockSpec((tm, SUBLANES, LANES), lambda i: (i, 0, 0)),
                   pl.BlockSpec((tm, SUBLANES, LANES), lambda i: (i, 0, 0)),
                   pl.BlockSpec((PEER_PICKS, tm), lambda i: (0, i)),
                   pl.BlockSpec((PEER_PICKS, tm), lambda i: (0, i))],
        out_shape=[jax.ShapeDtypeStruct((n, SUBLANES, LANES), F32),
                   jax.ShapeDtypeStruct((n, SUBLANES, LANES), F32),
                   jax.ShapeDtypeStruct((PEER_PICKS, n), jnp.int32),
                   jax.ShapeDtypeStruct((PEER_PICKS, n), F32)],
        compiler_params=_params("parallel"),
        name="peer_route",
    )(x, g2, wq, keys_bd, pair_ids)


def _apply_kernel(idx_hbm, gwt_ref, xn_ref, x_ref, fg_ref, tab_hbm, y_ref, idx_s, buf, gsem, isem, coef_ref, *, nblk):
    i = pl.program_id(0)
    ngroups = PEER_BLOCK // GROUP
    rows = GROUP * PEER_PICKS
    blk_words = PEER_BLOCK * PEER_PICKS
    assert ngroups % GATHER_SLOTS == 0 and GATHER_SLOTS == 2

    def idx_copy(blk, islot):
        return pltpu.make_async_copy(idx_hbm.at[pl.ds(blk * blk_words, blk_words)],
                                     idx_s.at[pl.ds(islot * blk_words, blk_words)], isem.at[islot])

    def row_copy(e, slot, r):
        return pltpu.make_async_copy(tab_hbm.at[e], buf.at[slot, r], gsem.at[slot])

    def wait_group(slot):
        pltpu.make_async_copy(tab_hbm.at[pl.ds(0, rows)], buf.at[slot], gsem.at[slot]).wait()

    islot = i % 2
    has_next = i + 1 < nblk

    @pl.when(i == 0)
    def _():
        idx_copy(0, 0).start()
        idx_copy(0, 0).wait()

        def body(r, carry):
            row_copy(idx_s[r], 0, r).start()
            return carry

        lax.fori_loop(0, rows, body, 0, unroll=8)

    @pl.when(has_next)
    def _():
        idx_copy(i + 1, 1 - islot).start()

    lane = lax.broadcasted_iota(jnp.int32, (PEER_PICKS, LANES), 1)
    sub = lax.broadcasted_iota(jnp.int32, (SUBLANES, LANES), 0)
    masks = {1: (sub & 1) == 0, 2: (sub & 2) == 0, 4: (sub & 4) == 0}

    def fold(a, b, h):
        return jnp.where(masks[h], a, b) + pltpu.roll(jnp.where(masks[h], b, a), h, axis=0)

    def row_sums(ps):
        c = [fold(ps[2 * m], ps[2 * m + 1], 1) for m in range(4)]
        d = [fold(c[0], c[1], 2), fold(c[2], c[3], 2)]
        return fold(d[0], d[1], 4)

    def do_group(g, slot, next_word0):
        nslot = 1 - slot
        wait_group(slot)
        t0 = pl.multiple_of(g * GROUP, GROUP)
        lane0 = pl.multiple_of((t0 // LANES) * LANES, LANES)
        gw_tile = gwt_ref[:, pl.ds(lane0, LANES)]
        ys = []
        for j in range(GROUP):
            for k in range(PEER_PICKS):
                r = j * PEER_PICKS + k
                row_copy(idx_s[next_word0 + r], nslot, r).start(priority=k % 2)
            x = xn_ref[t0 + j]
            sums = []
            for c in range(PEER_PICKS // SUBLANES):
                r0 = j * PEER_PICKS + c * SUBLANES
                sums.append(row_sums([buf[slot, r0 + k, 0:SUBLANES, :] * x for k in range(SUBLANES)]))
            act = jnp.sum(jnp.concatenate(sums, axis=0), axis=-1, keepdims=True)
            gelu = 0.5 * act * (1.0 + lax.erf(act * (2.0 ** -0.5)))
            gw_col = jnp.sum(jnp.where(lane == t0 - lane0 + j, gw_tile, 0.0), axis=-1, keepdims=True)
            coef_ref[...] = jnp.broadcast_to(gw_col * gelu, (PEER_PICKS, LANES))
            accs = [jnp.zeros((SUBLANES, LANES), F32) for _ in range(4)]
            for k in range(PEER_PICKS):
                accs[k % 4] = accs[k % 4] + coef_ref[k:k + 1, :] * buf[slot, j * PEER_PICKS + k, SUBLANES:, :]
            y = x_ref[t0 + j] + ((accs[0] + accs[1]) + (accs[2] + accs[3]))
            ms = jnp.sum(jnp.sum(y * y, axis=0, keepdims=True), axis=-1, keepdims=True) * (1.0 / D_MODEL)
            ys.append(y * lax.rsqrt(ms + EPS) * fg_ref[...])
        y_ref[pl.ds(t0, GROUP), :] = jnp.stack(ys, axis=0).reshape(GROUP, D_MODEL)

    def pair_body(gg, carry):
        g = 2 * gg
        do_group(g, 0, islot * blk_words + (g + 1) * rows)
        last = gg + 1 == ngroups // 2

        @pl.when(jnp.logical_and(last, has_next))
        def _():
            idx_copy(i + 1, 1 - islot).wait()

        do_group(g + 1, 1, jnp.where(last, jnp.where(has_next, 1 - islot, islot) * blk_words,
                                     islot * blk_words + (g + 2) * rows))
        return carry

    lax.fori_loop(0, ngroups // 2, pair_body, 0)

    @pl.when(jnp.logical_not(has_next))
    def _():
        wait_group(0)


def _apply(idx, gwt, xn, x, fg, table):
    n = x.shape[0]
    nblk = n // PEER_BLOCK
    tile = pl.BlockSpec((PEER_BLOCK, SUBLANES, LANES), lambda i: (i, 0, 0))
    return pl.pallas_call(
        functools.partial(_apply_kernel, nblk=nblk),
        grid=(nblk,),
        in_specs=[pl.BlockSpec(memory_space=pl.ANY), pl.BlockSpec((PEER_PICKS, PEER_BLOCK), lambda i: (0, i)),
                  tile, tile,
                  pl.BlockSpec(fg.shape, lambda i: (0, 0)), pl.BlockSpec(memory_space=pl.ANY)],
        out_specs=pl.BlockSpec((PEER_BLOCK, D_MODEL), lambda i: (i, 0)),
        out_shape=jax.ShapeDtypeStruct((n, D_MODEL), F32),
        scratch_shapes=[pltpu.SMEM((2 * PEER_BLOCK * PEER_PICKS,), jnp.int32),
                        pltpu.VMEM((GATHER_SLOTS, GROUP * PEER_PICKS, EXPERT_ROWS, LANES), F32),
                        pltpu.SemaphoreType.DMA((GATHER_SLOTS,)),
                        pltpu.SemaphoreType.DMA((2,)),
                        pltpu.VMEM((PEER_PICKS, LANES), F32)],
        compiler_params=_params("arbitrary"),
        name="peer_apply",
    )(idx, gwt, xn, x, fg, table)


def _head_tile_steps(it, st_ref, pair_ids, gw_dst, idx_dst, lane_tiles):
    h = it // lane_tiles
    lanes = pl.ds(pl.multiple_of((it % lane_tiles) * LANES, LANES), LANES)
    r0 = pl.multiple_of(h * 2 * PEER_NKEYS, 2 * PEER_NKEYS)
    key_ids = lax.broadcasted_iota(jnp.int32, (PEER_NKEYS, LANES), 0)
    s = [st_ref[pl.ds(r0, PEER_NKEYS), lanes], st_ref[pl.ds(r0 + PEER_NKEYS, PEER_NKEYS), lanes]]
    vals, wins = ([], []), ([], [])

    def take_max(x, ids, big):
        m = jnp.max(x, axis=0, keepdims=True)
        w = jnp.min(jnp.where(x == m, ids, big), axis=0, keepdims=True)
        return m, w, jnp.where(ids == w, -jnp.inf, x)

    for _ in range(PEER_TOPK):
        for p in range(2):
            m, w, s[p] = take_max(s[p], key_ids, PEER_NKEYS)
            vals[p].append(m)
            wins[p].append(w)
        yield
    sv0, sv1 = jnp.concatenate(vals[0], axis=0), jnp.concatenate(vals[1], axis=0)
    si0, si1 = jnp.concatenate(wins[0], axis=0), jnp.concatenate(wins[1], axis=0)
    half = PEER_TOPK // 2
    cand = jnp.concatenate(
        [jnp.broadcast_to(sv0[0:1], (PEER_TOPK, LANES)) + sv1]
        + [jnp.broadcast_to(sv0[a:a + 1], (half, LANES)) + sv1[0:half] for a in range(1, half)]
        + [sv0[half:] + jnp.broadcast_to(sv1[0:1], (half, LANES))], axis=0)
    fvals, fwins = [], []
    for r in range(PEER_TOPK):
        m, w, cand = take_max(cand, pair_ids, PEER_TOPK * PEER_TOPK)
        fvals.append(m)
        fwins.append(w)
        if r % 2:
            yield
    fv, fi = jnp.concatenate(fvals, axis=0), jnp.concatenate(fwins, axis=0)
    eidx = _select(fi >> 4, si0) * PEER_NKEYS + _select(fi & (PEER_TOPK - 1), si1)
    e = jnp.exp(fv - fv[0:1])
    o0 = pl.multiple_of(h * PEER_TOPK, PEER_TOPK)
    gw_dst[pl.ds(o0, PEER_TOPK), lanes] = e / jnp.sum(e, axis=0, keepdims=True)
    idx_dst[pl.ds(o0, PEER_TOPK), lanes] = eidx
    yield


def _fused_kernel(x0_ref, x1_ref, xnext_ref, g_ref, wq_ref, keys_ref, pair_ref, fg_ref, tab_hbm, y_ref,
                  st_ref, xn_t, x_t, gw_v, idx_tv, idx_rows, idx_s, buf, gsem, isem, coef_ref, *, nblk):
    i = pl.program_id(0)
    ngroups = PEER_BLOCK // GROUP
    rows = GROUP * PEER_PICKS
    lane_tiles = PEER_BLOCK // LANES
    head_tiles = PEER_HEADS * lane_tiles
    assert ngroups // 2 == head_tiles and nblk >= 2
    islot = i % 2
    a3 = i % 3
    w3 = (i + 2) % 3
    has_next = i + 1 < nblk
    has_next2 = i + 2 < nblk
    pair_ids = pair_ref[...]

    def scores(x, slot):
        xn = _rms(x, g_ref[...])
        xn_t[slot] = xn.reshape(xn_t.shape[1:])
        x_t[slot] = x.reshape(x_t.shape[1:])
        q = jnp.dot(xn.astype(BF16), wq_ref[...], preferred_element_type=F32)
        st_ref[...] = lax.dot_general(keys_ref[...], q.astype(BF16), (((1,), (1,)), ((), ())),
                                      preferred_element_type=F32)

    def publish_copy(smem_slot):
        return pltpu.make_async_copy(idx_rows, idx_s.at[pl.ds(smem_slot * PEER_BLOCK, PEER_BLOCK), :],
                                     isem.at[smem_slot])

    def row_copy(e, slot, r):
        return pltpu.make_async_copy(tab_hbm.at[e], buf.at[slot, r], gsem.at[slot])

    def wait_group(slot):
        pltpu.make_async_copy(tab_hbm.at[pl.ds(0, rows)], buf.at[slot], gsem.at[slot]).wait()

    @pl.when(i == 0)
    def _():
        for blk, x_ref in ((0, x0_ref), (1, x1_ref)):
            scores(x_ref[...], blk)

            def body(it, carry):
                for _ in _head_tile_steps(it, st_ref, pair_ids, gw_v.at[blk], idx_tv, lane_tiles):
                    pass
                return carry

            lax.fori_loop(0, head_tiles, body, 0)
            idx_rows[...] = idx_tv[...].T
            publish_copy(blk).start()
            publish_copy(blk).wait()

        def issue(r, carry):
            row_copy(idx_s[r // PEER_PICKS, r % PEER_PICKS], 0, r).start()
            return carry

        lax.fori_loop(0, rows, issue, 0, unroll=8)

    @pl.when(has_next2)
    def _():
        scores(xnext_ref[...], w3)

    lane = lax.broadcasted_iota(jnp.int32, (PEER_PICKS, LANES), 1)
    sub = lax.broadcasted_iota(jnp.int32, (SUBLANES, LANES), 0)
    masks = {1: (sub & 1) == 0, 2: (sub & 2) == 0, 4: (sub & 4) == 0}

    def fold(a, b, h):
        return jnp.where(masks[h], a, b) + pltpu.roll(jnp.where(masks[h], b, a), h, axis=0)

    def row_sums(ps):
        c = [fold(ps[2 * m], ps[2 * m + 1], 1) for m in range(4)]
        d = [fold(c[0], c[1], 2), fold(c[2], c[3], 2)]
        return fold(d[0], d[1], 4)

    def do_group(g, slot, next_row0, between):
        nslot = 1 - slot

        def issue_token(j):
            for k in range(PEER_PICKS):
                row_copy(idx_s[next_row0 + j, k], nslot, j * PEER_PICKS + k).start(priority=k % 2)

        issue_token(0)
        wait_group(slot)
        t0 = pl.multiple_of(g * GROUP, GROUP)
        lane0 = pl.multiple_of((t0 // LANES) * LANES, LANES)
        gw_tile = gw_v[a3, :, pl.ds(lane0, LANES)]
        ys = []
        for j in range(GROUP):
            if j + 1 < GROUP:
                issue_token(j + 1)
            x = xn_t[a3, t0 + j]
            sums = []
            for c in range(PEER_PICKS // SUBLANES):
                r0 = j * PEER_PICKS + c * SUBLANES
                sums.append(row_sums([buf[slot, r0 + k, 0:SUBLANES, :] * x for k in range(SUBLANES)]))
            act = jnp.sum(jnp.concatenate(sums, axis=0), axis=-1, keepdims=True)
            gelu = 0.5 * act * (1.0 + lax.erf(act * (2.0 ** -0.5)))
            gw_col = jnp.sum(jnp.where(lane == t0 - lane0 + j, gw_tile, 0.0), axis=-1, keepdims=True)
            coef_ref[...] = jnp.broadcast_to(gw_col * gelu, (PEER_PICKS, LANES))
            accs = [jnp.zeros((SUBLANES, LANES), F32) for _ in range(4)]
            for k in range(PEER_PICKS):
                accs[k % 4] = accs[k % 4] + coef_ref[k:k + 1, :] * buf[slot, j * PEER_PICKS + k, SUBLANES:, :]
            y = x_t[a3, t0 + j] + ((accs[0] + accs[1]) + (accs[2] + accs[3]))
            ms = jnp.sum(jnp.sum(y * y, axis=0, keepdims=True), axis=-1, keepdims=True) * (1.0 / D_MODEL)
            ys.append(y * lax.rsqrt(ms + EPS) * fg_ref[...])
            between()
        y_ref[pl.ds(t0, GROUP), :] = jnp.stack(ys, axis=0).reshape(GROUP, D_MODEL)

    def pair_body(gg, carry):
        g = 2 * gg
        last = gg + 1 == ngroups // 2

        @pl.when(jnp.logical_and(last, jnp.logical_and(has_next, i >= 1)))
        def _():
            publish_copy(1 - islot).wait()

        steps = _head_tile_steps(gg, st_ref, pair_ids, gw_v.at[w3], idx_tv, lane_tiles)

        def between():
            next(steps, None)
            next(steps, None)

        do_group(g, 0, islot * PEER_BLOCK + (g + 1) * GROUP, between)
        do_group(g + 1, 1, jnp.where(last, jnp.where(has_next, 1 - islot, islot) * PEER_BLOCK,
                                     islot * PEER_BLOCK + (g + 2) * GROUP), between)
        for _ in steps:
            pass
        return carry

    lax.fori_loop(0, ngroups // 2, pair_body, 0)

    @pl.when(has_next2)
    def _():
        idx_rows[...] = idx_tv[...].T
        publish_copy(islot).start()

    @pl.when(jnp.logical_not(has_next))
    def _():
        wait_group(0)


def _route_apply(x, g2, wq, keys_bd, fg, table):
    n = x.shape[0]
    nblk = n // PEER_BLOCK
    full = lambda a: pl.BlockSpec(a.shape, lambda i: (0,) * a.ndim)
    pair_ids = jnp.asarray(np.broadcast_to(_PAIR_IDS[:, None], (_PAIR_IDS.size, LANES)))
    blk = lambda f: pl.BlockSpec((PEER_BLOCK, D_MODEL), f)
    return pl.pallas_call(
        functools.partial(_fused_kernel, nblk=nblk),
        grid=(nblk,),
        in_specs=[blk(lambda i: (0, 0)), blk(lambda i: (1, 0)), blk(lambda i: (jnp.minimum(i + 2, nblk - 1), 0)),
                  full(g2), full(wq), full(keys_bd), full(pair_ids), full(fg), pl.BlockSpec(memory_space=pl.ANY)],
        out_specs=blk(lambda i: (i, 0)),
        out_shape=jax.ShapeDtypeStruct((n, D_MODEL), F32),
        scratch_shapes=[pltpu.VMEM((PEER_HEADS * 2 * PEER_NKEYS, PEER_BLOCK), F32),
                        pltpu.VMEM((3, PEER_BLOCK, SUBLANES, LANES), F32),
                        pltpu.VMEM((3, PEER_BLOCK, SUBLANES, LANES), F32),
                        pltpu.VMEM((3, PEER_PICKS, PEER_BLOCK), F32),
                        pltpu.VMEM((PEER_PICKS, PEER_BLOCK), jnp.int32),
                        pltpu.VMEM((PEER_BLOCK, PEER_PICKS), jnp.int32),
                        pltpu.SMEM((2 * PEER_BLOCK, PEER_PICKS), jnp.int32),
                        pltpu.VMEM((GATHER_SLOTS, GROUP * PEER_PICKS, EXPERT_ROWS, LANES), F32),
                        pltpu.SemaphoreType.DMA((GATHER_SLOTS,)),
                        pltpu.SemaphoreType.DMA((2,)),
                        pltpu.VMEM((PEER_PICKS, LANES), F32)],
        compiler_params=_params("arbitrary"),
        name="peer_route_apply",
    )(x, x, x, g2, wq, keys_bd, pair_ids, fg, table)


def _layer(x3, weights, att_kv, att_geom, ml_state, L):
    B, S, _ = x3.shape
    n = B * S
    x = x3.reshape(n, D_MODEL)
    (q, k, v, kf, vf, mq, mk, mv, mo, ga, gb, gates) = _inproj(
        x, weights["g1"], weights["w_main"], weights["b_main"], weights["wg_hi"], weights["wg_lo"],
        weights["bg"], weights["fb"])
    sh = lambda a: a.reshape(B, S, a.shape[-1])
    k_all, v_all = att_kv(sh(k), sh(v))
    att = _attention(sh(q), k_all, v_all, att_geom["bias"], L=att_geom["L"], KB=att_geom["KB"], pad=att_geom["pad"])
    h, c_new, n_new, m_new = _mlstm(sh(mq), sh(mk), sh(mv), sh(gates), *ml_state, L=L)
    x2 = _merge(x, att.reshape(n, ATT_W), h.reshape(n, ML_W), mo, ga, gb,
                weights["wa"], weights["wm"], weights["wo"])
    fg = weights["fg"].reshape(SUBLANES, LANES)
    if n >= 2 * PEER_BLOCK:
        y = _route_apply(x2, weights["g2"], weights["wq"], weights["keys_bd"], fg, weights["table"])
    else:
        xn2, x2_tiles, idx_t, gw_t = _route(x2, weights["g2"], weights["wq"], weights["keys_bd"])
        y = _apply(idx_t.T.reshape(n * PEER_PICKS), gw_t, xn2, x2_tiles, fg, weights["table"])
    return (y.reshape(B, S, D_MODEL), sh(kf).reshape(B, S, ATT_HEADS, ATT_HD), sh(vf).reshape(B, S, ATT_HEADS, ATT_HD),
            c_new, n_new, m_new[:, :, 0])


def kernel(x_prompt, x_sample, cache_att_k, cache_att_v, state_mlstm_C, state_mlstm_n, state_mlstm_m, norm1_g, w_in, b_in, ml_f_bias, att_rel_bias, w_att_branch, w_ml_branch, w_out, norm2_g, peer_wq, peer_sub_keys, peer_u, peer_v, final_g):
    depth = w_in.shape[0]
    assert depth == 1, "single-layer step"
    l = 0
    w = w_in[l]
    b = b_in[l]
    gate_lo, gate_hi = 7 * 512, 7 * 512 + 2 * ML_HEADS
    w_main = jnp.concatenate([w[:, :gate_lo], w[:, gate_hi:]], axis=1).astype(BF16)
    b_main = jnp.concatenate([b[:gate_lo], b[gate_hi:]])[None, :]
    wg = jnp.pad(w[:, gate_lo:gate_hi], ((0, 0), (0, GATE_COLS - 2 * ML_HEADS)))
    wg_hi = wg.astype(BF16)
    wg_lo = (wg - wg_hi.astype(F32)).astype(BF16)
    bg = jnp.pad(b[gate_lo:gate_hi], (0, GATE_COLS - 2 * ML_HEADS))[None, :]
    fb = jnp.pad(ml_f_bias[l], (ML_HEADS, GATE_COLS - 2 * ML_HEADS))[None, :]
    sk = peer_sub_keys[l].reshape(PEER_HEADS * 2, PEER_NKEYS, PEER_DKEY // 2)
    eye = jnp.eye(PEER_HEADS * 2, dtype=F32)
    keys_bd = (sk[:, :, None, :] * eye[:, None, :, None]).reshape(PEER_HEADS * 2 * PEER_NKEYS, D_MODEL).astype(BF16)
    weights = dict(
        g1=norm1_g[l][None, :], w_main=w_main, b_main=b_main, wg_hi=wg_hi, wg_lo=wg_lo, bg=bg, fb=fb,
        wa=w_att_branch[l].astype(BF16), wm=w_ml_branch[l].astype(BF16), wo=w_out[l].astype(BF16),
        g2=norm2_g[l][None, :], wq=peer_wq[l].astype(BF16), keys_bd=keys_bd, fg=final_g[None, :],
        table=jnp.concatenate([peer_u[l].reshape(-1, SUBLANES, LANES), peer_v[l].reshape(-1, SUBLANES, LANES)],
                              axis=1))

    Bs, T, _ = x_sample.shape
    P = cache_att_k.shape[2]
    assert P == ATT_CTX
    ck = cache_att_k[l].reshape(Bs, P, ATT_W).astype(BF16)
    cv = cache_att_v[l].reshape(Bs, P, ATT_W).astype(BF16)
    cat_kv = lambda k, v: (jnp.concatenate([ck, k], axis=1), jnp.concatenate([cv, v], axis=1))
    geom_s = dict(bias=_rel_bias(att_rel_bias[l], T, P + T), L=T, KB=P + T, pad=0)
    state_s = (state_mlstm_C[l], state_mlstm_n[l],
               jnp.broadcast_to(state_mlstm_m[l][:, :, None], (Bs, ML_HEADS, ML_DK)))
    y_s, k_s, v_s, c_s, n_s, m_s = _layer(x_sample, weights, cat_kv, geom_s, state_s, T)

    B, S, _ = x_prompt.shape
    pad_kv = lambda k, v: (jnp.pad(k, ((0, 0), (ATT_CTX, 0), (0, 0))), jnp.pad(v, ((0, 0), (ATT_CTX, 0), (0, 0))))
    geom_p = dict(bias=_pair_bias(_rel_bias(att_rel_bias[l], CHUNK, ATT_CTX + CHUNK)), L=2 * CHUNK,
                  KB=ATT_CTX + 2 * CHUNK, pad=ATT_CTX)
    state_p = (jnp.zeros((B, ML_HEADS, ML_DK, ML_DK), F32), jnp.zeros((B, ML_HEADS, ML_DK), F32),
               jnp.full((B, ML_HEADS, ML_DK), NEG, F32))
    y_p, k_p, v_p, c_p, n_p, m_p = _layer(x_prompt, weights, pad_kv, geom_p, state_p, CHUNK)
    keep = min(ATT_CTX, S)

    st = lambda a: a[None]
    return (y_p, y_s, st(k_p[:, -keep:]), st(v_p[:, -keep:]), st(c_p), st(n_p), st(m_p),
            st(k_s), st(v_s), st(c_s), st(n_s), st(m_s))
```

```python
import functools

import jax
import jax.numpy as jnp
import numpy as np
from jax import lax
from jax.experimental import pallas as pl
from jax.experimental.pallas import tpu as pltpu

F32 = jnp.float32
BF16 = jnp.bfloat16

D_MODEL = 1024
CHUNK = 64
ATT_HEADS = 8
ATT_HD = 64
ATT_W = ATT_HEADS * ATT_HD
ATT_CTX = 8 * CHUNK
MAX_REL = 128
ML_HEADS = 4
ML_DK = 128
ML_W = ML_HEADS * ML_DK
PEER_HEADS = 8
PEER_NKEYS = 128
PEER_DKEY = 128
PEER_TOPK = 16
PEER_PICKS = PEER_HEADS * PEER_TOPK
EPS = 1e-6
NEG = -1e30

LANES = 128
SUBLANES = 8
ROW_TILE = 256
GATE_COLS = LANES
MAIN_COLS = 7 * 512 + 2 * D_MODEL
ML_BATCH_ROWS = 2
GROUP = SUBLANES
PEER_BLOCK = 256
GATHER_SLOTS = 2
EXPERT_ROWS = 2 * D_MODEL // LANES
VMEM_LIMIT = 56 * 1024 * 1024


def _params(*sem):
    return pltpu.CompilerParams(dimension_semantics=sem, vmem_limit_bytes=VMEM_LIMIT)


def _rms(x, g):
    return x * lax.rsqrt(jnp.mean(x * x, axis=-1, keepdims=True) + EPS) * g


def _inproj_kernel(x_ref, g_ref, w_ref, b_ref, wgh_ref, wgl_ref, bg_ref, fb_ref,
                   q_o, k_o, v_o, kf_o, vf_o, mq_o, mk_o, mv_o, mo_o, ga_o, gb_o, gt_o):
    xn = _rms(x_ref[...], g_ref[...])
    xh = xn.astype(BF16)

    def proj(off, width):
        return jnp.dot(xh, w_ref[:, off:off + width], preferred_element_type=F32) + b_ref[:, off:off + width]

    q_o[...] = (proj(0, 512) * (ATT_HD ** -0.5)).astype(BF16)
    k = proj(512, 512)
    kf_o[...] = k
    k_o[...] = k.astype(BF16)
    v = proj(1024, 512)
    vf_o[...] = v
    v_o[...] = v.astype(BF16)
    mq_o[...] = proj(1536, 512).astype(BF16)
    mk_o[...] = (proj(2048, 512) * (ML_DK ** -0.5)).astype(BF16)
    mv_o[...] = proj(2560, 512).astype(BF16)
    mo_o[...] = proj(3072, 512)
    ga_o[...] = proj(3584, D_MODEL)
    gb_o[...] = proj(3584 + D_MODEL, D_MODEL)
    xl = (xn - xh.astype(F32)).astype(BF16)
    zg = (jnp.dot(xh, wgh_ref[...], preferred_element_type=F32)
          + jnp.dot(xh, wgl_ref[...], preferred_element_type=F32)
          + jnp.dot(xl, wgh_ref[...], preferred_element_type=F32)) + bg_ref[...]
    t = zg + fb_ref[...]
    log_sig = jnp.minimum(t, 0.0) - jnp.log1p(jnp.exp(-jnp.abs(t)))
    lane = lax.broadcasted_iota(jnp.int32, zg.shape, 1)
    gt_o[...] = jnp.where(lane < ML_HEADS, zg, log_sig)


def _inproj(x, g1, w_main, b_main, wg_hi, wg_lo, bg, fb):
    n = x.shape[0]
    tm = min(ROW_TILE, n)
    row = lambda w: pl.BlockSpec((tm, w), lambda i: (i, 0))
    full = lambda a: pl.BlockSpec(a.shape, lambda i: (0, 0))
    widths = [(512, BF16)] * 3 + [(512, F32)] * 2 + [(512, BF16)] * 3 + [(512, F32), (D_MODEL, F32), (D_MODEL, F32),
                                                                     (GATE_COLS, F32)]
    return pl.pallas_call(
        _inproj_kernel,
        grid=(n // tm,),
        in_specs=[row(D_MODEL), full(g1), full(w_main), full(b_main), full(wg_hi), full(wg_lo), full(bg), full(fb)],
        out_specs=[row(w) for w, _ in widths],
        out_shape=[jax.ShapeDtypeStruct((n, w), dt) for w, dt in widths],
        compiler_params=_params("parallel"),
        name="inproj",
    )(x, g1, w_main, b_main, wg_hi, wg_lo, bg, fb)


def _attn_kernel(q_ref, k_ref, v_ref, bias_ref, o_ref, *, L, KB, pad):
    c = pl.program_id(1)
    start = pl.multiple_of(c * L, L)
    kb = k_ref[0, pl.ds(start, KB), :]
    vb = v_ref[0, pl.ds(start, KB), :]
    q = q_ref[0].astype(F32)
    col = lax.broadcasted_iota(jnp.int32, (1, KB), 1)
    valid = (start + col) >= pad
    lo = lax.broadcasted_iota(jnp.int32, (L, LANES), 1) < ATT_HD
    for hp in range(ATT_HEADS // 2):
        sl = slice(hp * LANES, (hp + 1) * LANES)
        qp, kp, vp = q[:, sl], kb[:, sl], vb[:, sl]
        qs = jnp.concatenate([jnp.where(lo, qp, 0.0), jnp.where(lo, 0.0, qp)], axis=0).astype(BF16)
        s = lax.dot_general(qs, kp, (((1,), (1,)), ((), ())), preferred_element_type=F32)
        s = jnp.where(valid, s + bias_ref[hp], NEG)
        e = jnp.exp(s - jnp.max(s, axis=-1, keepdims=True))
        pv = jnp.dot(e.astype(BF16), vp, preferred_element_type=F32) / jnp.sum(e, axis=-1, keepdims=True)
        o_ref[0, :, sl] = jnp.where(lo, pv[:L], pv[L:]).astype(BF16)


def _attention(q, k, v, bias, *, L, KB, pad):
    B, S, _ = q.shape
    ktot = k.shape[1]
    bias = bias.reshape(ATT_HEADS // 2, 2 * L, KB)
    return pl.pallas_call(
        functools.partial(_attn_kernel, L=L, KB=KB, pad=pad),
        grid=(B, S // L),
        in_specs=[pl.BlockSpec((1, L, ATT_W), lambda b, c: (b, c, 0)),
                  pl.BlockSpec((1, ktot, ATT_W), lambda b, c: (b, 0, 0)),
                  pl.BlockSpec((1, ktot, ATT_W), lambda b, c: (b, 0, 0)),
                  pl.BlockSpec(bias.shape, lambda b, c: (0, 0, 0))],
        out_specs=pl.BlockSpec((1, L, ATT_W), lambda b, c: (b, c, 0)),
        out_shape=jax.ShapeDtypeStruct((B, S, ATT_W), BF16),
        compiler_params=_params("parallel", "arbitrary"),
        name="attention",
    )(q, k, v, bias)


def _rel_bias(table, L, KB):
    d = np.arange(L)[:, None] + ATT_CTX - np.arange(KB)[None, :]
    return jnp.transpose(table[np.clip(d, -MAX_REL, MAX_REL) + MAX_REL], (2, 0, 1))


def _pair_bias(bias):
    h, l, _ = bias.shape
    neg = jnp.full((h, l, l), NEG, bias.dtype)
    return jnp.concatenate([jnp.concatenate([bias, neg], axis=2), jnp.concatenate([neg, bias], axis=2)], axis=1)


def _t128(x):
    rows = x.shape[0]
    if rows < LANES:
        x = jnp.concatenate([x, jnp.zeros((LANES - rows, LANES), x.dtype)], axis=0)
    return x.T[:, :rows]


def _mlstm_kernel(q_ref, k_ref, v_ref, g_ref, c0_ref, n0_ref, m0_ref,
                  h_ref, c_ref, n_ref, m_ref, cs, ns, ms, *, L):
    c = pl.program_id(1)

    @pl.when(c == 0)
    def _():
        cs[...] = c0_ref[...]
        ns[...] = n0_ref[...]
        ms[...] = m0_ref[...]

    tril = lax.broadcasted_iota(jnp.int32, (L, L), 0) >= lax.broadcasted_iota(jnp.int32, (L, L), 1)
    for bb in range(q_ref.shape[0]):
        gates = g_ref[bb]
        row = lax.broadcasted_iota(jnp.int32, gates.shape, 0)
        csum = gates
        sh = 1
        while sh < L:
            csum = csum + jnp.where(row >= sh, pltpu.roll(csum, sh, axis=0), 0.0)
            sh *= 2
        gates_t = _t128(gates)
        csum_t = _t128(csum)
        for h in range(ML_HEADS):
            sl = slice(h * ML_DK, (h + 1) * ML_DK)
            q, k, v = q_ref[bb, :, sl], k_ref[bb, :, sl], v_ref[bb, :, sl]
            b_col = csum[:, ML_HEADS + h:ML_HEADS + h + 1]
            b_row = csum_t[ML_HEADS + h:ML_HEADS + h + 1, :]
            ig_col = gates[:, h:h + 1]
            ig_row = gates_t[h:h + 1, :]
            m_prev = ms[bb, h:h + 1, 0:1]
            a_col = b_col + m_prev
            dmat = jnp.where(tril, b_col - b_row + ig_row, NEG)
            m_t = jnp.maximum(a_col, jnp.max(dmat, axis=-1, keepdims=True))
            w_inter = jnp.exp(a_col - m_t)
            wmat = jnp.exp(dmat - m_t)
            qk = lax.dot_general(q, k, (((1,), (1,)), ((), ())), preferred_element_type=F32)
            wqk = wmat * qk
            c_old = cs[bb, h]
            n_old = ns[bb, h:h + 1, :]
            num = (w_inter * jnp.dot(q, c_old.astype(BF16), preferred_element_type=F32)
                   + jnp.dot(wqk.astype(BF16), v, preferred_element_type=F32))
            den = (w_inter * jnp.sum(q.astype(F32) * n_old, axis=-1, keepdims=True)
                   + jnp.sum(wqk, axis=-1, keepdims=True))
            h_ref[bb, :, sl] = num / jnp.maximum(jnp.abs(den), jnp.exp(-m_t))
            m_new = m_t[L - 1:L, :]
            decay = jnp.exp(a_col[L - 1:L, :] - m_new)
            wl_col = jnp.exp(b_col[L - 1:L, :] - b_col + ig_col - m_new)
            kw = k.astype(F32) * wl_col
            cs[bb, h] = decay * c_old + jnp.dot(_t128(kw).astype(BF16), v, preferred_element_type=F32)
            ns[bb, h:h + 1, :] = decay * n_old + jnp.sum(kw, axis=0, keepdims=True)
            ms[bb, h:h + 1, :] = jnp.broadcast_to(m_new, (1, ML_DK))

    @pl.when(c == pl.num_programs(1) - 1)
    def _():
        c_ref[...] = cs[...]
        n_ref[...] = ns[...]
        m_ref[...] = ms[...]


def _mlstm(q, k, v, gates, c0, n0, m0, *, L):
    B, S, _ = q.shape
    bb = ML_BATCH_ROWS
    seq = lambda w: pl.BlockSpec((bb, L, w), lambda b, c: (b, c, 0))
    st4 = pl.BlockSpec((bb, ML_HEADS, ML_DK, ML_DK), lambda b, c: (b, 0, 0, 0))
    st3 = pl.BlockSpec((bb, ML_HEADS, ML_DK), lambda b, c: (b, 0, 0))
    return pl.pallas_call(
        functools.partial(_mlstm_kernel, L=L),
        grid=(B // bb, S // L),
        in_specs=[seq(ML_W), seq(ML_W), seq(ML_W), seq(GATE_COLS), st4, st3, st3],
        out_specs=[seq(ML_W), st4, st3, st3],
        out_shape=[jax.ShapeDtypeStruct((B, S, ML_W), F32),
                   jax.ShapeDtypeStruct((B, ML_HEADS, ML_DK, ML_DK), F32),
                   jax.ShapeDtypeStruct((B, ML_HEADS, ML_DK), F32),
                   jax.ShapeDtypeStruct((B, ML_HEADS, ML_DK), F32)],
        scratch_shapes=[pltpu.VMEM((bb, ML_HEADS, ML_DK, ML_DK), F32),
                        pltpu.VMEM((bb, ML_HEADS, ML_DK), F32),
                        pltpu.VMEM((bb, ML_HEADS, ML_DK), F32)],
        compiler_params=_params("parallel", "arbitrary"),
        name="mlstm",
    )(q, k, v, gates, c0, n0, m0)


def _merge_kernel(x_ref, att_ref, h_ref, mo_ref, ga_ref, gb_ref, wa_ref, wm_ref, wo_ref, o_ref):
    ml = (jax.nn.sigmoid(mo_ref[...]) * h_ref[...]).astype(BF16)
    a = jnp.dot(att_ref[...], wa_ref[...], preferred_element_type=F32)
    b = jnp.dot(ml, wm_ref[...], preferred_element_type=F32)
    merged = jax.nn.sigmoid(ga_ref[...]) * a + jax.nn.sigmoid(gb_ref[...]) * b
    o_ref[...] = x_ref[...] + jnp.dot(merged.astype(BF16), wo_ref[...], preferred_element_type=F32)


def _merge(x, att, h, mo, ga, gb, wa, wm, wo):
    n = x.shape[0]
    tm = min(ROW_TILE, n)
    row = lambda w: pl.BlockSpec((tm, w), lambda i: (i, 0))
    full = lambda a: pl.BlockSpec(a.shape, lambda i: (0, 0))
    return pl.pallas_call(
        _merge_kernel,
        grid=(n // tm,),
        in_specs=[row(D_MODEL), row(ATT_W), row(ML_W), row(ML_W), row(D_MODEL), row(D_MODEL),
                  full(wa), full(wm), full(wo)],
        out_specs=row(D_MODEL),
        out_shape=jax.ShapeDtypeStruct((n, D_MODEL), F32),
        compiler_params=_params("parallel"),
        name="merge",
    )(x, att, h, mo, ga, gb, wa, wm, wo)


def _top16(s, ids, big):
    vals, win = [], []
    for _ in range(PEER_TOPK):
        m = jnp.max(s, axis=0, keepdims=True)
        i = jnp.min(jnp.where(s == m, ids, big), axis=0, keepdims=True)
        vals.append(m)
        win.append(i)
        s = jnp.where(ids == i, -jnp.inf, s)
    return jnp.concatenate(vals, axis=0), jnp.concatenate(win, axis=0)


_PAIR_IDS = np.array([b for b in range(16)] + [a * 16 + b for a in range(1, 8) for b in range(8)]
                     + [a * 16 for a in range(8, 16)], np.int32)


def _select(rank, table):
    out = jnp.zeros_like(table)
    for a in range(PEER_TOPK):
        out = jnp.where(rank == a, table[a:a + 1], out)
    return out


def _route_kernel(x_ref, g_ref, wq_ref, keys_ref, pair_ref, xn_o, xt_o, idx_o, gw_o, st_ref):
    x = x_ref[...]
    xn = _rms(x, g_ref[...])
    xn_o[...] = xn.reshape(xn_o.shape)
    xt_o[...] = x.reshape(xt_o.shape)
    q = jnp.dot(xn.astype(BF16), wq_ref[...], preferred_element_type=F32)
    st_ref[...] = lax.dot_general(keys_ref[...], q.astype(BF16), (((1,), (1,)), ((), ())),
                                  preferred_element_type=F32)
    lane_tiles = st_ref.shape[1] // LANES
    key_ids = lax.broadcasted_iota(jnp.int32, (PEER_NKEYS, LANES), 0)

    def head_tile(it, carry):
        h = it // lane_tiles
        lanes = pl.ds(pl.multiple_of((it % lane_tiles) * LANES, LANES), LANES)
        r0 = pl.multiple_of(h * 2 * PEER_NKEYS, 2 * PEER_NKEYS)
        sv0, si0 = _top16(st_ref[pl.ds(r0, PEER_NKEYS), lanes], key_ids, PEER_NKEYS)
        sv1, si1 = _top16(st_ref[pl.ds(r0 + PEER_NKEYS, PEER_NKEYS), lanes], key_ids, PEER_NKEYS)
        half = PEER_TOPK // 2
        cand = jnp.concatenate(
            [jnp.broadcast_to(sv0[0:1], (PEER_TOPK, LANES)) + sv1]
            + [jnp.broadcast_to(sv0[a:a + 1], (half, LANES)) + sv1[0:half] for a in range(1, half)]
            + [sv0[half:] + jnp.broadcast_to(sv1[0:1], (half, LANES))], axis=0)
        fv, fi = _top16(cand, pair_ref[...], PEER_TOPK * PEER_TOPK)
        eidx = _select(fi >> 4, si0) * PEER_NKEYS + _select(fi & (PEER_TOPK - 1), si1)
        e = jnp.exp(fv - fv[0:1])
        o0 = pl.multiple_of(h * PEER_TOPK, PEER_TOPK)
        gw_o[pl.ds(o0, PEER_TOPK), lanes] = e / jnp.sum(e, axis=0, keepdims=True)
        idx_o[pl.ds(o0, PEER_TOPK), lanes] = eidx
        return carry

    lax.fori_loop(0, PEER_HEADS * lane_tiles, head_tile, 0)


def _route(x, g2, wq, keys_bd):
    n = x.shape[0]
    tm = min(ROW_TILE, n)
    full = lambda a: pl.BlockSpec(a.shape, lambda i: (0, 0))
    pair_ids = jnp.asarray(np.broadcast_to(_PAIR_IDS[:, None], (_PAIR_IDS.size, LANES)))
    return pl.pallas_call(
        _route_kernel,
        grid=(n // tm,),
        scratch_shapes=[pltpu.VMEM((PEER_HEADS * 2 * PEER_NKEYS, tm), F32)],
        in_specs=[pl.BlockSpec((tm, D_MODEL), lambda i: (i, 0)), full(g2), full(wq), full(keys_bd), full(pair_ids)],
        out_specs=[pl.BlockSpec((tm, SUBLANES, LANES), lambda i: (i, 0, 0)),
                   pl.BlockSpec((tm, SUBLANES, LANES), lambda i: (i, 0, 0)),
                   pl.BlockSpec((PEER_PICKS, tm), lambda i: (0, i)),
                   pl.BlockSpec((PEER_PICKS, tm), lambda i: (0, i))],
        out_shape=[jax.ShapeDtypeStruct((n, SUBLANES, LANES), F32),
                   jax.ShapeDtypeStruct((n, SUBLANES, LANES), F32),
                   jax.ShapeDtypeStruct((PEER_PICKS, n), jnp.int32),
                   jax.ShapeDtypeStruct((PEER_PICKS, n), F32)],
        compiler_params=_params("parallel"),
        name="peer_route",
    )(x, g2, wq, keys_bd, pair_ids)


def _apply_kernel(idx_hbm, gwt_ref, xn_ref, x_ref, fg_ref, tab_hbm, y_ref, idx_s, buf, gsem, isem, coef_ref, *, nblk):
    i = pl.program_id(0)
    ngroups = PEER_BLOCK // GROUP
    rows = GROUP * PEER_PICKS
    blk_words = PEER_BLOCK * PEER_PICKS
    assert ngroups % GATHER_SLOTS == 0 and GATHER_SLOTS == 2

    def idx_copy(blk, islot):
        return pltpu.make_async_copy(idx_hbm.at[pl.ds(blk * blk_words, blk_words)],
                                     idx_s.at[pl.ds(islot * blk_words, blk_words)], isem.at[islot])

    def row_copy(e, slot, r):
        return pltpu.make_async_copy(tab_hbm.at[e], buf.at[slot, r], gsem.at[slot])

    def wait_group(slot):
        pltpu.make_async_copy(tab_hbm.at[pl.ds(0, rows)], buf.at[slot], gsem.at[slot]).wait()

    islot = i % 2
    has_next = i + 1 < nblk

    @pl.when(i == 0)
    def _():
        idx_copy(0, 0).start()
        idx_copy(0, 0).wait()

        def body(r, carry):
            row_copy(idx_s[r], 0, r).start()
            return carry

        lax.fori_loop(0, rows, body, 0, unroll=8)

    @pl.when(has_next)
    def _():
        idx_copy(i + 1, 1 - islot).start()

    lane = lax.broadcasted_iota(jnp.int32, (PEER_PICKS, LANES), 1)
    sub = lax.broadcasted_iota(jnp.int32, (SUBLANES, LANES), 0)
    masks = {1: (sub & 1) == 0, 2: (sub & 2) == 0, 4: (sub & 4) == 0}

    def fold(a, b, h):
        return jnp.where(masks[h], a, b) + pltpu.roll(jnp.where(masks[h], b, a), h, axis=0)

    def row_sums(ps):
        c = [fold(ps[2 * m], ps[2 * m + 1], 1) for m in range(4)]
        d = [fold(c[0], c[1], 2), fold(c[2], c[3], 2)]
        return fold(d[0], d[1], 4)

    def do_group(g, slot, next_word0):
        nslot = 1 - slot
        wait_group(slot)
        t0 = pl.multiple_of(g * GROUP, GROUP)
        lane0 = pl.multiple_of((t0 // LANES) * LANES, LANES)
        gw_tile = gwt_ref[:, pl.ds(lane0, LANES)]
        ys = []
        for j in range(GROUP):
            for k in range(PEER_PICKS):
                r = j * PEER_PICKS + k
                row_copy(idx_s[next_word0 + r], nslot, r).start(priority=k % 2)
            x = xn_ref[t0 + j]
            sums = []
            for c in range(PEER_PICKS // SUBLANES):
                r0 = j * PEER_PICKS + c * SUBLANES
                sums.append(row_sums([buf[slot, r0 + k, 0:SUBLANES, :] * x for k in range(SUBLANES)]))
            act = jnp.sum(jnp.concatenate(sums, axis=0), axis=-1, keepdims=True)
            gelu = 0.5 * act * (1.0 + lax.erf(act * (2.0 ** -0.5)))
            gw_col = jnp.sum(jnp.where(lane == t0 - lane0 + j, gw_tile, 0.0), axis=-1, keepdims=True)
            coef_ref[...] = jnp.broadcast_to(gw_col * gelu, (PEER_PICKS, LANES))
            accs = [jnp.zeros((SUBLANES, LANES), F32) for _ in range(4)]
            for k in range(PEER_PICKS):
                accs[k % 4] = accs[k % 4] + coef_ref[k:k + 1, :] * buf[slot, j * PEER_PICKS + k, SUBLANES:, :]
            y = x_ref[t0 + j] + ((accs[0] + accs[1]) + (accs[2] + accs[3]))
            ms = jnp.sum(jnp.sum(y * y, axis=0, keepdims=True), axis=-1, keepdims=True) * (1.0 / D_MODEL)
            ys.append(y * lax.rsqrt(ms + EPS) * fg_ref[...])
        y_ref[pl.ds(t0, GROUP), :] = jnp.stack(ys, axis=0).reshape(GROUP, D_MODEL)

    def pair_body(gg, carry):
        g = 2 * gg
        do_group(g, 0, islot * blk_words + (g + 1) * rows)
        last = gg + 1 == ngroups // 2

        @pl.when(jnp.logical_and(last, has_next))
        def _():
            idx_copy(i + 1, 1 - islot).wait()

        do_group(g + 1, 1, jnp.where(last, jnp.where(has_next, 1 - islot, islot) * blk_words,
                                     islot * blk_words + (g + 2) * rows))
        return carry

    lax.fori_loop(0, ngroups // 2, pair_body, 0)

    @pl.when(jnp.logical_not(has_next))
    def _():
        wait_group(0)


def _apply(idx, gwt, xn, x, fg, table):
    n = x.shape[0]
    nblk = n // PEER_BLOCK
    tile = pl.BlockSpec((PEER_BLOCK, SUBLANES, LANES), lambda i: (i, 0, 0))
    return pl.pallas_call(
        functools.partial(_apply_kernel, nblk=nblk),
        grid=(nblk,),
        in_specs=[pl.BlockSpec(memory_space=pl.ANY), pl.BlockSpec((PEER_PICKS, PEER_BLOCK), lambda i: (0, i)),
                  tile, tile,
                  pl.BlockSpec(fg.shape, lambda i: (0, 0)), pl.BlockSpec(memory_space=pl.ANY)],
        out_specs=pl.BlockSpec((PEER_BLOCK, D_MODEL), lambda i: (i, 0)),
        out_shape=jax.ShapeDtypeStruct((n, D_MODEL), F32),
        scratch_shapes=[pltpu.SMEM((2 * PEER_BLOCK * PEER_PICKS,), jnp.int32),
                        pltpu.VMEM((GATHER_SLOTS, GROUP * PEER_PICKS, EXPERT_ROWS, LANES), F32),
                        pltpu.SemaphoreType.DMA((GATHER_SLOTS,)),
                        pltpu.SemaphoreType.DMA((2,)),
                        pltpu.VMEM((PEER_PICKS, LANES), F32)],
        compiler_params=_params("arbitrary"),
        name="peer_apply",
    )(idx, gwt, xn, x, fg, table)


def _head_tile_steps(it, st_ref, pair_ids, gw_dst, idx_dst, lane_tiles):
    h = it // lane_tiles
    lanes = pl.ds(pl.multiple_of((it % lane_tiles) * LANES, LANES), LANES)
    r0 = pl.multiple_of(h * 2 * PEER_NKEYS, 2 * PEER_NKEYS)
    key_ids = lax.broadcasted_iota(jnp.int32, (PEER_NKEYS, LANES), 0)
    s = [st_ref[pl.ds(r0, PEER_NKEYS), lanes], st_ref[pl.ds(r0 + PEER_NKEYS, PEER_NKEYS), lanes]]
    vals, wins = ([], []), ([], [])

    def take_max(x, ids, big):
        m = jnp.max(x, axis=0, keepdims=True)
        w = jnp.min(jnp.where(x == m, ids, big), axis=0, keepdims=True)
        return m, w, jnp.where(ids == w, -jnp.inf, x)

    for _ in range(PEER_TOPK):
        for p in range(2):
            m, w, s[p] = take_max(s[p], key_ids, PEER_NKEYS)
            vals[p].append(m)
            wins[p].append(w)
        yield
    sv0, sv1 = jnp.concatenate(vals[0], axis=0), jnp.concatenate(vals[1], axis=0)
    si0, si1 = jnp.concatenate(wins[0], axis=0), jnp.concatenate(wins[1], axis=0)
    half = PEER_TOPK // 2
    cand = jnp.concatenate(
        [jnp.broadcast_to(sv0[0:1], (PEER_TOPK, LANES)) + sv1]
        + [jnp.broadcast_to(sv0[a:a + 1], (half, LANES)) + sv1[0:half] for a in range(1, half)]
        + [sv0[half:] + jnp.broadcast_to(sv1[0:1], (half, LANES))], axis=0)
    fvals, fwins = [], []
    for r in range(PEER_TOPK):
        m, w, cand = take_max(cand, pair_ids, PEER_TOPK * PEER_TOPK)
        fvals.append(m)
        fwins.append(w)
        if r % 2:
            yield
    fv, fi = jnp.concatenate(fvals, axis=0), jnp.concatenate(fwins, axis=0)
    eidx = _select(fi >> 4, si0) * PEER_NKEYS + _select(fi & (PEER_TOPK - 1), si1)
    e = jnp.exp(fv - fv[0:1])
    o0 = pl.multiple_of(h * PEER_TOPK, PEER_TOPK)
    gw_dst[pl.ds(o0, PEER_TOPK), lanes] = e / jnp.sum(e, axis=0, keepdims=True)
    idx_dst[pl.ds(o0, PEER_TOPK), lanes] = eidx
    yield


def _fused_kernel(x0_ref, x1_ref, xnext_ref, g_ref, wq_ref, keys_ref, pair_ref, fg_ref, tab_hbm, y_ref,
                  st_ref, xn_t, x_t, gw_v, idx_tv, idx_rows, idx_s, buf, gsem, isem, coef_ref, *, nblk):
    i = pl.program_id(0)
    ngroups = PEER_BLOCK // GROUP
    rows = GROUP * PEER_PICKS
    lane_tiles = PEER_BLOCK // LANES
    head_tiles = PEER_HEADS * lane_tiles
    assert ngroups // 2 == head_tiles and nblk >= 2
    islot = i % 2
    a3 = i % 3
    w3 = (i + 2) % 3
    has_next = i + 1 < nblk
    has_next2 = i + 2 < nblk
    pair_ids = pair_ref[...]

    def scores(x, slot):
        xn = _rms(x, g_ref[...])
        xn_t[slot] = xn.reshape(xn_t.shape[1:])
        x_t[slot] = x.reshape(x_t.shape[1:])
        q = jnp.dot(xn.astype(BF16), wq_ref[...], preferred_element_type=F32)
        st_ref[...] = lax.dot_general(keys_ref[...], q.astype(BF16), (((1,), (1,)), ((), ())),
                                      preferred_element_type=F32)

    def publish_copy(smem_slot):
        return pltpu.make_async_copy(idx_rows, idx_s.at[pl.ds(smem_slot * PEER_BLOCK, PEER_BLOCK), :],
                                     isem.at[smem_slot])

    def row_copy(e, slot, r):
        return pltpu.make_async_copy(tab_hbm.at[e], buf.at[slot, r], gsem.at[slot])

    def wait_group(slot):
        pltpu.make_async_copy(tab_hbm.at[pl.ds(0, rows)], buf.at[slot], gsem.at[slot]).wait()

    @pl.when(i == 0)
    def _():
        for blk, x_ref in ((0, x0_ref), (1, x1_ref)):
            scores(x_ref[...], blk)

            def body(it, carry):
                for _ in _head_tile_steps(it, st_ref, pair_ids, gw_v.at[blk], idx_tv, lane_tiles):
                    pass
                return carry

            lax.fori_loop(0, head_tiles, body, 0)
            idx_rows[...] = idx_tv[...].T
            publish_copy(blk).start()
            publish_copy(blk).wait()

        def issue(r, carry):
            row_copy(idx_s[r // PEER_PICKS, r % PEER_PICKS], 0, r).start()
            return carry

        lax.fori_loop(0, rows, issue, 0, unroll=8)

    @pl.when(has_next2)
    def _():
        scores(xnext_ref[...], w3)

    lane = lax.broadcasted_iota(jnp.int32, (PEER_PICKS, LANES), 1)
    sub = lax.broadcasted_iota(jnp.int32, (SUBLANES, LANES), 0)
    masks = {1: (sub & 1) == 0, 2: (sub & 2) == 0, 4: (sub & 4) == 0}

    def fold(a, b, h):
        return jnp.where(masks[h], a, b) + pltpu.roll(jnp.where(masks[h], b, a), h, axis=0)

    def row_sums(ps):
        c = [fold(ps[2 * m], ps[2 * m + 1], 1) for m in range(4)]
        d = [fold(c[0], c[1], 2), fold(c[2], c[3], 2)]
        return fold(d[0], d[1], 4)

    def do_group(g, slot, next_row0, between):
        nslot = 1 - slot

        def issue_token(j):
            for k in range(PEER_PICKS):
                row_copy(idx_s[next_row0 + j, k], nslot, j * PEER_PICKS + k).start(priority=k % 2)

        issue_token(0)
        wait_group(slot)
        t0 = pl.multiple_of(g * GROUP, GROUP)
        lane0 = pl.multiple_of((t0 // LANES) * LANES, LANES)
        gw_tile = gw_v[a3, :, pl.ds(lane0, LANES)]
        ys = []
        for j in range(GROUP):
            if j + 1 < GROUP:
                issue_token(j + 1)
            x = xn_t[a3, t0 + j]
            sums = []
            for c in range(PEER_PICKS // SUBLANES):
                r0 = j * PEER_PICKS + c * SUBLANES
                sums.append(row_sums([buf[slot, r0 + k, 0:SUBLANES, :] * x for k in range(SUBLANES)]))
            act = jnp.sum(jnp.concatenate(sums, axis=0), axis=-1, keepdims=True)
            gelu = 0.5 * act * (1.0 + lax.erf(act * (2.0 ** -0.5)))
            gw_col = jnp.sum(jnp.where(lane == t0 - lane0 + j, gw_tile, 0.0), axis=-1, keepdims=True)
            coef_ref[...] = jnp.broadcast_to(gw_col * gelu, (PEER_PICKS, LANES))
            accs = [jnp.zeros((SUBLANES, LANES), F32) for _ in range(4)]
            for k in range(PEER_PICKS):
                accs[k % 4] = accs[k % 4] + coef_ref[k:k + 1, :] * buf[slot, j * PEER_PICKS + k, SUBLANES:, :]
            y = x_t[a3, t0 + j] + ((accs[0] + accs[1]) + (accs[2] + accs[3]))
            ms = jnp.sum(jnp.sum(y * y, axis=0, keepdims=True), axis=-1, keepdims=True) * (1.0 / D_MODEL)
            ys.append(y * lax.rsqrt(ms + EPS) * fg_ref[...])
            between()
        y_ref[pl.ds(t0, GROUP), :] = jnp.stack(ys, axis=0).reshape(GROUP, D_MODEL)

    def pair_body(gg, carry):
        g = 2 * gg
        last = gg + 1 == ngroups // 2

        @pl.when(jnp.logical_and(last, jnp.logical_and(has_next, i >= 1)))
        def _():
            publish_copy(1 - islot).wait()

        steps = _head_tile_steps(gg, st_ref, pair_ids, gw_v.at[w3], idx_tv, lane_tiles)

        def between():
            next(steps, None)
            next(steps, None)

        do_group(g, 0, islot * PEER_BLOCK + (g + 1) * GROUP, between)
        do_group(g + 1, 1, jnp.where(last, jnp.where(has_next, 1 - islot, islot) * PEER_BLOCK,
                                     islot * PEER_BLOCK + (g + 2) * GROUP), between)
        for _ in steps:
            pass
        return carry

    lax.fori_loop(0, ngroups // 2, pair_body, 0)

    @pl.when(has_next2)
    def _():
        idx_rows[...] = idx_tv[...].T
        publish_copy(islot).start()

    @pl.when(jnp.logical_not(has_next))
    def _():
        wait_group(0)


def _route_apply(x, g2, wq, keys_bd, fg, table):
    n = x.shape[0]
    nblk = n // PEER_BLOCK
    full = lambda a: pl.BlockSpec(a.shape, lambda i: (0,) * a.ndim)
    pair_ids = jnp.asarray(np.broadcast_to(_PAIR_IDS[:, None], (_PAIR_IDS.size, LANES)))
    blk = lambda f: pl.BlockSpec((PEER_BLOCK, D_MODEL), f)
    return pl.pallas_call(
        functools.partial(_fused_kernel, nblk=nblk),
        grid=(nblk,),
        in_specs=[blk(lambda i: (0, 0)), blk(lambda i: (1, 0)), blk(lambda i: (jnp.minimum(i + 2, nblk - 1), 0)),
                  full(g2), full(wq), full(keys_bd), full(pair_ids), full(fg), pl.BlockSpec(memory_space=pl.ANY)],
        out_specs=blk(lambda i: (i, 0)),
        out_shape=jax.ShapeDtypeStruct((n, D_MODEL), F32),
        scratch_shapes=[pltpu.VMEM((PEER_HEADS * 2 * PEER_NKEYS, PEER_BLOCK), F32),
                        pltpu.VMEM((3, PEER_BLOCK, SUBLANES, LANES), F32),
                        pltpu.VMEM((3, PEER_BLOCK, SUBLANES, LANES), F32),
                        pltpu.VMEM((3, PEER_PICKS, PEER_BLOCK), F32),
                        pltpu.VMEM((PEER_PICKS, PEER_BLOCK), jnp.int32),
                        pltpu.VMEM((PEER_BLOCK, PEER_PICKS), jnp.int32),
                        pltpu.SMEM((2 * PEER_BLOCK, PEER_PICKS), jnp.int32),
                        pltpu.VMEM((GATHER_SLOTS, GROUP * PEER_PICKS, EXPERT_ROWS, LANES), F32),
                        pltpu.SemaphoreType.DMA((GATHER_SLOTS,)),
                        pltpu.SemaphoreType.DMA((2,)),
                        pltpu.VMEM((PEER_PICKS, LANES), F32)],
        compiler_params=_params("arbitrary"),
        name="peer_route_apply",
    )(x, x, x, g2, wq, keys_bd, pair_ids, fg, table)


def _layer(x3, weights, att_kv, att_geom, ml_state, L):
    B, S, _ = x3.shape
    n = B * S
    x = x3.reshape(n, D_MODEL)
    (q, k, v, kf, vf, mq, mk, mv, mo, ga, gb, gates) = _inproj(
        x, weights["g1"], weights["w_main"], weights["b_main"], weights["wg_hi"], weights["wg_lo"],
        weights["bg"], weights["fb"])
    sh = lambda a: a.reshape(B, S, a.shape[-1])
    k_all, v_all = att_kv(sh(k), sh(v))
    att = _attention(sh(q), k_all, v_all, att_geom["bias"], L=att_geom["L"], KB=att_geom["KB"], pad=att_geom["pad"])
    h, c_new, n_new, m_new = _mlstm(sh(mq), sh(mk), sh(mv), sh(gates), *ml_state, L=L)
    x2 = _merge(x, att.reshape(n, ATT_W), h.reshape(n, ML_W), mo, ga, gb,
                weights["wa"], weights["wm"], weights["wo"])
    fg = weights["fg"].reshape(SUBLANES, LANES)
    if n >= 2 * PEER_BLOCK:
        y = _route_apply(x2, weights["g2"], weights["wq"], weights["keys_bd"], fg, weights["table"])
    else:
        xn2, x2_tiles, idx_t, gw_t = _route(x2, weights["g2"], weights["wq"], weights["keys_bd"])
        y = _apply(idx_t.T.reshape(n * PEER_PICKS), gw_t, xn2, x2_tiles, fg, weights["table"])
    return (y.reshape(B, S, D_MODEL), sh(kf).reshape(B, S, ATT_HEADS, ATT_HD), sh(vf).reshape(B, S, ATT_HEADS, ATT_HD),
            c_new, n_new, m_new[:, :, 0])


def kernel(x_prompt, x_sample, cache_att_k, cache_att_v, state_mlstm_C, state_mlstm_n, state_mlstm_m, norm1_g, w_in, b_in, ml_f_bias, att_rel_bias, w_att_branch, w_ml_branch, w_out, norm2_g, peer_wq, peer_sub_keys, peer_u, peer_v, final_g):
    depth = w_in.shape[0]
    assert depth == 1, "single-layer step"
    l = 0
    w = w_in[l]
    b = b_in[l]
    gate_lo, gate_hi = 7 * 512, 7 * 512 + 2 * ML_HEADS
    w_main = jnp.concatenate([w[:, :gate_lo], w[:, gate_hi:]], axis=1).astype(BF16)
    b_main = jnp.concatenate([b[:gate_lo], b[gate_hi:]])[None, :]
    wg = jnp.pad(w[:, gate_lo:gate_hi], ((0, 0), (0, GATE_COLS - 2 * ML_HEADS)))
    wg_hi = wg.astype(BF16)
    wg_lo = (wg - wg_hi.astype(F32)).astype(BF16)
    bg = jnp.pad(b[gate_lo:gate_hi], (0, GATE_COLS - 2 * ML_HEADS))[None, :]
    fb = jnp.pad(ml_f_bias[l], (ML_HEADS, GATE_COLS - 2 * ML_HEADS))[None, :]
    sk = peer_sub_keys[l].reshape(PEER_HEADS * 2, PEER_NKEYS, PEER_DKEY // 2)
    eye = jnp.eye(PEER_HEADS * 2, dtype=F32)
    keys_bd = (sk[:, :, None, :] * eye[:, None, :, None]).reshape(PEER_HEADS * 2 * PEER_NKEYS, D_MODEL).astype(BF16)
    weights = dict(
        g1=norm1_g[l][None, :], w_main=w_main, b_main=b_main, wg_hi=wg_hi, wg_lo=wg_lo, bg=bg, fb=fb,
        wa=w_att_branch[l].astype(BF16), wm=w_ml_branch[l].astype(BF16), wo=w_out[l].astype(BF16),
        g2=norm2_g[l][None, :], wq=peer_wq[l].astype(BF16), keys_bd=keys_bd, fg=final_g[None, :],
        table=jnp.concatenate([peer_u[l].reshape(-1, SUBLANES, LANES), peer_v[l].reshape(-1, SUBLANES, LANES)],
                              axis=1))

    Bs, T, _ = x_sample.shape
    P = cache_att_k.shape[2]
    assert P == ATT_CTX
    ck = cache_att_k[l].reshape(Bs, P, ATT_W).astype(BF16)
    cv = cache_att_v[l].reshape(Bs, P, ATT_W).astype(BF16)
    cat_kv = lambda k, v: (jnp.concatenate([ck, k], axis=1), jnp.concatenate([cv, v], axis=1))
    geom_s = dict(bias=_rel_bias(att_rel_bias[l], T, P + T), L=T, KB=P + T, pad=0)
    state_s = (state_mlstm_C[l], state_mlstm_n[l],
               jnp.broadcast_to(state_mlstm_m[l][:, :, None], (Bs, ML_HEADS, ML_DK)))
    y_s, k_s, v_s, c_s, n_s, m_s = _layer(x_sample, weights, cat_kv, geom_s, state_s, T)

    B, S, _ = x_prompt.shape
    pad_kv = lambda k, v: (jnp.pad(k, ((0, 0), (ATT_CTX, 0), (0, 0))), jnp.pad(v, ((0, 0), (ATT_CTX, 0), (0, 0))))
    geom_p = dict(bias=_pair_bias(_rel_bias(att_rel_bias[l], CHUNK, ATT_CTX + CHUNK)), L=2 * CHUNK,
                  KB=ATT_CTX + 2 * CHUNK, pad=ATT_CTX)
    state_p = (jnp.zeros((B, ML_HEADS, ML_DK, ML_DK), F32), jnp.zeros((B, ML_HEADS, ML_DK), F32),
               jnp.full((B, ML_HEADS, ML_DK), NEG, F32))
    y_p, k_p, v_p, c_p, n_p, m_p = _layer(x_prompt, weights, pad_kv, geom_p, state_p, CHUNK)
    keep = min(ATT_CTX, S)

    st = lambda a: a[None]
    return (y_p, y_s, st(k_p[:, -keep:]), st(v_p[:, -keep:]), st(c_p), st(n_p), st(m_p),
            st(k_s), st(v_s), st(c_s), st(n_s), st(m_s))
```

```python
import functools

import jax
import jax.numpy as jnp
import numpy as np
from jax import lax
from jax.experimental import pallas as pl
from jax.experimental.pallas import tpu as pltpu

F32 = jnp.float32
BF16 = jnp.bfloat16

D_MODEL = 1024
CHUNK = 64
ATT_HEADS = 8
ATT_HD = 64
ATT_W = ATT_HEADS * ATT_HD
ATT_CTX = 8 * CHUNK
MAX_REL = 128
ML_HEADS = 4
ML_DK = 128
ML_W = ML_HEADS * ML_DK
PEER_HEADS = 8
PEER_NKEYS = 128
PEER_DKEY = 128
PEER_TOPK = 16
PEER_PICKS = PEER_HEADS * PEER_TOPK
EPS = 1e-6
NEG = -1e30

LANES = 128
SUBLANES = 8
ROW_TILE = 256
GATE_COLS = LANES
MAIN_COLS = 7 * 512 + 2 * D_MODEL
ML_BATCH_ROWS = 2
GROUP = SUBLANES
PEER_BLOCK = 256
GATHER_SLOTS = 2
EXPERT_ROWS = 2 * D_MODEL // LANES
VMEM_LIMIT = 56 * 1024 * 1024


def _params(*sem):
    return pltpu.CompilerParams(dimension_semantics=sem, vmem_limit_bytes=VMEM_LIMIT)


def _rms(x, g):
    return x * lax.rsqrt(jnp.mean(x * x, axis=-1, keepdims=True) + EPS) * g


def _inproj_kernel(x_ref, g_ref, w_ref, b_ref, wgh_ref, wgl_ref, bg_ref, fb_ref,
                   q_o, k_o, v_o, kf_o, vf_o, mq_o, mk_o, mv_o, mo_o, ga_o, gb_o, gt_o):
    xn = _rms(x_ref[...], g_ref[...])
    xh = xn.astype(BF16)

    def proj(off, width):
        return jnp.dot(xh, w_ref[:, off:off + width], preferred_element_type=F32) + b_ref[:, off:off + width]

    q_o[...] = (proj(0, 512) * (ATT_HD ** -0.5)).astype(BF16)
    k = proj(512, 512)
    kf_o[...] = k
    k_o[...] = k.astype(BF16)
    v = proj(1024, 512)
    vf_o[...] = v
    v_o[...] = v.astype(BF16)
    mq_o[...] = proj(1536, 512).astype(BF16)
    mk_o[...] = (proj(2048, 512) * (ML_DK ** -0.5)).astype(BF16)
    mv_o[...] = proj(2560, 512).astype(BF16)
    mo_o[...] = proj(3072, 512)
    ga_o[...] = proj(3584, D_MODEL)
    gb_o[...] = proj(3584 + D_MODEL, D_MODEL)
    xl = (xn - xh.astype(F32)).astype(BF16)
    zg = (jnp.dot(xh, wgh_ref[...], preferred_element_type=F32)
          + jnp.dot(xh, wgl_ref[...], preferred_element_type=F32)
          + jnp.dot(xl, wgh_ref[...], preferred_element_type=F32)) + bg_ref[...]
    t = zg + fb_ref[...]
    log_sig = jnp.minimum(t, 0.0) - jnp.log1p(jnp.exp(-jnp.abs(t)))
    lane = lax.broadcasted_iota(jnp.int32, zg.shape, 1)
    gt_o[...] = jnp.where(lane < ML_HEADS, zg, log_sig)


def _inproj(x, g1, w_main, b_main, wg_hi, wg_lo, bg, fb):
    n = x.shape[0]
    tm = min(ROW_TILE, n)
    row = lambda w: pl.BlockSpec((tm, w), lambda i: (i, 0))
    full = lambda a: pl.BlockSpec(a.shape, lambda i: (0, 0))
    widths = [(512, BF16)] * 3 + [(512, F32)] * 2 + [(512, BF16)] * 3 + [(512, F32), (D_MODEL, F32), (D_MODEL, F32),
                                                                     (GATE_COLS, F32)]
    return pl.pallas_call(
        _inproj_kernel,
        grid=(n // tm,),
        in_specs=[row(D_MODEL), full(g1), full(w_main), full(b_main), full(wg_hi), full(wg_lo), full(bg), full(fb)],
        out_specs=[row(w) for w, _ in widths],
        out_shape=[jax.ShapeDtypeStruct((n, w), dt) for w, dt in widths],
        compiler_params=_params("parallel"),
        name="inproj",
    )(x, g1, w_main, b_main, wg_hi, wg_lo, bg, fb)


def _attn_kernel(q_ref, k_ref, v_ref, bias_ref, o_ref, *, L, KB, pad):
    c = pl.program_id(1)
    start = pl.multiple_of(c * L, L)
    kb = k_ref[0, pl.ds(start, KB), :]
    vb = v_ref[0, pl.ds(start, KB), :]
    q = q_ref[0].astype(F32)
    col = lax.broadcasted_iota(jnp.int32, (1, KB), 1)
    valid = (start + col) >= pad
    lo = lax.broadcasted_iota(jnp.int32, (L, LANES), 1) < ATT_HD
    for hp in range(ATT_HEADS // 2):
        sl = slice(hp * LANES, (hp + 1) * LANES)
        qp, kp, vp = q[:, sl], kb[:, sl], vb[:, sl]
        qs = jnp.concatenate([jnp.where(lo, qp, 0.0), jnp.where(lo, 0.0, qp)], axis=0).astype(BF16)
        s = lax.dot_general(qs, kp, (((1,), (1,)), ((), ())), preferred_element_type=F32)
        s = jnp.where(valid, s + bias_ref[hp], NEG)
        e = jnp.exp(s - jnp.max(s, axis=-1, keepdims=True))
        pv = jnp.dot(e.astype(BF16), vp, preferred_element_type=F32) / jnp.sum(e, axis=-1, keepdims=True)
        o_ref[0, :, sl] = jnp.where(lo, pv[:L], pv[L:]).astype(BF16)


def _attention(q, k, v, bias, *, L, KB, pad):
    B, S, _ = q.shape
    ktot = k.shape[1]
    bias = bias.reshape(ATT_HEADS // 2, 2 * L, KB)
    return pl.pallas_call(
        functools.partial(_attn_kernel, L=L, KB=KB, pad=pad),
        grid=(B, S // L),
        in_specs=[pl.BlockSpec((1, L, ATT_W), lambda b, c: (b, c, 0)),
                  pl.BlockSpec((1, ktot, ATT_W), lambda b, c: (b, 0, 0)),
                  pl.BlockSpec((1, ktot, ATT_W), lambda b, c: (b, 0, 0)),
                  pl.BlockSpec(bias.shape, lambda b, c: (0, 0, 0))],
        out_specs=pl.BlockSpec((1, L, ATT_W), lambda b, c: (b, c, 0)),
        out_shape=jax.ShapeDtypeStruct((B, S, ATT_W), BF16),
        compiler_params=_params("parallel", "arbitrary"),
        name="attention",
    )(q, k, v, bias)


def _rel_bias(table, L, KB):
    m = np.arange(KB + L - 1)
    g = table[np.clip(L - 1 + ATT_CTX - m, -MAX_REL, MAX_REL) + MAX_REL].T
    return jnp.stack([g[:, L - 1 - l:L - 1 - l + KB] for l in range(L)], axis=1)


def _pair_bias(bias):
    h, l, _ = bias.shape
    neg = jnp.full((h, l, l), NEG, bias.dtype)
    return jnp.concatenate([jnp.concatenate([bias, neg], axis=2), jnp.concatenate([neg, bias], axis=2)], axis=1)


def _t128(x):
    rows = x.shape[0]
    if rows < LANES:
        x = jnp.concatenate([x, jnp.zeros((LANES - rows, LANES), x.dtype)], axis=0)
    return x.T[:, :rows]


def _mlstm_kernel(q_ref, k_ref, v_ref, g_ref, c0_ref, n0_ref, m0_ref,
                  h_ref, c_ref, n_ref, m_ref, cs, ns, ms, *, L):
    c = pl.program_id(1)

    @pl.when(c == 0)
    def _():
        cs[...] = c0_ref[...]
        ns[...] = n0_ref[...]
        ms[...] = m0_ref[...]

    tril = lax.broadcasted_iota(jnp.int32, (L, L), 0) >= lax.broadcasted_iota(jnp.int32, (L, L), 1)
    for bb in range(q_ref.shape[0]):
        gates = g_ref[bb]
        row = lax.broadcasted_iota(jnp.int32, gates.shape, 0)
        csum = gates
        sh = 1
        while sh < L:
            csum = csum + jnp.where(row >= sh, pltpu.roll(csum, sh, axis=0), 0.0)
            sh *= 2
        gates_t = _t128(gates)
        csum_t = _t128(csum)
        for h in range(ML_HEADS):
            sl = slice(h * ML_DK, (h + 1) * ML_DK)
            q, k, v = q_ref[bb, :, sl], k_ref[bb, :, sl], v_ref[bb, :, sl]
            b_col = csum[:, ML_HEADS + h:ML_HEADS + h + 1]
            b_row = csum_t[ML_HEADS + h:ML_HEADS + h + 1, :]
            ig_col = gates[:, h:h + 1]
            ig_row = gates_t[h:h + 1, :]
            m_prev = ms[bb, h:h + 1, 0:1]
            a_col = b_col + m_prev
            dmat = jnp.where(tril, b_col - b_row + ig_row, NEG)
            m_t = jnp.maximum(a_col, jnp.max(dmat, axis=-1, keepdims=True))
            w_inter = jnp.exp(a_col - m_t)
            wmat = jnp.exp(dmat - m_t)
            qk = lax.dot_general(q, k, (((1,), (1,)), ((), ())), preferred_element_type=F32)
            wqk = wmat * qk
            c_old = cs[bb, h]
            n_old = ns[bb, h:h + 1, :]
            num = (w_inter * jnp.dot(q, c_old.astype(BF16), preferred_element_type=F32)
                   + jnp.dot(wqk.astype(BF16), v, preferred_element_type=F32))
            den = (w_inter * jnp.sum(q.astype(F32) * n_old, axis=-1, keepdims=True)
                   + jnp.sum(wqk, axis=-1, keepdims=True))
            h_ref[bb, :, sl] = num / jnp.maximum(jnp.abs(den), jnp.exp(-m_t))
            m_new = m_t[L - 1:L, :]
            decay = jnp.exp(a_col[L - 1:L, :] - m_new)
            wl_col = jnp.exp(b_col[L - 1:L, :] - b_col + ig_col - m_new)
            kw = k.astype(F32) * wl_col
            cs[bb, h] = decay * c_old + lax.dot_general(kw.astype(BF16), v, (((0,), (0,)), ((), ())),
                                                        preferred_element_type=F32)
            ns[bb, h:h + 1, :] = decay * n_old + jnp.sum(kw, axis=0, keepdims=True)
            ms[bb, h:h + 1, :] = jnp.broadcast_to(m_new, (1, ML_DK))

    @pl.when(c == pl.num_programs(1) - 1)
    def _():
        c_ref[...] = cs[...]
        n_ref[...] = ns[...]
        m_ref[...] = ms[...]


def _mlstm(q, k, v, gates, c0, n0, m0, *, L):
    B, S, _ = q.shape
    bb = ML_BATCH_ROWS
    seq = lambda w: pl.BlockSpec((bb, L, w), lambda b, c: (b, c, 0))
    st4 = pl.BlockSpec((bb, ML_HEADS, ML_DK, ML_DK), lambda b, c: (b, 0, 0, 0))
    st3 = pl.BlockSpec((bb, ML_HEADS, ML_DK), lambda b, c: (b, 0, 0))
    return pl.pallas_call(
        functools.partial(_mlstm_kernel, L=L),
        grid=(B // bb, S // L),
        in_specs=[seq(ML_W), seq(ML_W), seq(ML_W), seq(GATE_COLS), st4, st3, st3],
        out_specs=[seq(ML_W), st4, st3, st3],
        out_shape=[jax.ShapeDtypeStruct((B, S, ML_W), F32),
                   jax.ShapeDtypeStruct((B, ML_HEADS, ML_DK, ML_DK), F32),
                   jax.ShapeDtypeStruct((B, ML_HEADS, ML_DK), F32),
                   jax.ShapeDtypeStruct((B, ML_HEADS, ML_DK), F32)],
        scratch_shapes=[pltpu.VMEM((bb, ML_HEADS, ML_DK, ML_DK), F32),
                        pltpu.VMEM((bb, ML_HEADS, ML_DK), F32),
                        pltpu.VMEM((bb, ML_HEADS, ML_DK), F32)],
        compiler_params=_params("parallel", "arbitrary"),
        name="mlstm",
    )(q, k, v, gates, c0, n0, m0)


def _merge_kernel(x_ref, att_ref, h_ref, mo_ref, ga_ref, gb_ref, wa_ref, wm_ref, wo_ref, o_ref):
    ml = (jax.nn.sigmoid(mo_ref[...]) * h_ref[...]).astype(BF16)
    a = jnp.dot(att_ref[...], wa_ref[...], preferred_element_type=F32)
    b = jnp.dot(ml, wm_ref[...], preferred_element_type=F32)
    merged = jax.nn.sigmoid(ga_ref[...]) * a + jax.nn.sigmoid(gb_ref[...]) * b
    o_ref[...] = x_ref[...] + jnp.dot(merged.astype(BF16), wo_ref[...], preferred_element_type=F32)


def _merge(x, att, h, mo, ga, gb, wa, wm, wo):
    n = x.shape[0]
    tm = min(ROW_TILE, n)
    row = lambda w: pl.BlockSpec((tm, w), lambda i: (i, 0))
    full = lambda a: pl.BlockSpec(a.shape, lambda i: (0, 0))
    return pl.pallas_call(
        _merge_kernel,
        grid=(n // tm,),
        in_specs=[row(D_MODEL), row(ATT_W), row(ML_W), row(ML_W), row(D_MODEL), row(D_MODEL),
                  full(wa), full(wm), full(wo)],
        out_specs=row(D_MODEL),
        out_shape=jax.ShapeDtypeStruct((n, D_MODEL), F32),
        compiler_params=_params("parallel"),
        name="merge",
    )(x, att, h, mo, ga, gb, wa, wm, wo)


def _top16(s, ids, big):
    vals, win = [], []
    for _ in range(PEER_TOPK):
        m = jnp.max(s, axis=0, keepdims=True)
        i = jnp.min(jnp.where(s == m, ids, big), axis=0, keepdims=True)
        vals.append(m)
        win.append(i)
        s = jnp.where(ids == i, -jnp.inf, s)
    return jnp.concatenate(vals, axis=0), jnp.concatenate(win, axis=0)


_PAIR_IDS = np.array([b for b in range(16)] + [a * 16 + b for a in range(1, 8) for b in range(8)]
                     + [a * 16 for a in range(8, 16)], np.int32)


def _select(rank, table):
    out = jnp.zeros_like(table)
    for a in range(PEER_TOPK):
        out = jnp.where(rank == a, table[a:a + 1], out)
    return out


def _route_kernel(x_ref, g_ref, wq_ref, keys_ref, pair_ref, xn_o, xt_o, idx_o, gw_o, st_ref):
    x = x_ref[...]
    xn = _rms(x, g_ref[...])
    xn_o[...] = xn.reshape(xn_o.shape)
    xt_o[...] = x.reshape(xt_o.shape)
    q = jnp.dot(xn.astype(BF16), wq_ref[...], preferred_element_type=F32)
    st_ref[...] = lax.dot_general(keys_ref[...], q.astype(BF16), (((1,), (1,)), ((), ())),
                                  preferred_element_type=F32)
    lane_tiles = st_ref.shape[1] // LANES
    key_ids = lax.broadcasted_iota(jnp.int32, (PEER_NKEYS, LANES), 0)

    def head_tile(it, carry):
        h = it // lane_tiles
        lanes = pl.ds(pl.multiple_of((it % lane_tiles) * LANES, LANES), LANES)
        r0 = pl.multiple_of(h * 2 * PEER_NKEYS, 2 * PEER_NKEYS)
        sv0, si0 = _top16(st_ref[pl.ds(r0, PEER_NKEYS), lanes], key_ids, PEER_NKEYS)
        sv1, si1 = _top16(st_ref[pl.ds(r0 + PEER_NKEYS, PEER_NKEYS), lanes], key_ids, PEER_NKEYS)
        half = PEER_TOPK // 2
        cand = jnp.concatenate(
            [jnp.broadcast_to(sv0[0:1], (PEER_TOPK, LANES)) + sv1]
            + [jnp.broadcast_to(sv0[a:a + 1], (half, LANES)) + sv1[0:half] for a in range(1, half)]
            + [sv0[half:] + jnp.broadcast_to(sv1[0:1], (half, LANES))], axis=0)
        fv, fi = _top16(cand, pair_ref[...], PEER_TOPK * PEER_TOPK)
        eidx = _select(fi >> 4, si0) * PEER_NKEYS + _select(fi & (PEER_TOPK - 1), si1)
        e = jnp.exp(fv - fv[0:1])
        o0 = pl.multiple_of(h * PEER_TOPK, PEER_TOPK)
        gw_o[pl.ds(o0, PEER_TOPK), lanes] = e / jnp.sum(e, axis=0, keepdims=True)
        idx_o[pl.ds(o0, PEER_TOPK), lanes] = eidx
        return carry

    lax.fori_loop(0, PEER_HEADS * lane_tiles, head_tile, 0)


def _route(x, g2, wq, keys_bd):
    n = x.shape[0]
    tm = min(ROW_TILE, n)
    full = lambda a: pl.BlockSpec(a.shape, lambda i: (0, 0))
    pair_ids = jnp.asarray(np.broadcast_to(_PAIR_IDS[:, None], (_PAIR_IDS.size, LANES)))
    return pl.pallas_call(
        _route_kernel,
        grid=(n // tm,),
        scratch_shapes=[pltpu.VMEM((PEER_HEADS * 2 * PEER_NKEYS, tm), F32)],
        in_specs=[pl.BlockSpec((tm, D_MODEL), lambda i: (i, 0)), full(g2), full(wq), full(keys_bd), full(pair_ids)],
        out_specs=[pl.BlockSpec((tm, SUBLANES, LANES), lambda i: (i, 0, 0)),
                   pl.BlockSpec((tm, SUBLANES, LANES), lambda i: (i, 0, 0)),
                   pl.BlockSpec((PEER_PICKS, tm), lambda i: (0, i)),
                   pl.BlockSpec((PEER_PICKS, tm), lambda i: (0, i))],
        out_shape=[jax.ShapeDtypeStruct((n, SUBLANES, LANES), F32),
                   jax.ShapeDtypeStruct((n, SUBLANES, LANES), F32),
                   jax.ShapeDtypeStruct((PEER_PICKS, n), jnp.int32),
                   jax.ShapeDtypeStruct((PEER_PICKS, n), F32)],
        compiler_params=_params("parallel"),
        name="peer_route",
    )(x, g2, wq, keys_bd, pair_ids)


def _apply_kernel(idx_hbm, gwt_ref, xn_ref, x_ref, fg_ref, tab_hbm, y_ref, idx_s, buf, gsem, isem, coef_ref, *, nblk):
    i = pl.program_id(0)
    ngroups = PEER_BLOCK // GROUP
    rows = GROUP * PEER_PICKS
    blk_words = PEER_BLOCK * PEER_PICKS
    assert ngroups % GATHER_SLOTS == 0 and GATHER_SLOTS == 2

    def idx_copy(blk, islot):
        return pltpu.make_async_copy(idx_hbm.at[pl.ds(blk * blk_words, blk_words)],
                                     idx_s.at[pl.ds(islot * blk_words, blk_words)], isem.at[islot])

    def row_copy(e, slot, r):
        return pltpu.make_async_copy(tab_hbm.at[e], buf.at[slot, r], gsem.at[slot])

    def wait_group(slot):
        pltpu.make_async_copy(tab_hbm.at[pl.ds(0, rows)], buf.at[slot], gsem.at[slot]).wait()

    islot = i % 2
    has_next = i + 1 < nblk

    @pl.when(i == 0)
    def _():
        idx_copy(0, 0).start()
        idx_copy(0, 0).wait()

        def body(r, carry):
            row_copy(idx_s[r], 0, r).start()
            return carry

        lax.fori_loop(0, rows, body, 0, unroll=8)

    @pl.when(has_next)
    def _():
        idx_copy(i + 1, 1 - islot).start()

    lane = lax.broadcasted_iota(jnp.int32, (PEER_PICKS, LANES), 1)
    sub = lax.broadcasted_iota(jnp.int32, (SUBLANES, LANES), 0)
    masks = {1: (sub & 1) == 0, 2: (sub & 2) == 0, 4: (sub & 4) == 0}

    def fold(a, b, h):
        return jnp.where(masks[h], a, b) + pltpu.roll(jnp.where(masks[h], b, a), h, axis=0)

    def row_sums(ps):
        c = [fold(ps[2 * m], ps[2 * m + 1], 1) for m in range(4)]
        d = [fold(c[0], c[1], 2), fold(c[2], c[3], 2)]
        return fold(d[0], d[1], 4)

    def do_group(g, slot, next_word0):
        nslot = 1 - slot
        wait_group(slot)
        t0 = pl.multiple_of(g * GROUP, GROUP)
        lane0 = pl.multiple_of((t0 // LANES) * LANES, LANES)
        gw_tile = gwt_ref[:, pl.ds(lane0, LANES)]
        ys = []
        for j in range(GROUP):
            for k in range(PEER_PICKS):
                r = j * PEER_PICKS + k
                row_copy(idx_s[next_word0 + r], nslot, r).start(priority=k % 2)
            x = xn_ref[t0 + j]
            sums = []
            for c in range(PEER_PICKS // SUBLANES):
                r0 = j * PEER_PICKS + c * SUBLANES
                sums.append(row_sums([buf[slot, r0 + k, 0:SUBLANES, :] * x for k in range(SUBLANES)]))
            act = jnp.sum(jnp.concatenate(sums, axis=0), axis=-1, keepdims=True)
            gelu = 0.5 * act * (1.0 + lax.erf(act * (2.0 ** -0.5)))
            gw_col = jnp.sum(jnp.where(lane == t0 - lane0 + j, gw_tile, 0.0), axis=-1, keepdims=True)
            coef_ref[...] = jnp.broadcast_to(gw_col * gelu, (PEER_PICKS, LANES))
            accs = [jnp.zeros((SUBLANES, LANES), F32) for _ in range(4)]
            for k in range(PEER_PICKS):
                accs[k % 4] = accs[k % 4] + coef_ref[k:k + 1, :] * buf[slot, j * PEER_PICKS + k, SUBLANES:, :]
            y = x_ref[t0 + j] + ((accs[0] + accs[1]) + (accs[2] + accs[3]))
            ms = jnp.sum(jnp.sum(y * y, axis=0, keepdims=True), axis=-1, keepdims=True) * (1.0 / D_MODEL)
            ys.append(y * lax.rsqrt(ms + EPS) * fg_ref[...])
        y_ref[pl.ds(t0, GROUP), :] = jnp.stack(ys, axis=0).reshape(GROUP, D_MODEL)

    def pair_body(gg, carry):
        g = 2 * gg
        do_group(g, 0, islot * blk_words + (g + 1) * rows)
        last = gg + 1 == ngroups // 2

        @pl.when(jnp.logical_and(last, has_next))
        def _():
            idx_copy(i + 1, 1 - islot).wait()

        do_group(g + 1, 1, jnp.where(last, jnp.where(has_next, 1 - islot, islot) * blk_words,
                                     islot * blk_words + (g + 2) * rows))
        return carry

    lax.fori_loop(0, ngroups // 2, pair_body, 0)

    @pl.when(jnp.logical_not(has_next))
    def _():
        wait_group(0)


def _apply(idx, gwt, xn, x, fg, table):
    n = x.shape[0]
    nblk = n // PEER_BLOCK
    tile = pl.BlockSpec((PEER_BLOCK, SUBLANES, LANES), lambda i: (i, 0, 0))
    return pl.pallas_call(
        functools.partial(_apply_kernel, nblk=nblk),
        grid=(nblk,),
        in_specs=[pl.BlockSpec(memory_space=pl.ANY), pl.BlockSpec((PEER_PICKS, PEER_BLOCK), lambda i: (0, i)),
                  tile, tile,
                  pl.BlockSpec(fg.shape, lambda i: (0, 0)), pl.BlockSpec(memory_space=pl.ANY)],
        out_specs=pl.BlockSpec((PEER_BLOCK, D_MODEL), lambda i: (i, 0)),
        out_shape=jax.ShapeDtypeStruct((n, D_MODEL), F32),
        scratch_shapes=[pltpu.SMEM((2 * PEER_BLOCK * PEER_PICKS,), jnp.int32),
                        pltpu.VMEM((GATHER_SLOTS, GROUP * PEER_PICKS, EXPERT_ROWS, LANES), F32),
                        pltpu.SemaphoreType.DMA((GATHER_SLOTS,)),
                        pltpu.SemaphoreType.DMA((2,)),
                        pltpu.VMEM((PEER_PICKS, LANES), F32)],
        compiler_params=_params("arbitrary"),
        name="peer_apply",
    )(idx, gwt, xn, x, fg, table)


def _head_tile_steps(it, st_ref, pair_ids, gw_dst, idx_dst, lane_tiles):
    h = it // lane_tiles
    lanes = pl.ds(pl.multiple_of((it % lane_tiles) * LANES, LANES), LANES)
    r0 = pl.multiple_of(h * 2 * PEER_NKEYS, 2 * PEER_NKEYS)
    key_ids = lax.broadcasted_iota(jnp.int32, (PEER_NKEYS, LANES), 0)
    s = [st_ref[pl.ds(r0, PEER_NKEYS), lanes], st_ref[pl.ds(r0 + PEER_NKEYS, PEER_NKEYS), lanes]]
    vals, wins = ([], []), ([], [])

    def take_max(x, ids, big):
        m = jnp.max(x, axis=0, keepdims=True)
        w = jnp.min(jnp.where(x == m, ids, big), axis=0, keepdims=True)
        return m, w, jnp.where(ids == w, -jnp.inf, x)

    for _ in range(PEER_TOPK):
        for p in range(2):
            m, w, s[p] = take_max(s[p], key_ids, PEER_NKEYS)
            vals[p].append(m)
            wins[p].append(w)
        yield
    sv0, sv1 = jnp.concatenate(vals[0], axis=0), jnp.concatenate(vals[1], axis=0)
    si0, si1 = jnp.concatenate(wins[0], axis=0), jnp.concatenate(wins[1], axis=0)
    half = PEER_TOPK // 2
    cand = jnp.concatenate(
        [jnp.broadcast_to(sv0[0:1], (PEER_TOPK, LANES)) + sv1]
        + [jnp.broadcast_to(sv0[a:a + 1], (half, LANES)) + sv1[0:half] for a in range(1, half)]
        + [sv0[half:] + jnp.broadcast_to(sv1[0:1], (half, LANES))], axis=0)
    fvals, fwins = [], []
    for r in range(PEER_TOPK):
        m, w, cand = take_max(cand, pair_ids, PEER_TOPK * PEER_TOPK)
        fvals.append(m)
        fwins.append(w)
        if r % 2:
            yield
    fv, fi = jnp.concatenate(fvals, axis=0), jnp.concatenate(fwins, axis=0)
    eidx = _select(fi >> 4, si0) * PEER_NKEYS + _select(fi & (PEER_TOPK - 1), si1)
    e = jnp.exp(fv - fv[0:1])
    o0 = pl.multiple_of(h * PEER_TOPK, PEER_TOPK)
    gw_dst[pl.ds(o0, PEER_TOPK), lanes] = e / jnp.sum(e, axis=0, keepdims=True)
    idx_dst[pl.ds(o0, PEER_TOPK), lanes] = eidx
    yield


def _fused_kernel(x0_ref, x1_ref, xnext_ref, g_ref, wq_ref, keys_ref, pair_ref, fg_ref, tab_hbm, y_ref,
                  st_ref, xn_t, x_t, gw_v, idx_tv, idx_rows, idx_s, buf, gsem, isem, coef_ref, *, nblk):
    i = pl.program_id(0)
    ngroups = PEER_BLOCK // GROUP
    rows = GROUP * PEER_PICKS
    lane_tiles = PEER_BLOCK // LANES
    head_tiles = PEER_HEADS * lane_tiles
    assert ngroups // 2 == head_tiles and nblk >= 2
    islot = i % 2
    a3 = i % 3
    w3 = (i + 2) % 3
    has_next = i + 1 < nblk
    has_next2 = i + 2 < nblk
    pair_ids = pair_ref[...]

    def scores(x, slot):
        xn = _rms(x, g_ref[...])
        xn_t[slot] = xn.reshape(xn_t.shape[1:])
        x_t[slot] = x.reshape(x_t.shape[1:])
        q = jnp.dot(xn.astype(BF16), wq_ref[...], preferred_element_type=F32)
        st_ref[...] = lax.dot_general(keys_ref[...], q.astype(BF16), (((1,), (1,)), ((), ())),
                                      preferred_element_type=F32)

    def publish_copy(smem_slot):
        return pltpu.make_async_copy(idx_rows, idx_s.at[pl.ds(smem_slot * PEER_BLOCK, PEER_BLOCK), :],
                                     isem.at[smem_slot])

    def row_copy(e, slot, r):
        return pltpu.make_async_copy(tab_hbm.at[e], buf.at[slot, r], gsem.at[slot])

    def wait_group(slot):
        pltpu.make_async_copy(tab_hbm.at[pl.ds(0, rows)], buf.at[slot], gsem.at[slot]).wait()

    @pl.when(i == 0)
    def _():
        for blk, x_ref in ((0, x0_ref), (1, x1_ref)):
            scores(x_ref[...], blk)

            def body(it, carry):
                for _ in _head_tile_steps(it, st_ref, pair_ids, gw_v.at[blk], idx_tv, lane_tiles):
                    pass
                return carry

            lax.fori_loop(0, head_tiles, body, 0)
            idx_rows[...] = idx_tv[...].T
            publish_copy(blk).start()
            publish_copy(blk).wait()

        def issue(r, carry):
            row_copy(idx_s[r // PEER_PICKS, r % PEER_PICKS], 0, r).start()
            return carry

        lax.fori_loop(0, rows, issue, 0, unroll=8)

    @pl.when(has_next2)
    def _():
        scores(xnext_ref[...], w3)

    lane = lax.broadcasted_iota(jnp.int32, (PEER_PICKS, LANES), 1)
    sub = lax.broadcasted_iota(jnp.int32, (SUBLANES, LANES), 0)
    masks = {1: (sub & 1) == 0, 2: (sub & 2) == 0, 4: (sub & 4) == 0}

    def fold(a, b, h):
        return jnp.where(masks[h], a, b) + pltpu.roll(jnp.where(masks[h], b, a), h, axis=0)

    def row_sums(ps):
        c = [fold(ps[2 * m], ps[2 * m + 1], 1) for m in range(4)]
        d = [fold(c[0], c[1], 2), fold(c[2], c[3], 2)]
        return fold(d[0], d[1], 4)

    def do_group(g, slot, next_row0, between):
        nslot = 1 - slot

        def issue_token(j):
            for k in range(PEER_PICKS):
                row_copy(idx_s[next_row0 + j, k], nslot, j * PEER_PICKS + k).start(priority=k % 2)

        issue_token(0)
        wait_group(slot)
        t0 = pl.multiple_of(g * GROUP, GROUP)
        lane0 = pl.multiple_of((t0 // LANES) * LANES, LANES)
        gw_tile = gw_v[a3, :, pl.ds(lane0, LANES)]
        ys = []
        for j in range(GROUP):
            if j + 1 < GROUP:
                issue_token(j + 1)
            x = xn_t[a3, t0 + j]
            sums = []
            for c in range(PEER_PICKS // SUBLANES):
                r0 = j * PEER_PICKS + c * SUBLANES
                sums.append(row_sums([buf[slot, r0 + k, 0:SUBLANES, :] * x for k in range(SUBLANES)]))
            act = jnp.sum(jnp.concatenate(sums, axis=0), axis=-1, keepdims=True)
            gelu = 0.5 * act * (1.0 + lax.erf(act * (2.0 ** -0.5)))
            gw_col = jnp.sum(jnp.where(lane == t0 - lane0 + j, gw_tile, 0.0), axis=-1, keepdims=True)
            coef_ref[...] = jnp.broadcast_to(gw_col * gelu, (PEER_PICKS, LANES))
            accs = [jnp.zeros((SUBLANES, LANES), F32) for _ in range(4)]
            for k in range(PEER_PICKS):
                accs[k % 4] = accs[k % 4] + coef_ref[k:k + 1, :] * buf[slot, j * PEER_PICKS + k, SUBLANES:, :]
            y = x_t[a3, t0 + j] + ((accs[0] + accs[1]) + (accs[2] + accs[3]))
            ms = jnp.sum(jnp.sum(y * y, axis=0, keepdims=True), axis=-1, keepdims=True) * (1.0 / D_MODEL)
            ys.append(y * lax.rsqrt(ms + EPS) * fg_ref[...])
            between()
        y_ref[pl.ds(t0, GROUP), :] = jnp.stack(ys, axis=0).reshape(GROUP, D_MODEL)

    def pair_body(gg, carry):
        g = 2 * gg
        last = gg + 1 == ngroups // 2

        @pl.when(jnp.logical_and(last, jnp.logical_and(has_next, i >= 1)))
        def _():
            publish_copy(1 - islot).wait()

        steps = _head_tile_steps(gg, st_ref, pair_ids, gw_v.at[w3], idx_tv, lane_tiles)

        def between():
            next(steps, None)
            next(steps, None)

        do_group(g, 0, islot * PEER_BLOCK + (g + 1) * GROUP, between)
        do_group(g + 1, 1, jnp.where(last, jnp.where(has_next, 1 - islot, islot) * PEER_BLOCK,
                                     islot * PEER_BLOCK + (g + 2) * GROUP), between)
        for _ in steps:
            pass
        return carry

    lax.fori_loop(0, ngroups // 2, pair_body, 0)

    @pl.when(has_next2)
    def _():
        idx_rows[...] = idx_tv[...].T
        publish_copy(islot).start()

    @pl.when(jnp.logical_not(has_next))
    def _():
        wait_group(0)


def _route_apply(x, g2, wq, keys_bd, fg, table):
    n = x.shape[0]
    nblk = n // PEER_BLOCK
    full = lambda a: pl.BlockSpec(a.shape, lambda i: (0,) * a.ndim)
    pair_ids = jnp.asarray(np.broadcast_to(_PAIR_IDS[:, None], (_PAIR_IDS.size, LANES)))
    blk = lambda f: pl.BlockSpec((PEER_BLOCK, D_MODEL), f)
    return pl.pallas_call(
        functools.partial(_fused_kernel, nblk=nblk),
        grid=(nblk,),
        in_specs=[blk(lambda i: (0, 0)), blk(lambda i: (1, 0)), blk(lambda i: (jnp.minimum(i + 2, nblk - 1), 0)),
                  full(g2), full(wq), full(keys_bd), full(pair_ids), full(fg), pl.BlockSpec(memory_space=pl.ANY)],
        out_specs=blk(lambda i: (i, 0)),
        out_shape=jax.ShapeDtypeStruct((n, D_MODEL), F32),
        scratch_shapes=[pltpu.VMEM((PEER_HEADS * 2 * PEER_NKEYS, PEER_BLOCK), F32),
                        pltpu.VMEM((3, PEER_BLOCK, SUBLANES, LANES), F32),
                        pltpu.VMEM((3, PEER_BLOCK, SUBLANES, LANES), F32),
                        pltpu.VMEM((3, PEER_PICKS, PEER_BLOCK), F32),
                        pltpu.VMEM((PEER_PICKS, PEER_BLOCK), jnp.int32),
                        pltpu.VMEM((PEER_BLOCK, PEER_PICKS), jnp.int32),
                        pltpu.SMEM((2 * PEER_BLOCK, PEER_PICKS), jnp.int32),
                        pltpu.VMEM((GATHER_SLOTS, GROUP * PEER_PICKS, EXPERT_ROWS, LANES), F32),
                        pltpu.SemaphoreType.DMA((GATHER_SLOTS,)),
                        pltpu.SemaphoreType.DMA((2,)),
                        pltpu.VMEM((PEER_PICKS, LANES), F32)],
        compiler_params=_params("arbitrary"),
        name="peer_route_apply",
    )(x, x, x, g2, wq, keys_bd, pair_ids, fg, table)


def _layer(x3, weights, att_kv, att_geom, ml_state, L):
    B, S, _ = x3.shape
    n = B * S
    x = x3.reshape(n, D_MODEL)
    (q, k, v, kf, vf, mq, mk, mv, mo, ga, gb, gates) = _inproj(
        x, weights["g1"], weights["w_main"], weights["b_main"], weights["wg_hi"], weights["wg_lo"],
        weights["bg"], weights["fb"])
    sh = lambda a: a.reshape(B, S, a.shape[-1])
    k_all, v_all = att_kv(sh(k), sh(v))
    att = _attention(sh(q), k_all, v_all, att_geom["bias"], L=att_geom["L"], KB=att_geom["KB"], pad=att_geom["pad"])
    h, c_new, n_new, m_new = _mlstm(sh(mq), sh(mk), sh(mv), sh(gates), *ml_state, L=L)
    x2 = _merge(x, att.reshape(n, ATT_W), h.reshape(n, ML_W), mo, ga, gb,
                weights["wa"], weights["wm"], weights["wo"])
    fg = weights["fg"].reshape(SUBLANES, LANES)
    if n >= 2 * PEER_BLOCK:
        y = _route_apply(x2, weights["g2"], weights["wq"], weights["keys_bd"], fg, weights["table"])
    else:
        xn2, x2_tiles, idx_t, gw_t = _route(x2, weights["g2"], weights["wq"], weights["keys_bd"])
        y = _apply(idx_t.T.reshape(n * PEER_PICKS), gw_t, xn2, x2_tiles, fg, weights["table"])
    return (y.reshape(B, S, D_MODEL), sh(kf).reshape(B, S, ATT_HEADS, ATT_HD), sh(vf).reshape(B, S, ATT_HEADS, ATT_HD),
            c_new, n_new, m_new[:, :, 0])


def kernel(x_prompt, x_sample, cache_att_k, cache_att_v, state_mlstm_C, state_mlstm_n, state_mlstm_m, norm1_g, w_in, b_in, ml_f_bias, att_rel_bias, w_att_branch, w_ml_branch, w_out, norm2_g, peer_wq, peer_sub_keys, peer_u, peer_v, final_g):
    depth = w_in.shape[0]
    assert depth == 1, "single-layer step"
    l = 0
    w = w_in[l]
    b = b_in[l]
    gate_lo, gate_hi = 7 * 512, 7 * 512 + 2 * ML_HEADS
    w_main = jnp.concatenate([w[:, :gate_lo], w[:, gate_hi:]], axis=1).astype(BF16)
    b_main = jnp.concatenate([b[:gate_lo], b[gate_hi:]])[None, :]
    wg = jnp.pad(w[:, gate_lo:gate_hi], ((0, 0), (0, GATE_COLS - 2 * ML_HEADS)))
    wg_hi = wg.astype(BF16)
    wg_lo = (wg - wg_hi.astype(F32)).astype(BF16)
    bg = jnp.pad(b[gate_lo:gate_hi], (0, GATE_COLS - 2 * ML_HEADS))[None, :]
    fb = jnp.pad(ml_f_bias[l], (ML_HEADS, GATE_COLS - 2 * ML_HEADS))[None, :]
    sk = peer_sub_keys[l].reshape(PEER_HEADS * 2, PEER_NKEYS, PEER_DKEY // 2)
    eye = jnp.eye(PEER_HEADS * 2, dtype=F32)
    keys_bd = (sk[:, :, None, :] * eye[:, None, :, None]).reshape(PEER_HEADS * 2 * PEER_NKEYS, D_MODEL).astype(BF16)
    weights = dict(
        g1=norm1_g[l][None, :], w_main=w_main, b_main=b_main, wg_hi=wg_hi, wg_lo=wg_lo, bg=bg, fb=fb,
        wa=w_att_branch[l].astype(BF16), wm=w_ml_branch[l].astype(BF16), wo=w_out[l].astype(BF16),
        g2=norm2_g[l][None, :], wq=peer_wq[l].astype(BF16), keys_bd=keys_bd, fg=final_g[None, :],
        table=jnp.concatenate([peer_u[l].reshape(-1, SUBLANES, LANES), peer_v[l].reshape(-1, SUBLANES, LANES)],
                              axis=1))

    Bs, T, _ = x_sample.shape
    P = cache_att_k.shape[2]
    assert P == ATT_CTX
    ck = cache_att_k[l].reshape(Bs, P, ATT_W).astype(BF16)
    cv = cache_att_v[l].reshape(Bs, P, ATT_W).astype(BF16)
    cat_kv = lambda k, v: (jnp.concatenate([ck, k], axis=1), jnp.concatenate([cv, v], axis=1))
    geom_s = dict(bias=_rel_bias(att_rel_bias[l], T, P + T), L=T, KB=P + T, pad=0)
    state_s = (state_mlstm_C[l], state_mlstm_n[l],
               jnp.broadcast_to(state_mlstm_m[l][:, :, None], (Bs, ML_HEADS, ML_DK)))
    y_s, k_s, v_s, c_s, n_s, m_s = _layer(x_sample, weights, cat_kv, geom_s, state_s, T)

    B, S, _ = x_prompt.shape
    pad_kv = lambda k, v: (jnp.pad(k, ((0, 0), (ATT_CTX, 0), (0, 0))), jnp.pad(v, ((0, 0), (ATT_CTX, 0), (0, 0))))
    geom_p = dict(bias=_pair_bias(_rel_bias(att_rel_bias[l], CHUNK, ATT_CTX + CHUNK)), L=2 * CHUNK,
                  KB=ATT_CTX + 2 * CHUNK, pad=ATT_CTX)
    state_p = (jnp.zeros((B, ML_HEADS, ML_DK, ML_DK), F32), jnp.zeros((B, ML_HEADS, ML_DK), F32),
               jnp.full((B, ML_HEADS, ML_DK), NEG, F32))
    y_p, k_p, v_p, c_p, n_p, m_p = _layer(x_prompt, weights, pad_kv, geom_p, state_p, CHUNK)
    keep = min(ATT_CTX, S)

    st = lambda a: a[None]
    return (y_p, y_s, st(k_p[:, -keep:]), st(v_p[:, -keep:]), st(c_p), st(n_p), st(m_p),
            st(k_s), st(v_s), st(c_s), st(n_s), st(m_s))
```

```python
import functools

import jax
import jax.numpy as jnp
import numpy as np
from jax import lax
from jax.experimental import pallas as pl
from jax.experimental.pallas import tpu as pltpu

F32 = jnp.float32
BF16 = jnp.bfloat16

D_MODEL = 1024
CHUNK = 64
ATT_HEADS = 8
ATT_HD = 64
ATT_W = ATT_HEADS * ATT_HD
ATT_CTX = 8 * CHUNK
MAX_REL = 128
ML_HEADS = 4
ML_DK = 128
ML_W = ML_HEADS * ML_DK
PEER_HEADS = 8
PEER_NKEYS = 128
PEER_DKEY = 128
PEER_TOPK = 16
PEER_PICKS = PEER_HEADS * PEER_TOPK
EPS = 1e-6
NEG = -1e30

LANES = 128
SUBLANES = 8
ROW_TILE = 256
GATE_COLS = LANES
MAIN_COLS = 7 * 512 + 2 * D_MODEL
ML_BATCH_ROWS = 2
GROUP = SUBLANES
PEER_BLOCK = 256
GATHER_SLOTS = 2
ISSUE_AHEAD = 3
EXPERT_ROWS = 2 * D_MODEL // LANES
VMEM_LIMIT = 56 * 1024 * 1024


def _params(*sem):
    return pltpu.CompilerParams(dimension_semantics=sem, vmem_limit_bytes=VMEM_LIMIT)


def _rms(x, g):
    return x * lax.rsqrt(jnp.mean(x * x, axis=-1, keepdims=True) + EPS) * g


def _inproj_kernel(x_ref, g_ref, w_ref, b_ref, wgh_ref, wgl_ref, bg_ref, fb_ref,
                   q_o, k_o, v_o, kf_o, vf_o, mq_o, mk_o, mv_o, mo_o, ga_o, gb_o, gt_o):
    xn = _rms(x_ref[...], g_ref[...])
    xh = xn.astype(BF16)

    def proj(off, width):
        return jnp.dot(xh, w_ref[:, off:off + width], preferred_element_type=F32) + b_ref[:, off:off + width]

    q_o[...] = (proj(0, 512) * (ATT_HD ** -0.5)).astype(BF16)
    k = proj(512, 512)
    kf_o[...] = k
    k_o[...] = k.astype(BF16)
    v = proj(1024, 512)
    vf_o[...] = v
    v_o[...] = v.astype(BF16)
    mq_o[...] = proj(1536, 512).astype(BF16)
    mk_o[...] = (proj(2048, 512) * (ML_DK ** -0.5)).astype(BF16)
    mv_o[...] = proj(2560, 512).astype(BF16)
    mo_o[...] = proj(3072, 512)
    ga_o[...] = proj(3584, D_MODEL)
    gb_o[...] = proj(3584 + D_MODEL, D_MODEL)
    xl = (xn - xh.astype(F32)).astype(BF16)
    zg = (jnp.dot(xh, wgh_ref[...], preferred_element_type=F32)
          + jnp.dot(xh, wgl_ref[...], preferred_element_type=F32)
          + jnp.dot(xl, wgh_ref[...], preferred_element_type=F32)) + bg_ref[...]
    t = zg + fb_ref[...]
    log_sig = jnp.minimum(t, 0.0) - jnp.log1p(jnp.exp(-jnp.abs(t)))
    lane = lax.broadcasted_iota(jnp.int32, zg.shape, 1)
    gt_o[...] = jnp.where(lane < ML_HEADS, zg, log_sig)


def _inproj(x, g1, w_main, b_main, wg_hi, wg_lo, bg, fb):
    n = x.shape[0]
    tm = min(ROW_TILE, n)
    row = lambda w: pl.BlockSpec((tm, w), lambda i: (i, 0))
    full = lambda a: pl.BlockSpec(a.shape, lambda i: (0, 0))
    widths = [(512, BF16)] * 3 + [(512, F32)] * 2 + [(512, BF16)] * 3 + [(512, F32), (D_MODEL, F32), (D_MODEL, F32),
                                                                     (GATE_COLS, F32)]
    return pl.pallas_call(
        _inproj_kernel,
        grid=(n // tm,),
        in_specs=[row(D_MODEL), full(g1), full(w_main), full(b_main), full(wg_hi), full(wg_lo), full(bg), full(fb)],
        out_specs=[row(w) for w, _ in widths],
        out_shape=[jax.ShapeDtypeStruct((n, w), dt) for w, dt in widths],
        compiler_params=_params("parallel"),
        name="inproj",
    )(x, g1, w_main, b_main, wg_hi, wg_lo, bg, fb)


def _attn_kernel(q_ref, k_ref, v_ref, bias_ref, o_ref, *, L, KB, pad):
    c = pl.program_id(1)
    start = pl.multiple_of(c * L, L)
    kb = k_ref[0, pl.ds(start, KB), :]
    vb = v_ref[0, pl.ds(start, KB), :]
    q = q_ref[0].astype(F32)
    col = lax.broadcasted_iota(jnp.int32, (1, KB), 1)
    valid = (start + col) >= pad
    lo = lax.broadcasted_iota(jnp.int32, (L, LANES), 1) < ATT_HD
    for hp in range(ATT_HEADS // 2):
        sl = slice(hp * LANES, (hp + 1) * LANES)
        qp, kp, vp = q[:, sl], kb[:, sl], vb[:, sl]
        qs = jnp.concatenate([jnp.where(lo, qp, 0.0), jnp.where(lo, 0.0, qp)], axis=0).astype(BF16)
        s = lax.dot_general(qs, kp, (((1,), (1,)), ((), ())), preferred_element_type=F32)
        s = jnp.where(valid, s + bias_ref[hp], NEG)
        e = jnp.exp(s - jnp.max(s, axis=-1, keepdims=True))
        pv = jnp.dot(e.astype(BF16), vp, preferred_element_type=F32) / jnp.sum(e, axis=-1, keepdims=True)
        o_ref[0, :, sl] = jnp.where(lo, pv[:L], pv[L:]).astype(BF16)


def _attention(q, k, v, bias, *, L, KB, pad):
    B, S, _ = q.shape
    ktot = k.shape[1]
    bias = bias.reshape(ATT_HEADS // 2, 2 * L, KB)
    return pl.pallas_call(
        functools.partial(_attn_kernel, L=L, KB=KB, pad=pad),
        grid=(B, S // L),
        in_specs=[pl.BlockSpec((1, L, ATT_W), lambda b, c: (b, c, 0)),
                  pl.BlockSpec((1, ktot, ATT_W), lambda b, c: (b, 0, 0)),
                  pl.BlockSpec((1, ktot, ATT_W), lambda b, c: (b, 0, 0)),
                  pl.BlockSpec(bias.shape, lambda b, c: (0, 0, 0))],
        out_specs=pl.BlockSpec((1, L, ATT_W), lambda b, c: (b, c, 0)),
        out_shape=jax.ShapeDtypeStruct((B, S, ATT_W), BF16),
        compiler_params=_params("parallel", "arbitrary"),
        name="attention",
    )(q, k, v, bias)


def _rel_bias(table, L, KB):
    m = np.arange(KB + L - 1)
    g = table[np.clip(L - 1 + ATT_CTX - m, -MAX_REL, MAX_REL) + MAX_REL].T
    return jnp.stack([g[:, L - 1 - l:L - 1 - l + KB] for l in range(L)], axis=1)


def _pair_bias(bias):
    h, l, _ = bias.shape
    neg = jnp.full((h, l, l), NEG, bias.dtype)
    return jnp.concatenate([jnp.concatenate([bias, neg], axis=2), jnp.concatenate([neg, bias], axis=2)], axis=1)


def _t128(x):
    rows = x.shape[0]
    if rows < LANES:
        x = jnp.concatenate([x, jnp.zeros((LANES - rows, LANES), x.dtype)], axis=0)
    return x.T[:, :rows]


def _mlstm_kernel(q_ref, k_ref, v_ref, g_ref, c0_ref, n0_ref, m0_ref,
                  h_ref, c_ref, n_ref, m_ref, cs, ns, ms, *, L):
    c = pl.program_id(1)

    @pl.when(c == 0)
    def _():
        cs[...] = c0_ref[...]
        ns[...] = n0_ref[...]
        ms[...] = m0_ref[...]

    tril = lax.broadcasted_iota(jnp.int32, (L, L), 0) >= lax.broadcasted_iota(jnp.int32, (L, L), 1)
    for bb in range(q_ref.shape[0]):
        gates = g_ref[bb]
        row = lax.broadcasted_iota(jnp.int32, gates.shape, 0)
        csum = gates
        sh = 1
        while sh < L:
            csum = csum + jnp.where(row >= sh, pltpu.roll(csum, sh, axis=0), 0.0)
            sh *= 2
        gates_t = _t128(gates)
        csum_t = _t128(csum)
        for h in range(ML_HEADS):
            sl = slice(h * ML_DK, (h + 1) * ML_DK)
            q, k, v = q_ref[bb, :, sl], k_ref[bb, :, sl], v_ref[bb, :, sl]
            b_col = csum[:, ML_HEADS + h:ML_HEADS + h + 1]
            b_row = csum_t[ML_HEADS + h:ML_HEADS + h + 1, :]
            ig_col = gates[:, h:h + 1]
            ig_row = gates_t[h:h + 1, :]
            m_prev = ms[bb, h:h + 1, 0:1]
            a_col = b_col + m_prev
            dmat = jnp.where(tril, b_col - b_row + ig_row, NEG)
            m_t = jnp.maximum(a_col, jnp.max(dmat, axis=-1, keepdims=True))
            w_inter = jnp.exp(a_col - m_t)
            wmat = jnp.exp(dmat - m_t)
            qk = lax.dot_general(q, k, (((1,), (1,)), ((), ())), preferred_element_type=F32)
            wqk = wmat * qk
            c_old = cs[bb, h]
            n_old = ns[bb, h:h + 1, :]
            num = (w_inter * jnp.dot(q, c_old.astype(BF16), preferred_element_type=F32)
                   + jnp.dot(wqk.astype(BF16), v, preferred_element_type=F32))
            den = (w_inter * jnp.sum(q.astype(F32) * n_old, axis=-1, keepdims=True)
                   + jnp.sum(wqk, axis=-1, keepdims=True))
            h_ref[bb, :, sl] = num / jnp.maximum(jnp.abs(den), jnp.exp(-m_t))
            m_new = m_t[L - 1:L, :]
            decay = jnp.exp(a_col[L - 1:L, :] - m_new)
            wl_col = jnp.exp(b_col[L - 1:L, :] - b_col + ig_col - m_new)
            kw = k.astype(F32) * wl_col
            cs[bb, h] = decay * c_old + lax.dot_general(kw.astype(BF16), v, (((0,), (0,)), ((), ())),
                                                        preferred_element_type=F32)
            ns[bb, h:h + 1, :] = decay * n_old + jnp.sum(kw, axis=0, keepdims=True)
            ms[bb, h:h + 1, :] = jnp.broadcast_to(m_new, (1, ML_DK))

    @pl.when(c == pl.num_programs(1) - 1)
    def _():
        c_ref[...] = cs[...]
        n_ref[...] = ns[...]
        m_ref[...] = ms[...]


def _mlstm(q, k, v, gates, c0, n0, m0, *, L):
    B, S, _ = q.shape
    bb = ML_BATCH_ROWS
    seq = lambda w: pl.BlockSpec((bb, L, w), lambda b, c: (b, c, 0))
    st4 = pl.BlockSpec((bb, ML_HEADS, ML_DK, ML_DK), lambda b, c: (b, 0, 0, 0))
    st3 = pl.BlockSpec((bb, ML_HEADS, ML_DK), lambda b, c: (b, 0, 0))
    return pl.pallas_call(
        functools.partial(_mlstm_kernel, L=L),
        grid=(B // bb, S // L),
        in_specs=[seq(ML_W), seq(ML_W), seq(ML_W), seq(GATE_COLS), st4, st3, st3],
        out_specs=[seq(ML_W), st4, st3, st3],
        out_shape=[jax.ShapeDtypeStruct((B, S, ML_W), F32),
                   jax.ShapeDtypeStruct((B, ML_HEADS, ML_DK, ML_DK), F32),
                   jax.ShapeDtypeStruct((B, ML_HEADS, ML_DK), F32),
                   jax.ShapeDtypeStruct((B, ML_HEADS, ML_DK), F32)],
        scratch_shapes=[pltpu.VMEM((bb, ML_HEADS, ML_DK, ML_DK), F32),
                        pltpu.VMEM((bb, ML_HEADS, ML_DK), F32),
                        pltpu.VMEM((bb, ML_HEADS, ML_DK), F32)],
        compiler_params=_params("parallel", "arbitrary"),
        name="mlstm",
    )(q, k, v, gates, c0, n0, m0)


def _merge_kernel(x_ref, att_ref, h_ref, mo_ref, ga_ref, gb_ref, wa_ref, wm_ref, wo_ref, o_ref):
    ml = (jax.nn.sigmoid(mo_ref[...]) * h_ref[...]).astype(BF16)
    a = jnp.dot(att_ref[...], wa_ref[...], preferred_element_type=F32)
    b = jnp.dot(ml, wm_ref[...], preferred_element_type=F32)
    merged = jax.nn.sigmoid(ga_ref[...]) * a + jax.nn.sigmoid(gb_ref[...]) * b
    o_ref[...] = x_ref[...] + jnp.dot(merged.astype(BF16), wo_ref[...], preferred_element_type=F32)


def _merge(x, att, h, mo, ga, gb, wa, wm, wo):
    n = x.shape[0]
    tm = min(ROW_TILE, n)
    row = lambda w: pl.BlockSpec((tm, w), lambda i: (i, 0))
    full = lambda a: pl.BlockSpec(a.shape, lambda i: (0, 0))
    return pl.pallas_call(
        _merge_kernel,
        grid=(n // tm,),
        in_specs=[row(D_MODEL), row(ATT_W), row(ML_W), row(ML_W), row(D_MODEL), row(D_MODEL),
                  full(wa), full(wm), full(wo)],
        out_specs=row(D_MODEL),
        out_shape=jax.ShapeDtypeStruct((n, D_MODEL), F32),
        compiler_params=_params("parallel"),
        name="merge",
    )(x, att, h, mo, ga, gb, wa, wm, wo)


def _top16(s, ids, big):
    vals, win = [], []
    for _ in range(PEER_TOPK):
        m = jnp.max(s, axis=0, keepdims=True)
        i = jnp.min(jnp.where(s == m, ids, big), axis=0, keepdims=True)
        vals.append(m)
        win.append(i)
        s = jnp.where(ids == i, -jnp.inf, s)
    return jnp.concatenate(vals, axis=0), jnp.concatenate(win, axis=0)


_PAIR_IDS = np.array([b for b in range(16)] + [a * 16 + b for a in range(1, 8) for b in range(8)]
                     + [a * 16 for a in range(8, 16)], np.int32)


def _select(rank, table):
    out = jnp.zeros_like(table)
    for a in range(PEER_TOPK):
        out = jnp.where(rank == a, table[a:a + 1], out)
    return out


def _route_kernel(x_ref, g_ref, wq_ref, keys_ref, pair_ref, xn_o, xt_o, idx_o, gw_o, st_ref):
    x = x_ref[...]
    xn = _rms(x, g_ref[...])
    xn_o[...] = xn.reshape(xn_o.shape)
    xt_o[...] = x.reshape(xt_o.shape)
    q = jnp.dot(xn.astype(BF16), wq_ref[...], preferred_element_type=F32)
    st_ref[...] = lax.dot_general(keys_ref[...], q.astype(BF16), (((1,), (1,)), ((), ())),
                                  preferred_element_type=F32)
    lane_tiles = st_ref.shape[1] // LANES
    key_ids = lax.broadcasted_iota(jnp.int32, (PEER_NKEYS, LANES), 0)

    def head_tile(it, carry):
        h = it // lane_tiles
        lanes = pl.ds(pl.multiple_of((it % lane_tiles) * LANES, LANES), LANES)
        r0 = pl.multiple_of(h * 2 * PEER_NKEYS, 2 * PEER_NKEYS)
        sv0, si0 = _top16(st_ref[pl.ds(r0, PEER_NKEYS), lanes], key_ids, PEER_NKEYS)
        sv1, si1 = _top16(st_ref[pl.ds(r0 + PEER_NKEYS, PEER_NKEYS), lanes], key_ids, PEER_NKEYS)
        half = PEER_TOPK // 2
        cand = jnp.concatenate(
            [jnp.broadcast_to(sv0[0:1], (PEER_TOPK, LANES)) + sv1]
            + [jnp.broadcast_to(sv0[a:a + 1], (half, LANES)) + sv1[0:half] for a in range(1, half)]
            + [sv0[half:] + jnp.broadcast_to(sv1[0:1], (half, LANES))], axis=0)
        fv, fi = _top16(cand, pair_ref[...], PEER_TOPK * PEER_TOPK)
        eidx = _select(fi >> 4, si0) * PEER_NKEYS + _select(fi & (PEER_TOPK - 1), si1)
        e = jnp.exp(fv - fv[0:1])
        o0 = pl.multiple_of(h * PEER_TOPK, PEER_TOPK)
        gw_o[pl.ds(o0, PEER_TOPK), lanes] = e / jnp.sum(e, axis=0, keepdims=True)
        idx_o[pl.ds(o0, PEER_TOPK), lanes] = eidx
        return carry

    lax.fori_loop(0, PEER_HEADS * lane_tiles, head_tile, 0)


def _route(x, g2, wq, keys_bd):
    n = x.shape[0]
    tm = min(ROW_TILE, n)
    full = lambda a: pl.BlockSpec(a.shape, lambda i: (0, 0))
    pair_ids = jnp.asarray(np.broadcast_to(_PAIR_IDS[:, None], (_PAIR_IDS.size, LANES)))
    return pl.pallas_call(
        _route_kernel,
        grid=(n // tm,),
        scratch_shapes=[pltpu.VMEM((PEER_HEADS * 2 * PEER_NKEYS, tm), F32)],
        in_specs=[pl.BlockSpec((tm, D_MODEL), lambda i: (i, 0)), full(g2), full(wq), full(keys_bd), full(pair_ids)],
        out_specs=[pl.BlockSpec((tm, SUBLANES, LANES), lambda i: (i, 0, 0)),
                   pl.BlockSpec((tm, SUBLANES, LANES), lambda i: (i, 0, 0)),
                   pl.BlockSpec((PEER_PICKS, tm), lambda i: (0, i)),
                   pl.BlockSpec((PEER_PICKS, tm), lambda i: (0, i))],
        out_shape=[jax.ShapeDtypeStruct((n, SUBLANES, LANES), F32),
                   jax.ShapeDtypeStruct((n, SUBLANES, LANES), F32),
                   jax.ShapeDtypeStruct((PEER_PICKS, n), jnp.int32),
                   jax.ShapeDtypeStruct((PEER_PICKS, n), F32)],
        compiler_params=_params("parallel"),
        name="peer_route",
    )(x, g2, wq, keys_bd, pair_ids)


def _apply_kernel(idx_hbm, gwt_ref, xn_ref, x_ref, fg_ref, tab_hbm, y_ref, idx_s, buf, gsem, isem, coef_ref, *, nblk):
    i = pl.program_id(0)
    ngroups = PEER_BLOCK // GROUP
    rows = GROUP * PEER_PICKS
    blk_words = PEER_BLOCK * PEER_PICKS
    assert ngroups % GATHER_SLOTS == 0 and GATHER_SLOTS == 2

    def idx_copy(blk, islot):
        return pltpu.make_async_copy(idx_hbm.at[pl.ds(blk * blk_words, blk_words)],
                                     idx_s.at[pl.ds(islot * blk_words, blk_words)], isem.at[islot])

    def row_copy(e, slot, r):
        return pltpu.make_async_copy(tab_hbm.at[e], buf.at[slot, r], gsem.at[slot])

    def wait_group(slot):
        pltpu.make_async_copy(tab_hbm.at[pl.ds(0, rows)], buf.at[slot], gsem.at[slot]).wait()

    islot = i % 2
    has_next = i + 1 < nblk

    @pl.when(i == 0)
    def _():
        idx_copy(0, 0).start()
        idx_copy(0, 0).wait()

        def body(r, carry):
            row_copy(idx_s[r], 0, r).start()
            return carry

        lax.fori_loop(0, rows, body, 0, unroll=8)

    @pl.when(has_next)
    def _():
        idx_copy(i + 1, 1 - islot).start()

    lane = lax.broadcasted_iota(jnp.int32, (PEER_PICKS, LANES), 1)
    sub = lax.broadcasted_iota(jnp.int32, (SUBLANES, LANES), 0)
    masks = {1: (sub & 1) == 0, 2: (sub & 2) == 0, 4: (sub & 4) == 0}

    def fold(a, b, h):
        return jnp.where(masks[h], a, b) + pltpu.roll(jnp.where(masks[h], b, a), h, axis=0)

    def row_sums(ps):
        c = [fold(ps[2 * m], ps[2 * m + 1], 1) for m in range(4)]
        d = [fold(c[0], c[1], 2), fold(c[2], c[3], 2)]
        return fold(d[0], d[1], 4)

    def do_group(g, slot, next_word0):
        nslot = 1 - slot
        wait_group(slot)
        t0 = pl.multiple_of(g * GROUP, GROUP)
        lane0 = pl.multiple_of((t0 // LANES) * LANES, LANES)
        gw_tile = gwt_ref[:, pl.ds(lane0, LANES)]
        ys = []
        for j in range(GROUP):
            for k in range(PEER_PICKS):
                r = j * PEER_PICKS + k
                row_copy(idx_s[next_word0 + r], nslot, r).start(priority=k % 2)
            x = xn_ref[t0 + j]
            sums = []
            for c in range(PEER_PICKS // SUBLANES):
                r0 = j * PEER_PICKS + c * SUBLANES
                sums.append(row_sums([buf[slot, r0 + k, 0:SUBLANES, :] * x for k in range(SUBLANES)]))
            act = jnp.sum(jnp.concatenate(sums, axis=0), axis=-1, keepdims=True)
            gelu = 0.5 * act * (1.0 + lax.erf(act * (2.0 ** -0.5)))
            gw_col = jnp.sum(jnp.where(lane == t0 - lane0 + j, gw_tile, 0.0), axis=-1, keepdims=True)
            coef_ref[...] = jnp.broadcast_to(gw_col * gelu, (PEER_PICKS, LANES))
            accs = [jnp.zeros((SUBLANES, LANES), F32) for _ in range(4)]
            for k in range(PEER_PICKS):
                accs[k % 4] = accs[k % 4] + coef_ref[k:k + 1, :] * buf[slot, j * PEER_PICKS + k, SUBLANES:, :]
            y = x_ref[t0 + j] + ((accs[0] + accs[1]) + (accs[2] + accs[3]))
            ms = jnp.sum(jnp.sum(y * y, axis=0, keepdims=True), axis=-1, keepdims=True) * (1.0 / D_MODEL)
            ys.append(y * lax.rsqrt(ms + EPS) * fg_ref[...])
        y_ref[pl.ds(t0, GROUP), :] = jnp.stack(ys, axis=0).reshape(GROUP, D_MODEL)

    def pair_body(gg, carry):
        g = 2 * gg
        do_group(g, 0, islot * blk_words + (g + 1) * rows)
        last = gg + 1 == ngroups // 2

        @pl.when(jnp.logical_and(last, has_next))
        def _():
            idx_copy(i + 1, 1 - islot).wait()

        do_group(g + 1, 1, jnp.where(last, jnp.where(has_next, 1 - islot, islot) * blk_words,
                                     islot * blk_words + (g + 2) * rows))
        return carry

    lax.fori_loop(0, ngroups // 2, pair_body, 0)

    @pl.when(jnp.logical_not(has_next))
    def _():
        wait_group(0)


def _apply(idx, gwt, xn, x, fg, table):
    n = x.shape[0]
    nblk = n // PEER_BLOCK
    tile = pl.BlockSpec((PEER_BLOCK, SUBLANES, LANES), lambda i: (i, 0, 0))
    return pl.pallas_call(
        functools.partial(_apply_kernel, nblk=nblk),
        grid=(nblk,),
        in_specs=[pl.BlockSpec(memory_space=pl.ANY), pl.BlockSpec((PEER_PICKS, PEER_BLOCK), lambda i: (0, i)),
                  tile, tile,
                  pl.BlockSpec(fg.shape, lambda i: (0, 0)), pl.BlockSpec(memory_space=pl.ANY)],
        out_specs=pl.BlockSpec((PEER_BLOCK, D_MODEL), lambda i: (i, 0)),
        out_shape=jax.ShapeDtypeStruct((n, D_MODEL), F32),
        scratch_shapes=[pltpu.SMEM((2 * PEER_BLOCK * PEER_PICKS,), jnp.int32),
                        pltpu.VMEM((GATHER_SLOTS, GROUP * PEER_PICKS, EXPERT_ROWS, LANES), F32),
                        pltpu.SemaphoreType.DMA((GATHER_SLOTS,)),
                        pltpu.SemaphoreType.DMA((2,)),
                        pltpu.VMEM((PEER_PICKS, LANES), F32)],
        compiler_params=_params("arbitrary"),
        name="peer_apply",
    )(idx, gwt, xn, x, fg, table)


def _head_tile_steps(it, st_ref, pair_ids, gw_dst, idx_dst, lane_tiles):
    h = it // lane_tiles
    lanes = pl.ds(pl.multiple_of((it % lane_tiles) * LANES, LANES), LANES)
    r0 = pl.multiple_of(h * 2 * PEER_NKEYS, 2 * PEER_NKEYS)
    key_ids = lax.broadcasted_iota(jnp.int32, (PEER_NKEYS, LANES), 0)
    s = [st_ref[pl.ds(r0, PEER_NKEYS), lanes], st_ref[pl.ds(r0 + PEER_NKEYS, PEER_NKEYS), lanes]]
    vals, wins = ([], []), ([], [])

    def take_max(x, ids, big):
        m = jnp.max(x, axis=0, keepdims=True)
        w = jnp.min(jnp.where(x == m, ids, big), axis=0, keepdims=True)
        return m, w, jnp.where(ids == w, -jnp.inf, x)

    for _ in range(PEER_TOPK):
        for p in range(2):
            m, w, s[p] = take_max(s[p], key_ids, PEER_NKEYS)
            vals[p].append(m)
            wins[p].append(w)
        yield
    sv0, sv1 = jnp.concatenate(vals[0], axis=0), jnp.concatenate(vals[1], axis=0)
    si0, si1 = jnp.concatenate(wins[0], axis=0), jnp.concatenate(wins[1], axis=0)
    half = PEER_TOPK // 2
    cand = jnp.concatenate(
        [jnp.broadcast_to(sv0[0:1], (PEER_TOPK, LANES)) + sv1]
        + [jnp.broadcast_to(sv0[a:a + 1], (half, LANES)) + sv1[0:half] for a in range(1, half)]
        + [sv0[half:] + jnp.broadcast_to(sv1[0:1], (half, LANES))], axis=0)
    fvals, fwins = [], []
    for r in range(PEER_TOPK):
        m, w, cand = take_max(cand, pair_ids, PEER_TOPK * PEER_TOPK)
        fvals.append(m)
        fwins.append(w)
        if r % 2:
            yield
    fv, fi = jnp.concatenate(fvals, axis=0), jnp.concatenate(fwins, axis=0)
    eidx = _select(fi >> 4, si0) * PEER_NKEYS + _select(fi & (PEER_TOPK - 1), si1)
    e = jnp.exp(fv - fv[0:1])
    o0 = pl.multiple_of(h * PEER_TOPK, PEER_TOPK)
    gw_dst[pl.ds(o0, PEER_TOPK), lanes] = e / jnp.sum(e, axis=0, keepdims=True)
    idx_dst[pl.ds(o0, PEER_TOPK), lanes] = eidx
    yield


def _fused_kernel(x0_ref, x1_ref, xnext_ref, g_ref, wq_ref, keys_ref, pair_ref, fg_ref, tab_hbm, y_ref,
                  st_ref, xn_t, x_t, gw_v, idx_tv, idx_rows, idx_s, buf, gsem, isem, coef_ref, *, nblk):
    i = pl.program_id(0)
    ngroups = PEER_BLOCK // GROUP
    rows = GROUP * PEER_PICKS
    lane_tiles = PEER_BLOCK // LANES
    head_tiles = PEER_HEADS * lane_tiles
    assert ngroups // 2 == head_tiles and nblk >= 2
    islot = i % 2
    a3 = i % 3
    w3 = (i + 2) % 3
    has_next = i + 1 < nblk
    has_next2 = i + 2 < nblk
    pair_ids = pair_ref[...]

    def scores(x, slot):
        xn = _rms(x, g_ref[...])
        xn_t[slot] = xn.reshape(xn_t.shape[1:])
        x_t[slot] = x.reshape(x_t.shape[1:])
        q = jnp.dot(xn.astype(BF16), wq_ref[...], preferred_element_type=F32)
        st_ref[...] = lax.dot_general(keys_ref[...], q.astype(BF16), (((1,), (1,)), ((), ())),
                                      preferred_element_type=F32)

    def publish_copy(smem_slot):
        return pltpu.make_async_copy(idx_rows, idx_s.at[pl.ds(smem_slot * PEER_BLOCK, PEER_BLOCK), :],
                                     isem.at[smem_slot])

    def row_copy(e, slot, r):
        return pltpu.make_async_copy(tab_hbm.at[e], buf.at[slot, r], gsem.at[slot])

    def wait_group(slot):
        pltpu.make_async_copy(tab_hbm.at[pl.ds(0, rows)], buf.at[slot], gsem.at[slot]).wait()

    @pl.when(i == 0)
    def _():
        for blk, x_ref in ((0, x0_ref), (1, x1_ref)):
            scores(x_ref[...], blk)

            def body(it, carry):
                for _ in _head_tile_steps(it, st_ref, pair_ids, gw_v.at[blk], idx_tv, lane_tiles):
                    pass
                return carry

            lax.fori_loop(0, head_tiles, body, 0)
            idx_rows[...] = idx_tv[...].T
            publish_copy(blk).start()
            publish_copy(blk).wait()

        def issue(r, carry):
            row_copy(idx_s[r // PEER_PICKS, r % PEER_PICKS], 0, r).start()
            return carry

        lax.fori_loop(0, rows, issue, 0, unroll=8)

    @pl.when(has_next2)
    def _():
        scores(xnext_ref[...], w3)

    lane = lax.broadcasted_iota(jnp.int32, (PEER_PICKS, LANES), 1)
    sub = lax.broadcasted_iota(jnp.int32, (SUBLANES, LANES), 0)
    masks = {1: (sub & 1) == 0, 2: (sub & 2) == 0, 4: (sub & 4) == 0}

    def fold(a, b, h):
        return jnp.where(masks[h], a, b) + pltpu.roll(jnp.where(masks[h], b, a), h, axis=0)

    def row_sums(ps):
        c = [fold(ps[2 * m], ps[2 * m + 1], 1) for m in range(4)]
        d = [fold(c[0], c[1], 2), fold(c[2], c[3], 2)]
        return fold(d[0], d[1], 4)

    def do_group(g, slot, next_row0, between):
        nslot = 1 - slot

        def issue_token(j):
            for k in range(PEER_PICKS):
                row_copy(idx_s[next_row0 + j, k], nslot, j * PEER_PICKS + k).start(priority=k % 2)

        for j in range(ISSUE_AHEAD):
            issue_token(j)
        wait_group(slot)
        t0 = pl.multiple_of(g * GROUP, GROUP)
        lane0 = pl.multiple_of((t0 // LANES) * LANES, LANES)
        gw_tile = gw_v[a3, :, pl.ds(lane0, LANES)]
        ys = []
        for j in range(GROUP):
            if j + ISSUE_AHEAD < GROUP:
                issue_token(j + ISSUE_AHEAD)
            x = xn_t[a3, t0 + j]
            sums = []
            for c in range(PEER_PICKS // SUBLANES):
                r0 = j * PEER_PICKS + c * SUBLANES
                sums.append(row_sums([buf[slot, r0 + k, 0:SUBLANES, :] * x for k in range(SUBLANES)]))
            act = jnp.sum(jnp.concatenate(sums, axis=0), axis=-1, keepdims=True)
            gelu = 0.5 * act * (1.0 + lax.erf(act * (2.0 ** -0.5)))
            gw_col = jnp.sum(jnp.where(lane == t0 - lane0 + j, gw_tile, 0.0), axis=-1, keepdims=True)
            coef_ref[...] = jnp.broadcast_to(gw_col * gelu, (PEER_PICKS, LANES))
            accs = [jnp.zeros((SUBLANES, LANES), F32) for _ in range(4)]
            for k in range(PEER_PICKS):
                accs[k % 4] = accs[k % 4] + coef_ref[k:k + 1, :] * buf[slot, j * PEER_PICKS + k, SUBLANES:, :]
            y = x_t[a3, t0 + j] + ((accs[0] + accs[1]) + (accs[2] + accs[3]))
            ms = jnp.sum(jnp.sum(y * y, axis=0, keepdims=True), axis=-1, keepdims=True) * (1.0 / D_MODEL)
            ys.append(y * lax.rsqrt(ms + EPS) * fg_ref[...])
            between()
        y_ref[pl.ds(t0, GROUP), :] = jnp.stack(ys, axis=0).reshape(GROUP, D_MODEL)

    def pair_body(gg, carry):
        g = 2 * gg
        last = gg + 1 == ngroups // 2

        @pl.when(jnp.logical_and(last, jnp.logical_and(has_next, i >= 1)))
        def _():
            publish_copy(1 - islot).wait()

        steps = _head_tile_steps(gg, st_ref, pair_ids, gw_v.at[w3], idx_tv, lane_tiles)

        def between():
            next(steps, None)
            next(steps, None)

        do_group(g, 0, islot * PEER_BLOCK + (g + 1) * GROUP, between)
        do_group(g + 1, 1, jnp.where(last, jnp.where(has_next, 1 - islot, islot) * PEER_BLOCK,
                                     islot * PEER_BLOCK + (g + 2) * GROUP), between)
        for _ in steps:
            pass
        return carry

    lax.fori_loop(0, ngroups // 2, pair_body, 0)

    @pl.when(has_next2)
    def _():
        idx_rows[...] = idx_tv[...].T
        publish_copy(islot).start()

    @pl.when(jnp.logical_not(has_next))
    def _():
        wait_group(0)


def _route_apply(x, g2, wq, keys_bd, fg, table):
    n = x.shape[0]
    nblk = n // PEER_BLOCK
    full = lambda a: pl.BlockSpec(a.shape, lambda i: (0,) * a.ndim)
    pair_ids = jnp.asarray(np.broadcast_to(_PAIR_IDS[:, None], (_PAIR_IDS.size, LANES)))
    blk = lambda f: pl.BlockSpec((PEER_BLOCK, D_MODEL), f)
    return pl.pallas_call(
        functools.partial(_fused_kernel, nblk=nblk),
        grid=(nblk,),
        in_specs=[blk(lambda i: (0, 0)), blk(lambda i: (1, 0)), blk(lambda i: (jnp.minimum(i + 2, nblk - 1), 0)),
                  full(g2), full(wq), full(keys_bd), full(pair_ids), full(fg), pl.BlockSpec(memory_space=pl.ANY)],
        out_specs=blk(lambda i: (i, 0)),
        out_shape=jax.ShapeDtypeStruct((n, D_MODEL), F32),
        scratch_shapes=[pltpu.VMEM((PEER_HEADS * 2 * PEER_NKEYS, PEER_BLOCK), F32),
                        pltpu.VMEM((3, PEER_BLOCK, SUBLANES, LANES), F32),
                        pltpu.VMEM((3, PEER_BLOCK, SUBLANES, LANES), F32),
                        pltpu.VMEM((3, PEER_PICKS, PEER_BLOCK), F32),
                        pltpu.VMEM((PEER_PICKS, PEER_BLOCK), jnp.int32),
                        pltpu.VMEM((PEER_BLOCK, PEER_PICKS), jnp.int32),
                        pltpu.SMEM((2 * PEER_BLOCK, PEER_PICKS), jnp.int32),
                        pltpu.VMEM((GATHER_SLOTS, GROUP * PEER_PICKS, EXPERT_ROWS, LANES), F32),
                        pltpu.SemaphoreType.DMA((GATHER_SLOTS,)),
                        pltpu.SemaphoreType.DMA((2,)),
                        pltpu.VMEM((PEER_PICKS, LANES), F32)],
        compiler_params=_params("arbitrary"),
        name="peer_route_apply",
    )(x, x, x, g2, wq, keys_bd, pair_ids, fg, table)


def _layer(x3, weights, att_kv, att_geom, ml_state, L):
    B, S, _ = x3.shape
    n = B * S
    x = x3.reshape(n, D_MODEL)
    (q, k, v, kf, vf, mq, mk, mv, mo, ga, gb, gates) = _inproj(
        x, weights["g1"], weights["w_main"], weights["b_main"], weights["wg_hi"], weights["wg_lo"],
        weights["bg"], weights["fb"])
    sh = lambda a: a.reshape(B, S, a.shape[-1])
    k_all, v_all = att_kv(sh(k), sh(v))
    att = _attention(sh(q), k_all, v_all, att_geom["bias"], L=att_geom["L"], KB=att_geom["KB"], pad=att_geom["pad"])
    h, c_new, n_new, m_new = _mlstm(sh(mq), sh(mk), sh(mv), sh(gates), *ml_state, L=L)
    x2 = _merge(x, att.reshape(n, ATT_W), h.reshape(n, ML_W), mo, ga, gb,
                weights["wa"], weights["wm"], weights["wo"])
    fg = weights["fg"].reshape(SUBLANES, LANES)
    if n >= 2 * PEER_BLOCK:
        y = _route_apply(x2, weights["g2"], weights["wq"], weights["keys_bd"], fg, weights["table"])
    else:
        xn2, x2_tiles, idx_t, gw_t = _route(x2, weights["g2"], weights["wq"], weights["keys_bd"])
        y = _apply(idx_t.T.reshape(n * PEER_PICKS), gw_t, xn2, x2_tiles, fg, weights["table"])
    return (y.reshape(B, S, D_MODEL), sh(kf).reshape(B, S, ATT_HEADS, ATT_HD), sh(vf).reshape(B, S, ATT_HEADS, ATT_HD),
            c_new, n_new, m_new[:, :, 0])


def kernel(x_prompt, x_sample, cache_att_k, cache_att_v, state_mlstm_C, state_mlstm_n, state_mlstm_m, norm1_g, w_in, b_in, ml_f_bias, att_rel_bias, w_att_branch, w_ml_branch, w_out, norm2_g, peer_wq, peer_sub_keys, peer_u, peer_v, final_g):
    depth = w_in.shape[0]
    assert depth == 1, "single-layer step"
    l = 0
    w = w_in[l]
    b = b_in[l]
    gate_lo, gate_hi = 7 * 512, 7 * 512 + 2 * ML_HEADS
    w_main = jnp.concatenate([w[:, :gate_lo], w[:, gate_hi:]], axis=1).astype(BF16)
    b_main = jnp.concatenate([b[:gate_lo], b[gate_hi:]])[None, :]
    wg = jnp.pad(w[:, gate_lo:gate_hi], ((0, 0), (0, GATE_COLS - 2 * ML_HEADS)))
    wg_hi = wg.astype(BF16)
    wg_lo = (wg - wg_hi.astype(F32)).astype(BF16)
    bg = jnp.pad(b[gate_lo:gate_hi], (0, GATE_COLS - 2 * ML_HEADS))[None, :]
    fb = jnp.pad(ml_f_bias[l], (ML_HEADS, GATE_COLS - 2 * ML_HEADS))[None, :]
    sk = peer_sub_keys[l].reshape(PEER_HEADS * 2, PEER_NKEYS, PEER_DKEY // 2)
    eye = jnp.eye(PEER_HEADS * 2, dtype=F32)
    keys_bd = (sk[:, :, None, :] * eye[:, None, :, None]).reshape(PEER_HEADS * 2 * PEER_NKEYS, D_MODEL).astype(BF16)
    weights = dict(
        g1=norm1_g[l][None, :], w_main=w_main, b_main=b_main, wg_hi=wg_hi, wg_lo=wg_lo, bg=bg, fb=fb,
        wa=w_att_branch[l].astype(BF16), wm=w_ml_branch[l].astype(BF16), wo=w_out[l].astype(BF16),
        g2=norm2_g[l][None, :], wq=peer_wq[l].astype(BF16), keys_bd=keys_bd, fg=final_g[None, :],
        table=jnp.concatenate([peer_u[l].reshape(-1, SUBLANES, LANES), peer_v[l].reshape(-1, SUBLANES, LANES)],
                              axis=1))

    Bs, T, _ = x_sample.shape
    P = cache_att_k.shape[2]
    assert P == ATT_CTX
    ck = cache_att_k[l].reshape(Bs, P, ATT_W).astype(BF16)
    cv = cache_att_v[l].reshape(Bs, P, ATT_W).astype(BF16)
    cat_kv = lambda k, v: (jnp.concatenate([ck, k], axis=1), jnp.concatenate([cv, v], axis=1))
    geom_s = dict(bias=_rel_bias(att_rel_bias[l], T, P + T), L=T, KB=P + T, pad=0)
    state_s = (state_mlstm_C[l], state_mlstm_n[l],
               jnp.broadcast_to(state_mlstm_m[l][:, :, None], (Bs, ML_HEADS, ML_DK)))
    y_s, k_s, v_s, c_s, n_s, m_s = _layer(x_sample, weights, cat_kv, geom_s, state_s, T)

    B, S, _ = x_prompt.shape
    pad_kv = lambda k, v: (jnp.pad(k, ((0, 0), (ATT_CTX, 0), (0, 0))), jnp.pad(v, ((0, 0), (ATT_CTX, 0), (0, 0))))
    geom_p = dict(bias=_pair_bias(_rel_bias(att_rel_bias[l], CHUNK, ATT_CTX + CHUNK)), L=2 * CHUNK,
                  KB=ATT_CTX + 2 * CHUNK, pad=ATT_CTX)
    state_p = (jnp.zeros((B, ML_HEADS, ML_DK, ML_DK), F32), jnp.zeros((B, ML_HEADS, ML_DK), F32),
               jnp.full((B, ML_HEADS, ML_DK), NEG, F32))
    y_p, k_p, v_p, c_p, n_p, m_p = _layer(x_prompt, weights, pad_kv, geom_p, state_p, CHUNK)
    keep = min(ATT_CTX, S)

    st = lambda a: a[None]
    return (y_p, y_s, st(k_p[:, -keep:]), st(v_p[:, -keep:]), st(c_p), st(n_p), st(m_p),
            st(k_s), st(v_s), st(c_s), st(n_s), st(m_s))
```

```python
import functools

import jax
import jax.numpy as jnp
import numpy as np
from jax import lax
from jax.experimental import pallas as pl
from jax.experimental.pallas import tpu as pltpu

F32 = jnp.float32
BF16 = jnp.bfloat16

D_MODEL = 1024
CHUNK = 64
ATT_HEADS = 8
ATT_HD = 64
ATT_W = ATT_HEADS * ATT_HD
ATT_CTX = 8 * CHUNK
MAX_REL = 128
ML_HEADS = 4
ML_DK = 128
ML_W = ML_HEADS * ML_DK
PEER_HEADS = 8
PEER_NKEYS = 128
PEER_DKEY = 128
PEER_TOPK = 16
PEER_PICKS = PEER_HEADS * PEER_TOPK
EPS = 1e-6
NEG = -1e30

LANES = 128
SUBLANES = 8
ROW_TILE = 256
GATE_COLS = LANES
MAIN_COLS = 7 * 512 + 2 * D_MODEL
ML_BATCH_ROWS = 2
GROUP = SUBLANES
PEER_BLOCK = 256
GATHER_SLOTS = 2
ISSUE_AHEAD = 4
EXPERT_ROWS = 2 * D_MODEL // LANES
VMEM_LIMIT = 56 * 1024 * 1024


def _params(*sem):
    return pltpu.CompilerParams(dimension_semantics=sem, vmem_limit_bytes=VMEM_LIMIT)


def _rms(x, g):
    return x * lax.rsqrt(jnp.mean(x * x, axis=-1, keepdims=True) + EPS) * g


def _inproj_kernel(x_ref, g_ref, w_ref, b_ref, wgh_ref, wgl_ref, bg_ref, fb_ref,
                   q_o, k_o, v_o, kf_o, vf_o, mq_o, mk_o, mv_o, mo_o, ga_o, gb_o, gt_o):
    xn = _rms(x_ref[...], g_ref[...])
    xh = xn.astype(BF16)

    def proj(off, width):
        return jnp.dot(xh, w_ref[:, off:off + width], preferred_element_type=F32) + b_ref[:, off:off + width]

    q_o[...] = (proj(0, 512) * (ATT_HD ** -0.5)).astype(BF16)
    k = proj(512, 512)
    kf_o[...] = k
    k_o[...] = k.astype(BF16)
    v = proj(1024, 512)
    vf_o[...] = v
    v_o[...] = v.astype(BF16)
    mq_o[...] = proj(1536, 512).astype(BF16)
    mk_o[...] = (proj(2048, 512) * (ML_DK ** -0.5)).astype(BF16)
    mv_o[...] = proj(2560, 512).astype(BF16)
    mo_o[...] = proj(3072, 512)
    ga_o[...] = proj(3584, D_MODEL)
    gb_o[...] = proj(3584 + D_MODEL, D_MODEL)
    xl = (xn - xh.astype(F32)).astype(BF16)
    zg = (jnp.dot(xh, wgh_ref[...], preferred_element_type=F32)
          + jnp.dot(xh, wgl_ref[...], preferred_element_type=F32)
          + jnp.dot(xl, wgh_ref[...], preferred_element_type=F32)) + bg_ref[...]
    t = zg + fb_ref[...]
    log_sig = jnp.minimum(t, 0.0) - jnp.log1p(jnp.exp(-jnp.abs(t)))
    lane = lax.broadcasted_iota(jnp.int32, zg.shape, 1)
    gt_o[...] = jnp.where(lane < ML_HEADS, zg, log_sig)


def _inproj(x, g1, w_main, b_main, wg_hi, wg_lo, bg, fb):
    n = x.shape[0]
    tm = min(ROW_TILE, n)
    row = lambda w: pl.BlockSpec((tm, w), lambda i: (i, 0))
    full = lambda a: pl.BlockSpec(a.shape, lambda i: (0, 0))
    widths = [(512, BF16)] * 3 + [(512, F32)] * 2 + [(512, BF16)] * 3 + [(512, F32), (D_MODEL, F32), (D_MODEL, F32),
                                                                     (GATE_COLS, F32)]
    return pl.pallas_call(
        _inproj_kernel,
        grid=(n // tm,),
        in_specs=[row(D_MODEL), full(g1), full(w_main), full(b_main), full(wg_hi), full(wg_lo), full(bg), full(fb)],
        out_specs=[row(w) for w, _ in widths],
        out_shape=[jax.ShapeDtypeStruct((n, w), dt) for w, dt in widths],
        compiler_params=_params("parallel"),
        name="inproj",
    )(x, g1, w_main, b_main, wg_hi, wg_lo, bg, fb)


def _attn_kernel(q_ref, k_ref, v_ref, bias_ref, o_ref, *, L, KB, pad):
    c = pl.program_id(1)
    start = pl.multiple_of(c * L, L)
    kb = k_ref[0, pl.ds(start, KB), :]
    vb = v_ref[0, pl.ds(start, KB), :]
    q = q_ref[0].astype(F32)
    col = lax.broadcasted_iota(jnp.int32, (1, KB), 1)
    valid = (start + col) >= pad
    lo = lax.broadcasted_iota(jnp.int32, (L, LANES), 1) < ATT_HD
    for hp in range(ATT_HEADS // 2):
        sl = slice(hp * LANES, (hp + 1) * LANES)
        qp, kp, vp = q[:, sl], kb[:, sl], vb[:, sl]
        qs = jnp.concatenate([jnp.where(lo, qp, 0.0), jnp.where(lo, 0.0, qp)], axis=0).astype(BF16)
        s = lax.dot_general(qs, kp, (((1,), (1,)), ((), ())), preferred_element_type=F32)
        s = jnp.where(valid, s + bias_ref[hp], NEG)
        e = jnp.exp(s - jnp.max(s, axis=-1, keepdims=True))
        pv = jnp.dot(e.astype(BF16), vp, preferred_element_type=F32) / jnp.sum(e, axis=-1, keepdims=True)
        o_ref[0, :, sl] = jnp.where(lo, pv[:L], pv[L:]).astype(BF16)


def _attention(q, k, v, bias, *, L, KB, pad):
    B, S, _ = q.shape
    ktot = k.shape[1]
    bias = bias.reshape(ATT_HEADS // 2, 2 * L, KB)
    return pl.pallas_call(
        functools.partial(_attn_kernel, L=L, KB=KB, pad=pad),
        grid=(B, S // L),
        in_specs=[pl.BlockSpec((1, L, ATT_W), lambda b, c: (b, c, 0)),
                  pl.BlockSpec((1, ktot, ATT_W), lambda b, c: (b, 0, 0)),
                  pl.BlockSpec((1, ktot, ATT_W), lambda b, c: (b, 0, 0)),
                  pl.BlockSpec(bias.shape, lambda b, c: (0, 0, 0))],
        out_specs=pl.BlockSpec((1, L, ATT_W), lambda b, c: (b, c, 0)),
        out_shape=jax.ShapeDtypeStruct((B, S, ATT_W), BF16),
        compiler_params=_params("parallel", "arbitrary"),
        name="attention",
    )(q, k, v, bias)


def _rel_bias(table, L, KB):
    m = np.arange(KB + L - 1)
    g = table[np.clip(L - 1 + ATT_CTX - m, -MAX_REL, MAX_REL) + MAX_REL].T
    return jnp.stack([g[:, L - 1 - l:L - 1 - l + KB] for l in range(L)], axis=1)


def _pair_bias(bias):
    h, l, _ = bias.shape
    neg = jnp.full((h, l, l), NEG, bias.dtype)
    return jnp.concatenate([jnp.concatenate([bias, neg], axis=2), jnp.concatenate([neg, bias], axis=2)], axis=1)


def _t128(x):
    rows = x.shape[0]
    if rows < LANES:
        x = jnp.concatenate([x, jnp.zeros((LANES - rows, LANES), x.dtype)], axis=0)
    return x.T[:, :rows]


def _mlstm_kernel(q_ref, k_ref, v_ref, g_ref, c0_ref, n0_ref, m0_ref,
                  h_ref, c_ref, n_ref, m_ref, cs, ns, ms, *, L):
    c = pl.program_id(1)

    @pl.when(c == 0)
    def _():
        cs[...] = c0_ref[...]
        ns[...] = n0_ref[...]
        ms[...] = m0_ref[...]

    tril = lax.broadcasted_iota(jnp.int32, (L, L), 0) >= lax.broadcasted_iota(jnp.int32, (L, L), 1)
    for bb in range(q_ref.shape[0]):
        gates = g_ref[bb]
        row = lax.broadcasted_iota(jnp.int32, gates.shape, 0)
        csum = gates
        sh = 1
        while sh < L:
            csum = csum + jnp.where(row >= sh, pltpu.roll(csum, sh, axis=0), 0.0)
            sh *= 2
        gates_t = _t128(gates)
        csum_t = _t128(csum)
        for h in range(ML_HEADS):
            sl = slice(h * ML_DK, (h + 1) * ML_DK)
            q, k, v = q_ref[bb, :, sl], k_ref[bb, :, sl], v_ref[bb, :, sl]
            b_col = csum[:, ML_HEADS + h:ML_HEADS + h + 1]
            b_row = csum_t[ML_HEADS + h:ML_HEADS + h + 1, :]
            ig_col = gates[:, h:h + 1]
            ig_row = gates_t[h:h + 1, :]
            m_prev = ms[bb, h:h + 1, 0:1]
            a_col = b_col + m_prev
            dmat = jnp.where(tril, b_col - b_row + ig_row, NEG)
            m_t = jnp.maximum(a_col, jnp.max(dmat, axis=-1, keepdims=True))
            w_inter = jnp.exp(a_col - m_t)
            wmat = jnp.exp(dmat - m_t)
            qk = lax.dot_general(q, k, (((1,), (1,)), ((), ())), preferred_element_type=F32)
            wqk = wmat * qk
            c_old = cs[bb, h]
            n_old = ns[bb, h:h + 1, :]
            num = (w_inter * jnp.dot(q, c_old.astype(BF16), preferred_element_type=F32)
                   + jnp.dot(wqk.astype(BF16), v, preferred_element_type=F32))
            den = (w_inter * jnp.sum(q.astype(F32) * n_old, axis=-1, keepdims=True)
                   + jnp.sum(wqk, axis=-1, keepdims=True))
            h_ref[bb, :, sl] = num / jnp.maximum(jnp.abs(den), jnp.exp(-m_t))
            m_new = m_t[L - 1:L, :]
            decay = jnp.exp(a_col[L - 1:L, :] - m_new)
            wl_col = jnp.exp(b_col[L - 1:L, :] - b_col + ig_col - m_new)
            kw = k.astype(F32) * wl_col
            cs[bb, h] = decay * c_old + lax.dot_general(kw.astype(BF16), v, (((0,), (0,)), ((), ())),
                                                        preferred_element_type=F32)
            ns[bb, h:h + 1, :] = decay * n_old + jnp.sum(kw, axis=0, keepdims=True)
            ms[bb, h:h + 1, :] = jnp.broadcast_to(m_new, (1, ML_DK))

    @pl.when(c == pl.num_programs(1) - 1)
    def _():
        c_ref[...] = cs[...]
        n_ref[...] = ns[...]
        m_ref[...] = ms[...]


def _mlstm(q, k, v, gates, c0, n0, m0, *, L):
    B, S, _ = q.shape
    bb = ML_BATCH_ROWS
    seq = lambda w: pl.BlockSpec((bb, L, w), lambda b, c: (b, c, 0))
    st4 = pl.BlockSpec((bb, ML_HEADS, ML_DK, ML_DK), lambda b, c: (b, 0, 0, 0))
    st3 = pl.BlockSpec((bb, ML_HEADS, ML_DK), lambda b, c: (b, 0, 0))
    return pl.pallas_call(
        functools.partial(_mlstm_kernel, L=L),
        grid=(B // bb, S // L),
        in_specs=[seq(ML_W), seq(ML_W), seq(ML_W), seq(GATE_COLS), st4, st3, st3],
        out_specs=[seq(ML_W), st4, st3, st3],
        out_shape=[jax.ShapeDtypeStruct((B, S, ML_W), F32),
                   jax.ShapeDtypeStruct((B, ML_HEADS, ML_DK, ML_DK), F32),
                   jax.ShapeDtypeStruct((B, ML_HEADS, ML_DK), F32),
                   jax.ShapeDtypeStruct((B, ML_HEADS, ML_DK), F32)],
        scratch_shapes=[pltpu.VMEM((bb, ML_HEADS, ML_DK, ML_DK), F32),
                        pltpu.VMEM((bb, ML_HEADS, ML_DK), F32),
                        pltpu.VMEM((bb, ML_HEADS, ML_DK), F32)],
        compiler_params=_params("parallel", "arbitrary"),
        name="mlstm",
    )(q, k, v, gates, c0, n0, m0)


def _merge_kernel(x_ref, att_ref, h_ref, mo_ref, ga_ref, gb_ref, wa_ref, wm_ref, wo_ref, o_ref):
    ml = (jax.nn.sigmoid(mo_ref[...]) * h_ref[...]).astype(BF16)
    a = jnp.dot(att_ref[...], wa_ref[...], preferred_element_type=F32)
    b = jnp.dot(ml, wm_ref[...], preferred_element_type=F32)
    merged = jax.nn.sigmoid(ga_ref[...]) * a + jax.nn.sigmoid(gb_ref[...]) * b
    o_ref[...] = x_ref[...] + jnp.dot(merged.astype(BF16), wo_ref[...], preferred_element_type=F32)


def _merge(x, att, h, mo, ga, gb, wa, wm, wo):
    n = x.shape[0]
    tm = min(ROW_TILE, n)
    row = lambda w: pl.BlockSpec((tm, w), lambda i: (i, 0))
    full = lambda a: pl.BlockSpec(a.shape, lambda i: (0, 0))
    return pl.pallas_call(
        _merge_kernel,
        grid=(n // tm,),
        in_specs=[row(D_MODEL), row(ATT_W), row(ML_W), row(ML_W), row(D_MODEL), row(D_MODEL),
                  full(wa), full(wm), full(wo)],
        out_specs=row(D_MODEL),
        out_shape=jax.ShapeDtypeStruct((n, D_MODEL), F32),
        compiler_params=_params("parallel"),
        name="merge",
    )(x, att, h, mo, ga, gb, wa, wm, wo)


def _top16(s, ids, big):
    vals, win = [], []
    for _ in range(PEER_TOPK):
        m = jnp.max(s, axis=0, keepdims=True)
        i = jnp.min(jnp.where(s == m, ids, big), axis=0, keepdims=True)
        vals.append(m)
        win.append(i)
        s = jnp.where(ids == i, -jnp.inf, s)
    return jnp.concatenate(vals, axis=0), jnp.concatenate(win, axis=0)


_PAIR_IDS = np.array([b for b in range(16)] + [a * 16 + b for a in range(1, 8) for b in range(8)]
                     + [a * 16 for a in range(8, 16)], np.int32)


def _select(rank, table):
    out = jnp.zeros_like(table)
    for a in range(PEER_TOPK):
        out = jnp.where(rank == a, table[a:a + 1], out)
    return out


def _route_kernel(x_ref, g_ref, wq_ref, keys_ref, pair_ref, xn_o, xt_o, idx_o, gw_o, st_ref):
    x = x_ref[...]
    xn = _rms(x, g_ref[...])
    xn_o[...] = xn.reshape(xn_o.shape)
    xt_o[...] = x.reshape(xt_o.shape)
    q = jnp.dot(xn.astype(BF16), wq_ref[...], preferred_element_type=F32)
    st_ref[...] = lax.dot_general(keys_ref[...], q.astype(BF16), (((1,), (1,)), ((), ())),
                                  preferred_element_type=F32)
    lane_tiles = st_ref.shape[1] // LANES
    key_ids = lax.broadcasted_iota(jnp.int32, (PEER_NKEYS, LANES), 0)

    def head_tile(it, carry):
        h = it // lane_tiles
        lanes = pl.ds(pl.multiple_of((it % lane_tiles) * LANES, LANES), LANES)
        r0 = pl.multiple_of(h * 2 * PEER_NKEYS, 2 * PEER_NKEYS)
        sv0, si0 = _top16(st_ref[pl.ds(r0, PEER_NKEYS), lanes], key_ids, PEER_NKEYS)
        sv1, si1 = _top16(st_ref[pl.ds(r0 + PEER_NKEYS, PEER_NKEYS), lanes], key_ids, PEER_NKEYS)
        half = PEER_TOPK // 2
        cand = jnp.concatenate(
            [jnp.broadcast_to(sv0[0:1], (PEER_TOPK, LANES)) + sv1]
            + [jnp.broadcast_to(sv0[a:a + 1], (half, LANES)) + sv1[0:half] for a in range(1, half)]
            + [sv0[half:] + jnp.broadcast_to(sv1[0:1], (half, LANES))], axis=0)
        fv, fi = _top16(cand, pair_ref[...], PEER_TOPK * PEER_TOPK)
        eidx = _select(fi >> 4, si0) * PEER_NKEYS + _select(fi & (PEER_TOPK - 1), si1)
        e = jnp.exp(fv - fv[0:1])
        o0 = pl.multiple_of(h * PEER_TOPK, PEER_TOPK)
        gw_o[pl.ds(o0, PEER_TOPK), lanes] = e / jnp.sum(e, axis=0, keepdims=True)
        idx_o[pl.ds(o0, PEER_TOPK), lanes] = eidx
        return carry

    lax.fori_loop(0, PEER_HEADS * lane_tiles, head_tile, 0)


def _route(x, g2, wq, keys_bd):
    n = x.shape[0]
    tm = min(ROW_TILE, n)
    full = lambda a: pl.BlockSpec(a.shape, lambda i: (0, 0))
    pair_ids = jnp.asarray(np.broadcast_to(_PAIR_IDS[:, None], (_PAIR_IDS.size, LANES)))
    return pl.pallas_call(
        _route_kernel,
        grid=(n // tm,),
        scratch_shapes=[pltpu.VMEM((PEER_HEADS * 2 * PEER_NKEYS, tm), F32)],
        in_specs=[pl.BlockSpec((tm, D_MODEL), lambda i: (i, 0)), full(g2), full(wq), full(keys_bd), full(pair_ids)],
        out_specs=[pl.BlockSpec((tm, SUBLANES, LANES), lambda i: (i, 0, 0)),
                   pl.BlockSpec((tm, SUBLANES, LANES), lambda i: (i, 0, 0)),
                   pl.BlockSpec((PEER_PICKS, tm), lambda i: (0, i)),
                   pl.BlockSpec((PEER_PICKS, tm), lambda i: (0, i))],
        out_shape=[jax.ShapeDtypeStruct((n, SUBLANES, LANES), F32),
                   jax.ShapeDtypeStruct((n, SUBLANES, LANES), F32),
                   jax.ShapeDtypeStruct((PEER_PICKS, n), jnp.int32),
                   jax.ShapeDtypeStruct((PEER_PICKS, n), F32)],
        compiler_params=_params("parallel"),
        name="peer_route",
    )(x, g2, wq, keys_bd, pair_ids)


def _apply_kernel(idx_hbm, gwt_ref, xn_ref, x_ref, fg_ref, tab_hbm, y_ref, idx_s, buf, gsem, isem, coef_ref, *, nblk):
    i = pl.program_id(0)
    ngroups = PEER_BLOCK // GROUP
    rows = GROUP * PEER_PICKS
    blk_words = PEER_BLOCK * PEER_PICKS
    assert ngroups % GATHER_SLOTS == 0 and GATHER_SLOTS == 2

    def idx_copy(blk, islot):
        return pltpu.make_async_copy(idx_hbm.at[pl.ds(blk * blk_words, blk_words)],
                                     idx_s.at[pl.ds(islot * blk_words, blk_words)], isem.at[islot])

    def row_copy(e, slot, r):
        return pltpu.make_async_copy(tab_hbm.at[e], buf.at[slot, r], gsem.at[slot])

    def wait_group(slot):
        pltpu.make_async_copy(tab_hbm.at[pl.ds(0, rows)], buf.at[slot], gsem.at[slot]).wait()

    islot = i % 2
    has_next = i + 1 < nblk

    @pl.when(i == 0)
    def _():
        idx_copy(0, 0).start()
        idx_copy(0, 0).wait()

        def body(r, carry):
            row_copy(idx_s[r], 0, r).start()
            return carry

        lax.fori_loop(0, rows, body, 0, unroll=8)

    @pl.when(has_next)
    def _():
        idx_copy(i + 1, 1 - islot).start()

    lane = lax.broadcasted_iota(jnp.int32, (PEER_PICKS, LANES), 1)
    sub = lax.broadcasted_iota(jnp.int32, (SUBLANES, LANES), 0)
    masks = {1: (sub & 1) == 0, 2: (sub & 2) == 0, 4: (sub & 4) == 0}

    def fold(a, b, h):
        return jnp.where(masks[h], a, b) + pltpu.roll(jnp.where(masks[h], b, a), h, axis=0)

    def row_sums(ps):
        c = [fold(ps[2 * m], ps[2 * m + 1], 1) for m in range(4)]
        d = [fold(c[0], c[1], 2), fold(c[2], c[3], 2)]
        return fold(d[0], d[1], 4)

    def do_group(g, slot, next_word0):
        nslot = 1 - slot
        wait_group(slot)
        t0 = pl.multiple_of(g * GROUP, GROUP)
        lane0 = pl.multiple_of((t0 // LANES) * LANES, LANES)
        gw_tile = gwt_ref[:, pl.ds(lane0, LANES)]
        ys = []
        for j in range(GROUP):
            for k in range(PEER_PICKS):
                r = j * PEER_PICKS + k
                row_copy(idx_s[next_word0 + r], nslot, r).start(priority=k % 2)
            x = xn_ref[t0 + j]
            sums = []
            for c in range(PEER_PICKS // SUBLANES):
                r0 = j * PEER_PICKS + c * SUBLANES
                sums.append(row_sums([buf[slot, r0 + k, 0:SUBLANES, :] * x for k in range(SUBLANES)]))
            act = jnp.sum(jnp.concatenate(sums, axis=0), axis=-1, keepdims=True)
            gelu = 0.5 * act * (1.0 + lax.erf(act * (2.0 ** -0.5)))
            gw_col = jnp.sum(jnp.where(lane == t0 - lane0 + j, gw_tile, 0.0), axis=-1, keepdims=True)
            coef_ref[...] = jnp.broadcast_to(gw_col * gelu, (PEER_PICKS, LANES))
            accs = [jnp.zeros((SUBLANES, LANES), F32) for _ in range(4)]
            for k in range(PEER_PICKS):
                accs[k % 4] = accs[k % 4] + coef_ref[k:k + 1, :] * buf[slot, j * PEER_PICKS + k, SUBLANES:, :]
            y = x_ref[t0 + j] + ((accs[0] + accs[1]) + (accs[2] + accs[3]))
            ms = jnp.sum(jnp.sum(y * y, axis=0, keepdims=True), axis=-1, keepdims=True) * (1.0 / D_MODEL)
            ys.append(y * lax.rsqrt(ms + EPS) * fg_ref[...])
        y_ref[pl.ds(t0, GROUP), :] = jnp.stack(ys, axis=0).reshape(GROUP, D_MODEL)

    def pair_body(gg, carry):
        g = 2 * gg
        do_group(g, 0, islot * blk_words + (g + 1) * rows)
        last = gg + 1 == ngroups // 2

        @pl.when(jnp.logical_and(last, has_next))
        def _():
            idx_copy(i + 1, 1 - islot).wait()

        do_group(g + 1, 1, jnp.where(last, jnp.where(has_next, 1 - islot, islot) * blk_words,
                                     islot * blk_words + (g + 2) * rows))
        return carry

    lax.fori_loop(0, ngroups // 2, pair_body, 0)

    @pl.when(jnp.logical_not(has_next))
    def _():
        wait_group(0)


def _apply(idx, gwt, xn, x, fg, table):
    n = x.shape[0]
    nblk = n // PEER_BLOCK
    tile = pl.BlockSpec((PEER_BLOCK, SUBLANES, LANES), lambda i: (i, 0, 0))
    return pl.pallas_call(
        functools.partial(_apply_kernel, nblk=nblk),
        grid=(nblk,),
        in_specs=[pl.BlockSpec(memory_space=pl.ANY), pl.BlockSpec((PEER_PICKS, PEER_BLOCK), lambda i: (0, i)),
                  tile, tile,
                  pl.BlockSpec(fg.shape, lambda i: (0, 0)), pl.BlockSpec(memory_space=pl.ANY)],
        out_specs=pl.BlockSpec((PEER_BLOCK, D_MODEL), lambda i: (i, 0)),
        out_shape=jax.ShapeDtypeStruct((n, D_MODEL), F32),
        scratch_shapes=[pltpu.SMEM((2 * PEER_BLOCK * PEER_PICKS,), jnp.int32),
                        pltpu.VMEM((GATHER_SLOTS, GROUP * PEER_PICKS, EXPERT_ROWS, LANES), F32),
                        pltpu.SemaphoreType.DMA((GATHER_SLOTS,)),
                        pltpu.SemaphoreType.DMA((2,)),
                        pltpu.VMEM((PEER_PICKS, LANES), F32)],
        compiler_params=_params("arbitrary"),
        name="peer_apply",
    )(idx, gwt, xn, x, fg, table)


def _head_tile_steps(it, st_ref, pair_ids, gw_dst, idx_dst, lane_tiles):
    h = it // lane_tiles
    lanes = pl.ds(pl.multiple_of((it % lane_tiles) * LANES, LANES), LANES)
    r0 = pl.multiple_of(h * 2 * PEER_NKEYS, 2 * PEER_NKEYS)
    key_ids = lax.broadcasted_iota(jnp.int32, (PEER_NKEYS, LANES), 0)
    s = [st_ref[pl.ds(r0, PEER_NKEYS), lanes], st_ref[pl.ds(r0 + PEER_NKEYS, PEER_NKEYS), lanes]]
    vals, wins = ([], []), ([], [])

    def take_max(x, ids, big):
        m = jnp.max(x, axis=0, keepdims=True)
        w = jnp.min(jnp.where(x == m, ids, big), axis=0, keepdims=True)
        return m, w, jnp.where(ids == w, -jnp.inf, x)

    for _ in range(PEER_TOPK):
        for p in range(2):
            m, w, s[p] = take_max(s[p], key_ids, PEER_NKEYS)
            vals[p].append(m)
            wins[p].append(w)
        yield
    sv0, sv1 = jnp.concatenate(vals[0], axis=0), jnp.concatenate(vals[1], axis=0)
    si0, si1 = jnp.concatenate(wins[0], axis=0), jnp.concatenate(wins[1], axis=0)
    half = PEER_TOPK // 2
    cand = jnp.concatenate(
        [jnp.broadcast_to(sv0[0:1], (PEER_TOPK, LANES)) + sv1]
        + [jnp.broadcast_to(sv0[a:a + 1], (half, LANES)) + sv1[0:half] for a in range(1, half)]
        + [sv0[half:] + jnp.broadcast_to(sv1[0:1], (half, LANES))], axis=0)
    fvals, fwins = [], []
    for r in range(PEER_TOPK):
        m, w, cand = take_max(cand, pair_ids, PEER_TOPK * PEER_TOPK)
        fvals.append(m)
        fwins.append(w)
        if r % 2:
            yield
    fv, fi = jnp.concatenate(fvals, axis=0), jnp.concatenate(fwins, axis=0)
    eidx = _select(fi >> 4, si0) * PEER_NKEYS + _select(fi & (PEER_TOPK - 1), si1)
    e = jnp.exp(fv - fv[0:1])
    o0 = pl.multiple_of(h * PEER_TOPK, PEER_TOPK)
    gw_dst[pl.ds(o0, PEER_TOPK), lanes] = e / jnp.sum(e, axis=0, keepdims=True)
    idx_dst[pl.ds(o0, PEER_TOPK), lanes] = eidx
    yield


def _fused_kernel(x0_ref, x1_ref, xnext_ref, g_ref, wq_ref, keys_ref, pair_ref, fg_ref, tab_hbm, y_ref,
                  st_ref, xn_t, x_t, gw_v, idx_tv, idx_rows, idx_s, buf, gsem, isem, coef_ref, *, nblk):
    i = pl.program_id(0)
    ngroups = PEER_BLOCK // GROUP
    rows = GROUP * PEER_PICKS
    lane_tiles = PEER_BLOCK // LANES
    head_tiles = PEER_HEADS * lane_tiles
    assert ngroups // 2 == head_tiles and nblk >= 2
    islot = i % 2
    a3 = i % 3
    w3 = (i + 2) % 3
    has_next = i + 1 < nblk
    has_next2 = i + 2 < nblk
    pair_ids = pair_ref[...]

    def scores(x, slot):
        xn = _rms(x, g_ref[...])
        xn_t[slot] = xn.reshape(xn_t.shape[1:])
        x_t[slot] = x.reshape(x_t.shape[1:])
        q = jnp.dot(xn.astype(BF16), wq_ref[...], preferred_element_type=F32)
        st_ref[...] = lax.dot_general(keys_ref[...], q.astype(BF16), (((1,), (1,)), ((), ())),
                                      preferred_element_type=F32)

    def publish_copy(smem_slot):
        return pltpu.make_async_copy(idx_rows, idx_s.at[pl.ds(smem_slot * PEER_BLOCK, PEER_BLOCK), :],
                                     isem.at[smem_slot])

    def row_copy(e, slot, r):
        return pltpu.make_async_copy(tab_hbm.at[e], buf.at[slot, r], gsem.at[slot])

    def wait_group(slot):
        pltpu.make_async_copy(tab_hbm.at[pl.ds(0, rows)], buf.at[slot], gsem.at[slot]).wait()

    @pl.when(i == 0)
    def _():
        for blk, x_ref in ((0, x0_ref), (1, x1_ref)):
            scores(x_ref[...], blk)

            def body(it, carry):
                for _ in _head_tile_steps(it, st_ref, pair_ids, gw_v.at[blk], idx_tv, lane_tiles):
                    pass
                return carry

            lax.fori_loop(0, head_tiles, body, 0)
            idx_rows[...] = idx_tv[...].T
            publish_copy(blk).start()
            publish_copy(blk).wait()

        def issue(r, carry):
            row_copy(idx_s[r // PEER_PICKS, r % PEER_PICKS], 0, r).start()
            return carry

        lax.fori_loop(0, rows, issue, 0, unroll=8)

    @pl.when(has_next2)
    def _():
        scores(xnext_ref[...], w3)

    lane = lax.broadcasted_iota(jnp.int32, (PEER_PICKS, LANES), 1)
    sub = lax.broadcasted_iota(jnp.int32, (SUBLANES, LANES), 0)
    masks = {1: (sub & 1) == 0, 2: (sub & 2) == 0, 4: (sub & 4) == 0}

    def fold(a, b, h):
        return jnp.where(masks[h], a, b) + pltpu.roll(jnp.where(masks[h], b, a), h, axis=0)

    def row_sums(ps):
        c = [fold(ps[2 * m], ps[2 * m + 1], 1) for m in range(4)]
        d = [fold(c[0], c[1], 2), fold(c[2], c[3], 2)]
        return fold(d[0], d[1], 4)

    def do_group(g, slot, next_row0, between):
        nslot = 1 - slot

        def issue_steps(j):
            for k in range(PEER_PICKS):
                row_copy(idx_s[next_row0 + j, k], nslot, j * PEER_PICKS + k).start(priority=k % 2)
                if k % 4 == 3:
                    yield

        for j in range(ISSUE_AHEAD):
            for _ in issue_steps(j):
                pass
        wait_group(slot)
        t0 = pl.multiple_of(g * GROUP, GROUP)
        lane0 = pl.multiple_of((t0 // LANES) * LANES, LANES)
        gw_tile = gw_v[a3, :, pl.ds(lane0, LANES)]
        ys = []
        for j in range(GROUP):
            issue = issue_steps(j + ISSUE_AHEAD) if j + ISSUE_AHEAD < GROUP else iter(())
            x = xn_t[a3, t0 + j]
            sums = []
            for c in range(PEER_PICKS // SUBLANES):
                next(issue, None)
                r0 = j * PEER_PICKS + c * SUBLANES
                sums.append(row_sums([buf[slot, r0 + k, 0:SUBLANES, :] * x for k in range(SUBLANES)]))
            act = jnp.sum(jnp.concatenate(sums, axis=0), axis=-1, keepdims=True)
            gelu = 0.5 * act * (1.0 + lax.erf(act * (2.0 ** -0.5)))
            gw_col = jnp.sum(jnp.where(lane == t0 - lane0 + j, gw_tile, 0.0), axis=-1, keepdims=True)
            coef_ref[...] = jnp.broadcast_to(gw_col * gelu, (PEER_PICKS, LANES))
            accs = [jnp.zeros((SUBLANES, LANES), F32) for _ in range(4)]
            for k in range(PEER_PICKS):
                if k % SUBLANES == 0:
                    next(issue, None)
                accs[k % 4] = accs[k % 4] + coef_ref[k:k + 1, :] * buf[slot, j * PEER_PICKS + k, SUBLANES:, :]
            for _ in issue:
                pass
            y = x_t[a3, t0 + j] + ((accs[0] + accs[1]) + (accs[2] + accs[3]))
            ms = jnp.sum(jnp.sum(y * y, axis=0, keepdims=True), axis=-1, keepdims=True) * (1.0 / D_MODEL)
            ys.append(y * lax.rsqrt(ms + EPS) * fg_ref[...])
            between()
        y_ref[pl.ds(t0, GROUP), :] = jnp.stack(ys, axis=0).reshape(GROUP, D_MODEL)

    def pair_body(gg, carry):
        g = 2 * gg
        last = gg + 1 == ngroups // 2

        @pl.when(jnp.logical_and(last, jnp.logical_and(has_next, i >= 1)))
        def _():
            publish_copy(1 - islot).wait()

        steps = _head_tile_steps(gg, st_ref, pair_ids, gw_v.at[w3], idx_tv, lane_tiles)

        def between():
            next(steps, None)
            next(steps, None)

        do_group(g, 0, islot * PEER_BLOCK + (g + 1) * GROUP, between)
        do_group(g + 1, 1, jnp.where(last, jnp.where(has_next, 1 - islot, islot) * PEER_BLOCK,
                                     islot * PEER_BLOCK + (g + 2) * GROUP), between)
        for _ in steps:
            pass
        return carry

    lax.fori_loop(0, ngroups // 2, pair_body, 0)

    @pl.when(has_next2)
    def _():
        idx_rows[...] = idx_tv[...].T
        publish_copy(islot).start()

    @pl.when(jnp.logical_not(has_next))
    def _():
        wait_group(0)


def _route_apply(x, g2, wq, keys_bd, fg, table):
    n = x.shape[0]
    nblk = n // PEER_BLOCK
    full = lambda a: pl.BlockSpec(a.shape, lambda i: (0,) * a.ndim)
    pair_ids = jnp.asarray(np.broadcast_to(_PAIR_IDS[:, None], (_PAIR_IDS.size, LANES)))
    blk = lambda f: pl.BlockSpec((PEER_BLOCK, D_MODEL), f)
    return pl.pallas_call(
        functools.partial(_fused_kernel, nblk=nblk),
        grid=(nblk,),
        in_specs=[blk(lambda i: (0, 0)), blk(lambda i: (1, 0)), blk(lambda i: (jnp.minimum(i + 2, nblk - 1), 0)),
                  full(g2), full(wq), full(keys_bd), full(pair_ids), full(fg), pl.BlockSpec(memory_space=pl.ANY)],
        out_specs=blk(lambda i: (i, 0)),
        out_shape=jax.ShapeDtypeStruct((n, D_MODEL), F32),
        scratch_shapes=[pltpu.VMEM((PEER_HEADS * 2 * PEER_NKEYS, PEER_BLOCK), F32),
                        pltpu.VMEM((3, PEER_BLOCK, SUBLANES, LANES), F32),
                        pltpu.VMEM((3, PEER_BLOCK, SUBLANES, LANES), F32),
                        pltpu.VMEM((3, PEER_PICKS, PEER_BLOCK), F32),
                        pltpu.VMEM((PEER_PICKS, PEER_BLOCK), jnp.int32),
                        pltpu.VMEM((PEER_BLOCK, PEER_PICKS), jnp.int32),
                        pltpu.SMEM((2 * PEER_BLOCK, PEER_PICKS), jnp.int32),
                        pltpu.VMEM((GATHER_SLOTS, GROUP * PEER_PICKS, EXPERT_ROWS, LANES), F32),
                        pltpu.SemaphoreType.DMA((GATHER_SLOTS,)),
                        pltpu.SemaphoreType.DMA((2,)),
                        pltpu.VMEM((PEER_PICKS, LANES), F32)],
        compiler_params=_params("arbitrary"),
        name="peer_route_apply",
    )(x, x, x, g2, wq, keys_bd, pair_ids, fg, table)


def _layer(x3, weights, att_kv, att_geom, ml_state, L):
    B, S, _ = x3.shape
    n = B * S
    x = x3.reshape(n, D_MODEL)
    (q, k, v, kf, vf, mq, mk, mv, mo, ga, gb, gates) = _inproj(
        x, weights["g1"], weights["w_main"], weights["b_main"], weights["wg_hi"], weights["wg_lo"],
        weights["bg"], weights["fb"])
    sh = lambda a: a.reshape(B, S, a.shape[-1])
    k_all, v_all = att_kv(sh(k), sh(v))
    att = _attention(sh(q), k_all, v_all, att_geom["bias"], L=att_geom["L"], KB=att_geom["KB"], pad=att_geom["pad"])
    h, c_new, n_new, m_new = _mlstm(sh(mq), sh(mk), sh(mv), sh(gates), *ml_state, L=L)
    x2 = _merge(x, att.reshape(n, ATT_W), h.reshape(n, ML_W), mo, ga, gb,
                weights["wa"], weights["wm"], weights["wo"])
    fg = weights["fg"].reshape(SUBLANES, LANES)
    if n >= 2 * PEER_BLOCK:
        y = _route_apply(x2, weights["g2"], weights["wq"], weights["keys_bd"], fg, weights["table"])
    else:
        xn2, x2_tiles, idx_t, gw_t = _route(x2, weights["g2"], weights["wq"], weights["keys_bd"])
        y = _apply(idx_t.T.reshape(n * PEER_PICKS), gw_t, xn2, x2_tiles, fg, weights["table"])
    return (y.reshape(B, S, D_MODEL), sh(kf).reshape(B, S, ATT_HEADS, ATT_HD), sh(vf).reshape(B, S, ATT_HEADS, ATT_HD),
            c_new, n_new, m_new[:, :, 0])


def kernel(x_prompt, x_sample, cache_att_k, cache_att_v, state_mlstm_C, state_mlstm_n, state_mlstm_m, norm1_g, w_in, b_in, ml_f_bias, att_rel_bias, w_att_branch, w_ml_branch, w_out, norm2_g, peer_wq, peer_sub_keys, peer_u, peer_v, final_g):
    depth = w_in.shape[0]
    assert depth == 1, "single-layer step"
    l = 0
    w = w_in[l]
    b = b_in[l]
    gate_lo, gate_hi = 7 * 512, 7 * 512 + 2 * ML_HEADS
    w_main = jnp.concatenate([w[:, :gate_lo], w[:, gate_hi:]], axis=1).astype(BF16)
    b_main = jnp.concatenate([b[:gate_lo], b[gate_hi:]])[None, :]
    wg = jnp.pad(w[:, gate_lo:gate_hi], ((0, 0), (0, GATE_COLS - 2 * ML_HEADS)))
    wg_hi = wg.astype(BF16)
    wg_lo = (wg - wg_hi.astype(F32)).astype(BF16)
    bg = jnp.pad(b[gate_lo:gate_hi], (0, GATE_COLS - 2 * ML_HEADS))[None, :]
    fb = jnp.pad(ml_f_bias[l], (ML_HEADS, GATE_COLS - 2 * ML_HEADS))[None, :]
    sk = peer_sub_keys[l].reshape(PEER_HEADS * 2, PEER_NKEYS, PEER_DKEY // 2)
    eye = jnp.eye(PEER_HEADS * 2, dtype=F32)
    keys_bd = (sk[:, :, None, :] * eye[:, None, :, None]).reshape(PEER_HEADS * 2 * PEER_NKEYS, D_MODEL).astype(BF16)
    weights = dict(
        g1=norm1_g[l][None, :], w_main=w_main, b_main=b_main, wg_hi=wg_hi, wg_lo=wg_lo, bg=bg, fb=fb,
        wa=w_att_branch[l].astype(BF16), wm=w_ml_branch[l].astype(BF16), wo=w_out[l].astype(BF16),
        g2=norm2_g[l][None, :], wq=peer_wq[l].astype(BF16), keys_bd=keys_bd, fg=final_g[None, :],
        table=jnp.concatenate([peer_u[l].reshape(-1, SUBLANES, LANES), peer_v[l].reshape(-1, SUBLANES, LANES)],
                              axis=1))

    Bs, T, _ = x_sample.shape
    P = cache_att_k.shape[2]
    assert P == ATT_CTX
    ck = cache_att_k[l].reshape(Bs, P, ATT_W).astype(BF16)
    cv = cache_att_v[l].reshape(Bs, P, ATT_W).astype(BF16)
    cat_kv = lambda k, v: (jnp.concatenate([ck, k], axis=1), jnp.concatenate([cv, v], axis=1))
    geom_s = dict(bias=_rel_bias(att_rel_bias[l], T, P + T), L=T, KB=P + T, pad=0)
    state_s = (state_mlstm_C[l], state_mlstm_n[l],
               jnp.broadcast_to(state_mlstm_m[l][:, :, None], (Bs, ML_HEADS, ML_DK)))
    y_s, k_s, v_s, c_s, n_s, m_s = _layer(x_sample, weights, cat_kv, geom_s, state_s, T)

    B, S, _ = x_prompt.shape
    pad_kv = lambda k, v: (jnp.pad(k, ((0, 0), (ATT_CTX, 0), (0, 0))), jnp.pad(v, ((0, 0), (ATT_CTX, 0), (0, 0))))
    geom_p = dict(bias=_pair_bias(_rel_bias(att_rel_bias[l], CHUNK, ATT_CTX + CHUNK)), L=2 * CHUNK,
                  KB=ATT_CTX + 2 * CHUNK, pad=ATT_CTX)
    state_p = (jnp.zeros((B, ML_HEADS, ML_DK, ML_DK), F32), jnp.zeros((B, ML_HEADS, ML_DK), F32),
               jnp.full((B, ML_HEADS, ML_DK), NEG, F32))
    y_p, k_p, v_p, c_p, n_p, m_p = _layer(x_prompt, weights, pad_kv, geom_p, state_p, CHUNK)
    keep = min(ATT_CTX, S)

    st = lambda a: a[None]
    return (y_p, y_s, st(k_p[:, -keep:]), st(v_p[:, -keep:]), st(c_p), st(n_p), st(m_p),
            st(k_s), st(v_s), st(c_s), st(n_s), st(m_s))
```

```python
import functools

import jax
import jax.numpy as jnp
import numpy as np
from jax import lax
from jax.experimental import pallas as pl
from jax.experimental.pallas import tpu as pltpu

F32 = jnp.float32
BF16 = jnp.bfloat16

D_MODEL = 1024
CHUNK = 64
ATT_HEADS = 8
ATT_HD = 64
ATT_W = ATT_HEADS * ATT_HD
ATT_CTX = 8 * CHUNK
MAX_REL = 128
ML_HEADS = 4
ML_DK = 128
ML_W = ML_HEADS * ML_DK
PEER_HEADS = 8
PEER_NKEYS = 128
PEER_DKEY = 128
PEER_TOPK = 16
PEER_PICKS = PEER_HEADS * PEER_TOPK
EPS = 1e-6
NEG = -1e30

LANES = 128
SUBLANES = 8
ROW_TILE = 256
GATE_COLS = LANES
ML_BATCH_ROWS = 2
GROUP = SUBLANES
PEER_BLOCK = 256
GATHER_SLOTS = 2
ISSUE_AHEAD = 4
EXPERT_ROWS = 2 * D_MODEL // LANES
VMEM_LIMIT = 56 * 1024 * 1024


def _params(*sem):
    return pltpu.CompilerParams(dimension_semantics=sem, vmem_limit_bytes=VMEM_LIMIT)


def _rms(x, g):
    return x * lax.rsqrt(jnp.mean(x * x, axis=-1, keepdims=True) + EPS) * g


def _inproj_kernel(x_ref, g_ref, w_ref, b_ref, wgh_ref, wgl_ref, bg_ref, fb_ref,
                   q_o, k_o, v_o, kf_o, vf_o, mq_o, mk_o, mv_o, mo_o, ga_o, gb_o, gt_o):
    xn = _rms(x_ref[...], g_ref[...])
    xh = xn.astype(BF16)

    def proj(off, width):
        return jnp.dot(xh, w_ref[:, off:off + width], preferred_element_type=F32) + b_ref[:, off:off + width]

    q_o[...] = (proj(0, 512) * (ATT_HD ** -0.5)).astype(BF16)
    k = proj(512, 512)
    kf_o[...] = k
    k_o[...] = k.astype(BF16)
    v = proj(1024, 512)
    vf_o[...] = v
    v_o[...] = v.astype(BF16)
    mq_o[...] = proj(1536, 512).astype(BF16)
    mk_o[...] = (proj(2048, 512) * (ML_DK ** -0.5)).astype(BF16)
    mv_o[...] = proj(2560, 512).astype(BF16)
    mo_o[...] = proj(3072, 512)
    ga_o[...] = proj(3584, D_MODEL)
    gb_o[...] = proj(3584 + D_MODEL, D_MODEL)
    xl = (xn - xh.astype(F32)).astype(BF16)
    zg = (jnp.dot(xh, wgh_ref[...], preferred_element_type=F32)
          + jnp.dot(xh, wgl_ref[...], preferred_element_type=F32)
          + jnp.dot(xl, wgh_ref[...], preferred_element_type=F32)) + bg_ref[...]
    t = zg + fb_ref[...]
    log_sig = jnp.minimum(t, 0.0) - jnp.log1p(jnp.exp(-jnp.abs(t)))
    lane = lax.broadcasted_iota(jnp.int32, zg.shape, 1)
    gt_o[...] = jnp.where(lane < ML_HEADS, zg, log_sig)


def _inproj(x, g1, w_main, b_main, wg_hi, wg_lo, bg, fb):
    n = x.shape[0]
    tm = min(ROW_TILE, n)
    row = lambda w: pl.BlockSpec((tm, w), lambda i: (i, 0))
    full = lambda a: pl.BlockSpec(a.shape, lambda i: (0, 0))
    widths = [(512, BF16)] * 3 + [(512, F32)] * 2 + [(512, BF16)] * 3 + [(512, F32), (D_MODEL, F32), (D_MODEL, F32),
                                                                     (GATE_COLS, F32)]
    return pl.pallas_call(
        _inproj_kernel,
        grid=(n // tm,),
        in_specs=[row(D_MODEL), full(g1), full(w_main), full(b_main), full(wg_hi), full(wg_lo), full(bg), full(fb)],
        out_specs=[row(w) for w, _ in widths],
        out_shape=[jax.ShapeDtypeStruct((n, w), dt) for w, dt in widths],
        compiler_params=_params("parallel"),
        name="inproj",
    )(x, g1, w_main, b_main, wg_hi, wg_lo, bg, fb)


def _attn_kernel(q_ref, k_ref, v_ref, bias_ref, o_ref, *, L, KB, pad):
    c = pl.program_id(1)
    start = pl.multiple_of(c * L, L)
    kb = k_ref[0, pl.ds(start, KB), :]
    vb = v_ref[0, pl.ds(start, KB), :]
    q = q_ref[0].astype(F32)
    col = lax.broadcasted_iota(jnp.int32, (1, KB), 1)
    valid = (start + col) >= pad
    lo = lax.broadcasted_iota(jnp.int32, (L, LANES), 1) < ATT_HD
    for hp in range(ATT_HEADS // 2):
        sl = slice(hp * LANES, (hp + 1) * LANES)
        qp, kp, vp = q[:, sl], kb[:, sl], vb[:, sl]
        qs = jnp.concatenate([jnp.where(lo, qp, 0.0), jnp.where(lo, 0.0, qp)], axis=0).astype(BF16)
        s = lax.dot_general(qs, kp, (((1,), (1,)), ((), ())), preferred_element_type=F32)
        s = jnp.where(valid, s + bias_ref[hp], NEG)
        e = jnp.exp(s - jnp.max(s, axis=-1, keepdims=True))
        pv = jnp.dot(e.astype(BF16), vp, preferred_element_type=F32) / jnp.sum(e, axis=-1, keepdims=True)
        o_ref[0, :, sl] = jnp.where(lo, pv[:L], pv[L:]).astype(BF16)


def _attention(q, k, v, bias, *, L, KB, pad):
    B, S, _ = q.shape
    ktot = k.shape[1]
    bias = bias.reshape(ATT_HEADS // 2, 2 * L, KB)
    return pl.pallas_call(
        functools.partial(_attn_kernel, L=L, KB=KB, pad=pad),
        grid=(B, S // L),
        in_specs=[pl.BlockSpec((1, L, ATT_W), lambda b, c: (b, c, 0)),
                  pl.BlockSpec((1, ktot, ATT_W), lambda b, c: (b, 0, 0)),
                  pl.BlockSpec((1, ktot, ATT_W), lambda b, c: (b, 0, 0)),
                  pl.BlockSpec(bias.shape, lambda b, c: (0, 0, 0))],
        out_specs=pl.BlockSpec((1, L, ATT_W), lambda b, c: (b, c, 0)),
        out_shape=jax.ShapeDtypeStruct((B, S, ATT_W), BF16),
        compiler_params=_params("parallel", "arbitrary"),
        name="attention",
    )(q, k, v, bias)


def _rel_bias(table, L, KB):
    m = np.arange(KB + L - 1)
    g = table[np.clip(L - 1 + ATT_CTX - m, -MAX_REL, MAX_REL) + MAX_REL].T
    return jnp.stack([g[:, L - 1 - l:L - 1 - l + KB] for l in range(L)], axis=1)


def _pair_bias(bias):
    h, l, _ = bias.shape
    neg = jnp.full((h, l, l), NEG, bias.dtype)
    return jnp.concatenate([jnp.concatenate([bias, neg], axis=2), jnp.concatenate([neg, bias], axis=2)], axis=1)


def _t128(x):
    rows = x.shape[0]
    if rows < LANES:
        x = jnp.concatenate([x, jnp.zeros((LANES - rows, LANES), x.dtype)], axis=0)
    return x.T[:, :rows]


def _mlstm_kernel(q_ref, k_ref, v_ref, g_ref, c0_ref, n0_ref, m0_ref,
                  h_ref, c_ref, n_ref, m_ref, cs, ns, ms, *, L):
    c = pl.program_id(1)

    @pl.when(c == 0)
    def _():
        cs[...] = c0_ref[...]
        ns[...] = n0_ref[...]
        ms[...] = m0_ref[...]

    tril = lax.broadcasted_iota(jnp.int32, (L, L), 0) >= lax.broadcasted_iota(jnp.int32, (L, L), 1)
    for bb in range(q_ref.shape[0]):
        gates = g_ref[bb]
        row = lax.broadcasted_iota(jnp.int32, gates.shape, 0)
        csum = gates
        sh = 1
        while sh < L:
            csum = csum + jnp.where(row >= sh, pltpu.roll(csum, sh, axis=0), 0.0)
            sh *= 2
        gates_t = _t128(gates)
        csum_t = _t128(csum)
        for h in range(ML_HEADS):
            sl = slice(h * ML_DK, (h + 1) * ML_DK)
            q, k, v = q_ref[bb, :, sl], k_ref[bb, :, sl], v_ref[bb, :, sl]
            b_col = csum[:, ML_HEADS + h:ML_HEADS + h + 1]
            b_row = csum_t[ML_HEADS + h:ML_HEADS + h + 1, :]
            ig_col = gates[:, h:h + 1]
            ig_row = gates_t[h:h + 1, :]
            m_prev = ms[bb, h:h + 1, 0:1]
            a_col = b_col + m_prev
            dmat = jnp.where(tril, b_col - b_row + ig_row, NEG)
            m_t = jnp.maximum(a_col, jnp.max(dmat, axis=-1, keepdims=True))
            w_inter = jnp.exp(a_col - m_t)
            wmat = jnp.exp(dmat - m_t)
            qk = lax.dot_general(q, k, (((1,), (1,)), ((), ())), preferred_element_type=F32)
            wqk = wmat * qk
            c_old = cs[bb, h]
            n_old = ns[bb, h:h + 1, :]
            num = (w_inter * jnp.dot(q, c_old.astype(BF16), preferred_element_type=F32)
                   + jnp.dot(wqk.astype(BF16), v, preferred_element_type=F32))
            den = (w_inter * jnp.sum(q.astype(F32) * n_old, axis=-1, keepdims=True)
                   + jnp.sum(wqk, axis=-1, keepdims=True))
            h_ref[bb, :, sl] = num / jnp.maximum(jnp.abs(den), jnp.exp(-m_t))
            m_new = m_t[L - 1:L, :]
            decay = jnp.exp(a_col[L - 1:L, :] - m_new)
            wl_col = jnp.exp(b_col[L - 1:L, :] - b_col + ig_col - m_new)
            kw = k.astype(F32) * wl_col
            cs[bb, h] = decay * c_old + lax.dot_general(kw.astype(BF16), v, (((0,), (0,)), ((), ())),
                                                        preferred_element_type=F32)
            ns[bb, h:h + 1, :] = decay * n_old + jnp.sum(kw, axis=0, keepdims=True)
            ms[bb, h:h + 1, :] = jnp.broadcast_to(m_new, (1, ML_DK))

    @pl.when(c == pl.num_programs(1) - 1)
    def _():
        c_ref[...] = cs[...]
        n_ref[...] = ns[...]
        m_ref[...] = ms[...]


def _mlstm(q, k, v, gates, c0, n0, m0, *, L):
    B, S, _ = q.shape
    bb = ML_BATCH_ROWS
    seq = lambda w: pl.BlockSpec((bb, L, w), lambda b, c: (b, c, 0))
    st4 = pl.BlockSpec((bb, ML_HEADS, ML_DK, ML_DK), lambda b, c: (b, 0, 0, 0))
    st3 = pl.BlockSpec((bb, ML_HEADS, ML_DK), lambda b, c: (b, 0, 0))
    return pl.pallas_call(
        functools.partial(_mlstm_kernel, L=L),
        grid=(B // bb, S // L),
        in_specs=[seq(ML_W), seq(ML_W), seq(ML_W), seq(GATE_COLS), st4, st3, st3],
        out_specs=[seq(ML_W), st4, st3, st3],
        out_shape=[jax.ShapeDtypeStruct((B, S, ML_W), F32),
                   jax.ShapeDtypeStruct((B, ML_HEADS, ML_DK, ML_DK), F32),
                   jax.ShapeDtypeStruct((B, ML_HEADS, ML_DK), F32),
                   jax.ShapeDtypeStruct((B, ML_HEADS, ML_DK), F32)],
        scratch_shapes=[pltpu.VMEM((bb, ML_HEADS, ML_DK, ML_DK), F32),
                        pltpu.VMEM((bb, ML_HEADS, ML_DK), F32),
                        pltpu.VMEM((bb, ML_HEADS, ML_DK), F32)],
        compiler_params=_params("parallel", "arbitrary"),
        name="mlstm",
    )(q, k, v, gates, c0, n0, m0)


def _merge_kernel(x_ref, att_ref, h_ref, mo_ref, ga_ref, gb_ref, wa_ref, wm_ref, wo_ref, o_ref):
    ml = (jax.nn.sigmoid(mo_ref[...]) * h_ref[...]).astype(BF16)
    a = jnp.dot(att_ref[...], wa_ref[...], preferred_element_type=F32)
    b = jnp.dot(ml, wm_ref[...], preferred_element_type=F32)
    merged = jax.nn.sigmoid(ga_ref[...]) * a + jax.nn.sigmoid(gb_ref[...]) * b
    o_ref[...] = x_ref[...] + jnp.dot(merged.astype(BF16), wo_ref[...], preferred_element_type=F32)


def _merge(x, att, h, mo, ga, gb, wa, wm, wo):
    n = x.shape[0]
    tm = min(ROW_TILE, n)
    row = lambda w: pl.BlockSpec((tm, w), lambda i: (i, 0))
    full = lambda a: pl.BlockSpec(a.shape, lambda i: (0, 0))
    return pl.pallas_call(
        _merge_kernel,
        grid=(n // tm,),
        in_specs=[row(D_MODEL), row(ATT_W), row(ML_W), row(ML_W), row(D_MODEL), row(D_MODEL),
                  full(wa), full(wm), full(wo)],
        out_specs=row(D_MODEL),
        out_shape=jax.ShapeDtypeStruct((n, D_MODEL), F32),
        compiler_params=_params("parallel"),
        name="merge",
    )(x, att, h, mo, ga, gb, wa, wm, wo)


def _top16(s, ids, big):
    vals, win = [], []
    for _ in range(PEER_TOPK):
        m = jnp.max(s, axis=0, keepdims=True)
        i = jnp.min(jnp.where(s == m, ids, big), axis=0, keepdims=True)
        vals.append(m)
        win.append(i)
        s = jnp.where(ids == i, -jnp.inf, s)
    return jnp.concatenate(vals, axis=0), jnp.concatenate(win, axis=0)


_PAIR_IDS = np.array([b for b in range(16)] + [a * 16 + b for a in range(1, 8) for b in range(8)]
                     + [a * 16 for a in range(8, 16)], np.int32)


def _select(rank, table):
    out = jnp.zeros_like(table)
    for a in range(PEER_TOPK):
        out = jnp.where(rank == a, table[a:a + 1], out)
    return out


def _route_kernel(x_ref, g_ref, wq_ref, keys_ref, pair_ref, xn_o, xt_o, idx_o, gw_o, st_ref):
    x = x_ref[...]
    xn = _rms(x, g_ref[...])
    xn_o[...] = xn.reshape(xn_o.shape)
    xt_o[...] = x.reshape(xt_o.shape)
    q = jnp.dot(xn.astype(BF16), wq_ref[...], preferred_element_type=F32)
    st_ref[...] = lax.dot_general(keys_ref[...], q.astype(BF16), (((1,), (1,)), ((), ())),
                                  preferred_element_type=F32)
    lane_tiles = st_ref.shape[1] // LANES
    key_ids = lax.broadcasted_iota(jnp.int32, (PEER_NKEYS, LANES), 0)

    def head_tile(it, carry):
        h = it // lane_tiles
        lanes = pl.ds(pl.multiple_of((it % lane_tiles) * LANES, LANES), LANES)
        r0 = pl.multiple_of(h * 2 * PEER_NKEYS, 2 * PEER_NKEYS)
        sv0, si0 = _top16(st_ref[pl.ds(r0, PEER_NKEYS), lanes], key_ids, PEER_NKEYS)
        sv1, si1 = _top16(st_ref[pl.ds(r0 + PEER_NKEYS, PEER_NKEYS), lanes], key_ids, PEER_NKEYS)
        half = PEER_TOPK // 2
        cand = jnp.concatenate(
            [jnp.broadcast_to(sv0[0:1], (PEER_TOPK, LANES)) + sv1]
            + [jnp.broadcast_to(sv0[a:a + 1], (half, LANES)) + sv1[0:half] for a in range(1, half)]
            + [sv0[half:] + jnp.broadcast_to(sv1[0:1], (half, LANES))], axis=0)
        fv, fi = _top16(cand, pair_ref[...], PEER_TOPK * PEER_TOPK)
        eidx = _select(fi >> 4, si0) * PEER_NKEYS + _select(fi & (PEER_TOPK - 1), si1)
        e = jnp.exp(fv - fv[0:1])
        o0 = pl.multiple_of(h * PEER_TOPK, PEER_TOPK)
        gw_o[pl.ds(o0, PEER_TOPK), lanes] = e / jnp.sum(e, axis=0, keepdims=True)
        idx_o[pl.ds(o0, PEER_TOPK), lanes] = eidx
        return carry

    lax.fori_loop(0, PEER_HEADS * lane_tiles, head_tile, 0)


def _route(x, g2, wq, keys_bd):
    n = x.shape[0]
    tm = min(ROW_TILE, n)
    full = lambda a: pl.BlockSpec(a.shape, lambda i: (0, 0))
    pair_ids = jnp.asarray(np.broadcast_to(_PAIR_IDS[:, None], (_PAIR_IDS.size, LANES)))
    return pl.pallas_call(
        _route_kernel,
        grid=(n // tm,),
        scratch_shapes=[pltpu.VMEM((PEER_HEADS * 2 * PEER_NKEYS, tm), F32)],
        in_specs=[pl.BlockSpec((tm, D_MODEL), lambda i: (i, 0)), full(g2), full(wq), full(keys_bd), full(pair_ids)],
        out_specs=[pl.BlockSpec((tm, SUBLANES, LANES), lambda i: (i, 0, 0)),
                   pl.BlockSpec((tm, SUBLANES, LANES), lambda i: (i, 0, 0)),
                   pl.BlockSpec((PEER_PICKS, tm), lambda i: (0, i)),
                   pl.BlockSpec((PEER_PICKS, tm), lambda i: (0, i))],
        out_shape=[jax.ShapeDtypeStruct((n, SUBLANES, LANES), F32),
                   jax.ShapeDtypeStruct((n, SUBLANES, LANES), F32),
                   jax.ShapeDtypeStruct((PEER_PICKS, n), jnp.int32),
                   jax.ShapeDtypeStruct((PEER_PICKS, n), F32)],
        compiler_params=_params("parallel"),
        name="peer_route",
    )(x, g2, wq, keys_bd, pair_ids)


def _apply_kernel(idx_hbm, gwt_ref, xn_ref, x_ref, fg_ref, tab_hbm, y_ref, idx_s, buf, gsem, isem, coef_ref, *, nblk):
    i = pl.program_id(0)
    ngroups = PEER_BLOCK // GROUP
    rows = GROUP * PEER_PICKS
    blk_words = PEER_BLOCK * PEER_PICKS
    assert ngroups % GATHER_SLOTS == 0 and GATHER_SLOTS == 2

    def idx_copy(blk, islot):
        return pltpu.make_async_copy(idx_hbm.at[pl.ds(blk * blk_words, blk_words)],
                                     idx_s.at[pl.ds(islot * blk_words, blk_words)], isem.at[islot])

    def row_copy(e, slot, r):
        return pltpu.make_async_copy(tab_hbm.at[e], buf.at[slot, r], gsem.at[slot])

    def wait_group(slot):
        pltpu.make_async_copy(tab_hbm.at[pl.ds(0, rows)], buf.at[slot], gsem.at[slot]).wait()

    islot = i % 2
    has_next = i + 1 < nblk

    @pl.when(i == 0)
    def _():
        idx_copy(0, 0).start()
        idx_copy(0, 0).wait()

        def body(r, carry):
            row_copy(idx_s[r], 0, r).start()
            return carry

        lax.fori_loop(0, rows, body, 0, unroll=8)

    @pl.when(has_next)
    def _():
        idx_copy(i + 1, 1 - islot).start()

    lane = lax.broadcasted_iota(jnp.int32, (PEER_PICKS, LANES), 1)
    sub = lax.broadcasted_iota(jnp.int32, (SUBLANES, LANES), 0)
    masks = {1: (sub & 1) == 0, 2: (sub & 2) == 0, 4: (sub & 4) == 0}

    def fold(a, b, h):
        return jnp.where(masks[h], a, b) + pltpu.roll(jnp.where(masks[h], b, a), h, axis=0)

    def row_sums(ps):
        c = [fold(ps[2 * m], ps[2 * m + 1], 1) for m in range(4)]
        d = [fold(c[0], c[1], 2), fold(c[2], c[3], 2)]
        return fold(d[0], d[1], 4)

    def do_group(g, slot, next_word0):
        nslot = 1 - slot
        wait_group(slot)
        t0 = pl.multiple_of(g * GROUP, GROUP)
        lane0 = pl.multiple_of((t0 // LANES) * LANES, LANES)
        gw_tile = gwt_ref[:, pl.ds(lane0, LANES)]
        ys = []
        for j in range(GROUP):
            for k in range(PEER_PICKS):
                r = j * PEER_PICKS + k
                row_copy(idx_s[next_word0 + r], nslot, r).start(priority=k % 2)
            x = xn_ref[t0 + j]
            sums = []
            for c in range(PEER_PICKS // SUBLANES):
                r0 = j * PEER_PICKS + c * SUBLANES
                sums.append(row_sums([buf[slot, r0 + k, 0:SUBLANES, :] * x for k in range(SUBLANES)]))
            act = jnp.sum(jnp.concatenate(sums, axis=0), axis=-1, keepdims=True)
            gelu = 0.5 * act * (1.0 + lax.erf(act * (2.0 ** -0.5)))
            gw_col = jnp.sum(jnp.where(lane == t0 - lane0 + j, gw_tile, 0.0), axis=-1, keepdims=True)
            coef_ref[...] = jnp.broadcast_to(gw_col * gelu, (PEER_PICKS, LANES))
            accs = [jnp.zeros((SUBLANES, LANES), F32) for _ in range(4)]
            for k in range(PEER_PICKS):
                accs[k % 4] = accs[k % 4] + coef_ref[k:k + 1, :] * buf[slot, j * PEER_PICKS + k, SUBLANES:, :]
            y = x_ref[t0 + j] + ((accs[0] + accs[1]) + (accs[2] + accs[3]))
            ms = jnp.sum(jnp.sum(y * y, axis=0, keepdims=True), axis=-1, keepdims=True) * (1.0 / D_MODEL)
            ys.append(y * lax.rsqrt(ms + EPS) * fg_ref[...])
        y_ref[pl.ds(t0, GROUP), :] = jnp.stack(ys, axis=0).reshape(GROUP, D_MODEL)

    def pair_body(gg, carry):
        g = 2 * gg
        do_group(g, 0, islot * blk_words + (g + 1) * rows)
        last = gg + 1 == ngroups // 2

        @pl.when(jnp.logical_and(last, has_next))
        def _():
            idx_copy(i + 1, 1 - islot).wait()

        do_group(g + 1, 1, jnp.where(last, jnp.where(has_next, 1 - islot, islot) * blk_words,
                                     islot * blk_words + (g + 2) * rows))
        return carry

    lax.fori_loop(0, ngroups // 2, pair_body, 0)

    @pl.when(jnp.logical_not(has_next))
    def _():
        wait_group(0)


def _apply(idx, gwt, xn, x, fg, table):
    n = x.shape[0]
    nblk = n // PEER_BLOCK
    tile = pl.BlockSpec((PEER_BLOCK, SUBLANES, LANES), lambda i: (i, 0, 0))
    return pl.pallas_call(
        functools.partial(_apply_kernel, nblk=nblk),
        grid=(nblk,),
        in_specs=[pl.BlockSpec(memory_space=pl.ANY), pl.BlockSpec((PEER_PICKS, PEER_BLOCK), lambda i: (0, i)),
                  tile, tile,
                  pl.BlockSpec(fg.shape, lambda i: (0, 0)), pl.BlockSpec(memory_space=pl.ANY)],
        out_specs=pl.BlockSpec((PEER_BLOCK, D_MODEL), lambda i: (i, 0)),
        out_shape=jax.ShapeDtypeStruct((n, D_MODEL), F32),
        scratch_shapes=[pltpu.SMEM((2 * PEER_BLOCK * PEER_PICKS,), jnp.int32),
                        pltpu.VMEM((GATHER_SLOTS, GROUP * PEER_PICKS, EXPERT_ROWS, LANES), F32),
                        pltpu.SemaphoreType.DMA((GATHER_SLOTS,)),
                        pltpu.SemaphoreType.DMA((2,)),
                        pltpu.VMEM((PEER_PICKS, LANES), F32)],
        compiler_params=_params("arbitrary"),
        name="peer_apply",
    )(idx, gwt, xn, x, fg, table)


def _head_tile_steps(it, st_ref, pair_ids, gw_dst, idx_dst, lane_tiles):
    h = it // lane_tiles
    lanes = pl.ds(pl.multiple_of((it % lane_tiles) * LANES, LANES), LANES)
    r0 = pl.multiple_of(h * 2 * PEER_NKEYS, 2 * PEER_NKEYS)
    key_ids = lax.broadcasted_iota(jnp.int32, (PEER_NKEYS, LANES), 0)
    s = [st_ref[pl.ds(r0, PEER_NKEYS), lanes], st_ref[pl.ds(r0 + PEER_NKEYS, PEER_NKEYS), lanes]]
    vals, wins = ([], []), ([], [])

    def take_max(x, ids, big):
        m = jnp.max(x, axis=0, keepdims=True)
        w = jnp.min(jnp.where(x == m, ids, big), axis=0, keepdims=True)
        return m, w, jnp.where(ids == w, -jnp.inf, x)

    for _ in range(PEER_TOPK):
        for p in range(2):
            m, w, s[p] = take_max(s[p], key_ids, PEER_NKEYS)
            vals[p].append(m)
            wins[p].append(w)
        yield
    sv0, sv1 = jnp.concatenate(vals[0], axis=0), jnp.concatenate(vals[1], axis=0)
    si0, si1 = jnp.concatenate(wins[0], axis=0), jnp.concatenate(wins[1], axis=0)
    half = PEER_TOPK // 2
    cand = jnp.concatenate(
        [jnp.broadcast_to(sv0[0:1], (PEER_TOPK, LANES)) + sv1]
        + [jnp.broadcast_to(sv0[a:a + 1], (half, LANES)) + sv1[0:half] for a in range(1, half)]
        + [sv0[half:] + jnp.broadcast_to(sv1[0:1], (half, LANES))], axis=0)
    fvals, fwins = [], []
    for r in range(PEER_TOPK):
        m, w, cand = take_max(cand, pair_ids, PEER_TOPK * PEER_TOPK)
        fvals.append(m)
        fwins.append(w)
        if r % 2:
            yield
    fv, fi = jnp.concatenate(fvals, axis=0), jnp.concatenate(fwins, axis=0)
    eidx = _select(fi >> 4, si0) * PEER_NKEYS + _select(fi & (PEER_TOPK - 1), si1)
    e = jnp.exp(fv - fv[0:1])
    o0 = pl.multiple_of(h * PEER_TOPK, PEER_TOPK)
    gw_dst[pl.ds(o0, PEER_TOPK), lanes] = e / jnp.sum(e, axis=0, keepdims=True)
    idx_dst[pl.ds(o0, PEER_TOPK), lanes] = eidx
    yield


def _fused_kernel(x0_ref, x1_ref, xnext_ref, g_ref, wq_ref, keys_ref, pair_ref, fg_ref, tab_hbm, y_ref,
                  st_ref, xn_t, x_t, gw_v, idx_tv, idx_rows, idx_s, buf, gsem, isem, coef_ref, *, nblk):
    i = pl.program_id(0)
    ngroups = PEER_BLOCK // GROUP
    rows = GROUP * PEER_PICKS
    lane_tiles = PEER_BLOCK // LANES
    head_tiles = PEER_HEADS * lane_tiles
    assert ngroups // 2 == head_tiles and nblk >= 2
    islot = i % 2
    a3 = i % 3
    w3 = (i + 2) % 3
    has_next = i + 1 < nblk
    has_next2 = i + 2 < nblk
    pair_ids = pair_ref[...]

    def scores(x, slot):
        xn = _rms(x, g_ref[...])
        xn_t[slot] = xn.reshape(xn_t.shape[1:])
        x_t[slot] = x.reshape(x_t.shape[1:])
        q = jnp.dot(xn.astype(BF16), wq_ref[...], preferred_element_type=F32)
        st_ref[...] = lax.dot_general(keys_ref[...], q.astype(BF16), (((1,), (1,)), ((), ())),
                                      preferred_element_type=F32)

    def publish_copy(smem_slot):
        return pltpu.make_async_copy(idx_rows, idx_s.at[pl.ds(smem_slot * PEER_BLOCK, PEER_BLOCK), :],
                                     isem.at[smem_slot])

    def row_copy(e, slot, r):
        return pltpu.make_async_copy(tab_hbm.at[e], buf.at[slot, r], gsem.at[slot])

    def wait_group(slot):
        pltpu.make_async_copy(tab_hbm.at[pl.ds(0, rows)], buf.at[slot], gsem.at[slot]).wait()

    @pl.when(i == 0)
    def _():
        for blk, x_ref in ((0, x0_ref), (1, x1_ref)):
            scores(x_ref[...], blk)

            def body(it, carry):
                for _ in _head_tile_steps(it, st_ref, pair_ids, gw_v.at[blk], idx_tv, lane_tiles):
                    pass
                return carry

            lax.fori_loop(0, head_tiles, body, 0)
            idx_rows[...] = idx_tv[...].T
            publish_copy(blk).start()
            publish_copy(blk).wait()

        def issue(r, carry):
            row_copy(idx_s[r // PEER_PICKS, r % PEER_PICKS], 0, r).start()
            return carry

        lax.fori_loop(0, rows, issue, 0, unroll=8)

    @pl.when(has_next2)
    def _():
        scores(xnext_ref[...], w3)

    lane = lax.broadcasted_iota(jnp.int32, (PEER_PICKS, LANES), 1)
    sub = lax.broadcasted_iota(jnp.int32, (SUBLANES, LANES), 0)
    masks = {1: (sub & 1) == 0, 2: (sub & 2) == 0, 4: (sub & 4) == 0}

    def fold(a, b, h):
        return jnp.where(masks[h], a, b) + pltpu.roll(jnp.where(masks[h], b, a), h, axis=0)

    def row_sums(ps):
        c = [fold(ps[2 * m], ps[2 * m + 1], 1) for m in range(4)]
        d = [fold(c[0], c[1], 2), fold(c[2], c[3], 2)]
        return fold(d[0], d[1], 4)

    def do_group(g, slot, next_row0, between):
        nslot = 1 - slot

        def issue_steps(j):
            for k in range(PEER_PICKS):
                row_copy(idx_s[next_row0 + j, k], nslot, j * PEER_PICKS + k).start(priority=k % 2)
                if k % 4 == 3:
                    yield

        for j in range(ISSUE_AHEAD):
            for _ in issue_steps(j):
                pass
        wait_group(slot)
        t0 = pl.multiple_of(g * GROUP, GROUP)
        lane0 = pl.multiple_of((t0 // LANES) * LANES, LANES)
        gw_tile = gw_v[a3, :, pl.ds(lane0, LANES)]
        ys = []
        for j in range(GROUP):
            issue = issue_steps(j + ISSUE_AHEAD) if j + ISSUE_AHEAD < GROUP else iter(())
            x = xn_t[a3, t0 + j]
            sums = []
            for c in range(PEER_PICKS // SUBLANES):
                next(issue, None)
                r0 = j * PEER_PICKS + c * SUBLANES
                sums.append(row_sums([buf[slot, r0 + k, 0:SUBLANES, :] * x for k in range(SUBLANES)]))
            act = jnp.sum(jnp.concatenate(sums, axis=0), axis=-1, keepdims=True)
            gelu = 0.5 * act * (1.0 + lax.erf(act * (2.0 ** -0.5)))
            gw_col = jnp.sum(jnp.where(lane == t0 - lane0 + j, gw_tile, 0.0), axis=-1, keepdims=True)
            coef_ref[...] = jnp.broadcast_to(gw_col * gelu, (PEER_PICKS, LANES))
            accs = [jnp.zeros((SUBLANES, LANES), F32) for _ in range(4)]
            for k in range(PEER_PICKS):
                if k % SUBLANES == 0:
                    next(issue, None)
                accs[k % 4] = accs[k % 4] + coef_ref[k:k + 1, :] * buf[slot, j * PEER_PICKS + k, SUBLANES:, :]
            for _ in issue:
                pass
            y = x_t[a3, t0 + j] + ((accs[0] + accs[1]) + (accs[2] + accs[3]))
            ms = jnp.sum(jnp.sum(y * y, axis=0, keepdims=True), axis=-1, keepdims=True) * (1.0 / D_MODEL)
            ys.append(y * lax.rsqrt(ms + EPS) * fg_ref[...])
            between()
        y_ref[pl.ds(t0, GROUP), :] = jnp.stack(ys, axis=0).reshape(GROUP, D_MODEL)

    def pair_body(gg, carry):
        g = 2 * gg
        last = gg + 1 == ngroups // 2

        @pl.when(jnp.logical_and(last, jnp.logical_and(has_next, i >= 1)))
        def _():
            publish_copy(1 - islot).wait()

        steps = _head_tile_steps(gg, st_ref, pair_ids, gw_v.at[w3], idx_tv, lane_tiles)

        def between():
            next(steps, None)
            next(steps, None)

        do_group(g, 0, islot * PEER_BLOCK + (g + 1) * GROUP, between)
        do_group(g + 1, 1, jnp.where(last, jnp.where(has_next, 1 - islot, islot) * PEER_BLOCK,
                                     islot * PEER_BLOCK + (g + 2) * GROUP), between)
        for _ in steps:
            pass
        return carry

    lax.fori_loop(0, ngroups // 2, pair_body, 0)

    @pl.when(has_next2)
    def _():
        idx_rows[...] = idx_tv[...].T
        publish_copy(islot).start()

    @pl.when(jnp.logical_not(has_next))
    def _():
        wait_group(0)


def _route_apply(x, g2, wq, keys_bd, fg, table):
    n = x.shape[0]
    nblk = n // PEER_BLOCK
    full = lambda a: pl.BlockSpec(a.shape, lambda i: (0,) * a.ndim)
    pair_ids = jnp.asarray(np.broadcast_to(_PAIR_IDS[:, None], (_PAIR_IDS.size, LANES)))
    blk = lambda f: pl.BlockSpec((PEER_BLOCK, D_MODEL), f)
    return pl.pallas_call(
        functools.partial(_fused_kernel, nblk=nblk),
        grid=(nblk,),
        in_specs=[blk(lambda i: (0, 0)), blk(lambda i: (1, 0)), blk(lambda i: (jnp.minimum(i + 2, nblk - 1), 0)),
                  full(g2), full(wq), full(keys_bd), full(pair_ids), full(fg), pl.BlockSpec(memory_space=pl.ANY)],
        out_specs=blk(lambda i: (i, 0)),
        out_shape=jax.ShapeDtypeStruct((n, D_MODEL), F32),
        scratch_shapes=[pltpu.VMEM((PEER_HEADS * 2 * PEER_NKEYS, PEER_BLOCK), F32),
                        pltpu.VMEM((3, PEER_BLOCK, SUBLANES, LANES), F32),
                        pltpu.VMEM((3, PEER_BLOCK, SUBLANES, LANES), F32),
                        pltpu.VMEM((3, PEER_PICKS, PEER_BLOCK), F32),
                        pltpu.VMEM((PEER_PICKS, PEER_BLOCK), jnp.int32),
                        pltpu.VMEM((PEER_BLOCK, PEER_PICKS), jnp.int32),
                        pltpu.SMEM((2 * PEER_BLOCK, PEER_PICKS), jnp.int32),
                        pltpu.VMEM((GATHER_SLOTS, GROUP * PEER_PICKS, EXPERT_ROWS, LANES), F32),
                        pltpu.SemaphoreType.DMA((GATHER_SLOTS,)),
                        pltpu.SemaphoreType.DMA((2,)),
                        pltpu.VMEM((PEER_PICKS, LANES), F32)],
        compiler_params=_params("arbitrary"),
        name="peer_route_apply",
    )(x, x, x, g2, wq, keys_bd, pair_ids, fg, table)


def _layer(x3, weights, att_kv, att_geom, ml_state, L):
    B, S, _ = x3.shape
    n = B * S
    x = x3.reshape(n, D_MODEL)
    (q, k, v, kf, vf, mq, mk, mv, mo, ga, gb, gates) = _inproj(
        x, weights["g1"], weights["w_main"], weights["b_main"], weights["wg_hi"], weights["wg_lo"],
        weights["bg"], weights["fb"])
    sh = lambda a: a.reshape(B, S, a.shape[-1])
    k_all, v_all = att_kv(sh(k), sh(v))
    att = _attention(sh(q), k_all, v_all, att_geom["bias"], L=att_geom["L"], KB=att_geom["KB"], pad=att_geom["pad"])
    h, c_new, n_new, m_new = _mlstm(sh(mq), sh(mk), sh(mv), sh(gates), *ml_state, L=L)
    x2 = _merge(x, att.reshape(n, ATT_W), h.reshape(n, ML_W), mo, ga, gb,
                weights["wa"], weights["wm"], weights["wo"])
    fg = weights["fg"].reshape(SUBLANES, LANES)
    if n >= 2 * PEER_BLOCK:
        y = _route_apply(x2, weights["g2"], weights["wq"], weights["keys_bd"], fg, weights["table"])
    else:
        xn2, x2_tiles, idx_t, gw_t = _route(x2, weights["g2"], weights["wq"], weights["keys_bd"])
        y = _apply(idx_t.T.reshape(n * PEER_PICKS), gw_t, xn2, x2_tiles, fg, weights["table"])
    return (y.reshape(B, S, D_MODEL), sh(kf).reshape(B, S, ATT_HEADS, ATT_HD), sh(vf).reshape(B, S, ATT_HEADS, ATT_HD),
            c_new, n_new, m_new[:, :, 0])


def kernel(x_prompt, x_sample, cache_att_k, cache_att_v, state_mlstm_C, state_mlstm_n, state_mlstm_m, norm1_g, w_in, b_in, ml_f_bias, att_rel_bias, w_att_branch, w_ml_branch, w_out, norm2_g, peer_wq, peer_sub_keys, peer_u, peer_v, final_g):
    depth = w_in.shape[0]
    assert depth == 1, "single-layer step"
    l = 0
    w = w_in[l]
    b = b_in[l]
    gate_lo, gate_hi = 7 * 512, 7 * 512 + 2 * ML_HEADS
    w_main = jnp.concatenate([w[:, :gate_lo], w[:, gate_hi:]], axis=1).astype(BF16)
    b_main = jnp.concatenate([b[:gate_lo], b[gate_hi:]])[None, :]
    wg = jnp.pad(w[:, gate_lo:gate_hi], ((0, 0), (0, GATE_COLS - 2 * ML_HEADS)))
    wg_hi = wg.astype(BF16)
    wg_lo = (wg - wg_hi.astype(F32)).astype(BF16)
    bg = jnp.pad(b[gate_lo:gate_hi], (0, GATE_COLS - 2 * ML_HEADS))[None, :]
    fb = jnp.pad(ml_f_bias[l], (ML_HEADS, GATE_COLS - 2 * ML_HEADS))[None, :]
    sk = peer_sub_keys[l].reshape(PEER_HEADS * 2, PEER_NKEYS, PEER_DKEY // 2)
    eye = jnp.eye(PEER_HEADS * 2, dtype=F32)
    keys_bd = (sk[:, :, None, :] * eye[:, None, :, None]).reshape(PEER_HEADS * 2 * PEER_NKEYS, D_MODEL).astype(BF16)
    weights = dict(
        g1=norm1_g[l][None, :], w_main=w_main, b_main=b_main, wg_hi=wg_hi, wg_lo=wg_lo, bg=bg, fb=fb,
        wa=w_att_branch[l].astype(BF16), wm=w_ml_branch[l].astype(BF16), wo=w_out[l].astype(BF16),
        g2=norm2_g[l][None, :], wq=peer_wq[l].astype(BF16), keys_bd=keys_bd, fg=final_g[None, :],
        table=jnp.concatenate([peer_u[l].reshape(-1, SUBLANES, LANES), peer_v[l].reshape(-1, SUBLANES, LANES)],
                              axis=1))

    Bs, T, _ = x_sample.shape
    P = cache_att_k.shape[2]
    assert P == ATT_CTX
    ck = cache_att_k[l].reshape(Bs, P, ATT_W).astype(BF16)
    cv = cache_att_v[l].reshape(Bs, P, ATT_W).astype(BF16)
    cat_kv = lambda k, v: (jnp.concatenate([ck, k], axis=1), jnp.concatenate([cv, v], axis=1))
    geom_s = dict(bias=_rel_bias(att_rel_bias[l], T, P + T), L=T, KB=P + T, pad=0)
    state_s = (state_mlstm_C[l], state_mlstm_n[l],
               jnp.broadcast_to(state_mlstm_m[l][:, :, None], (Bs, ML_HEADS, ML_DK)))
    y_s, k_s, v_s, c_s, n_s, m_s = _layer(x_sample, weights, cat_kv, geom_s, state_s, T)

    B, S, _ = x_prompt.shape
    pad_kv = lambda k, v: (jnp.pad(k, ((0, 0), (ATT_CTX, 0), (0, 0))), jnp.pad(v, ((0, 0), (ATT_CTX, 0), (0, 0))))
    geom_p = dict(bias=_pair_bias(_rel_bias(att_rel_bias[l], CHUNK, ATT_CTX + CHUNK)), L=2 * CHUNK,
                  KB=ATT_CTX + 2 * CHUNK, pad=ATT_CTX)
    state_p = (jnp.zeros((B, ML_HEADS, ML_DK, ML_DK), F32), jnp.zeros((B, ML_HEADS, ML_DK), F32),
               jnp.full((B, ML_HEADS, ML_DK), NEG, F32))
    y_p, k_p, v_p, c_p, n_p, m_p = _layer(x_prompt, weights, pad_kv, geom_p, state_p, CHUNK)
    keep = min(ATT_CTX, S)

    st = lambda a: a[None]
    return (y_p, y_s, st(k_p[:, -keep:]), st(v_p[:, -keep:]), st(c_p), st(n_p), st(m_p),
            st(k_s), st(v_s), st(c_s), st(n_s), st(m_s))
```

```python
import functools

import jax
import jax.numpy as jnp
import numpy as np
from jax import lax
from jax.experimental import pallas as pl
from jax.experimental.pallas import tpu as pltpu

F32 = jnp.float32
BF16 = jnp.bfloat16

D_MODEL = 1024
CHUNK = 64
ATT_HEADS = 8
ATT_HD = 64
ATT_W = ATT_HEADS * ATT_HD
ATT_CTX = 8 * CHUNK
MAX_REL = 128
ML_HEADS = 4
ML_DK = 128
ML_W = ML_HEADS * ML_DK
PEER_HEADS = 8
PEER_NKEYS = 128
PEER_DKEY = 128
PEER_TOPK = 16
PEER_PICKS = PEER_HEADS * PEER_TOPK
EPS = 1e-6
NEG = -1e30

LANES = 128
SUBLANES = 8
ROW_TILE = 256
GATE_COLS = LANES
ATT_CHUNKS_PER_STEP = 4
ML_BATCH_ROWS = 2
GROUP = SUBLANES
PEER_BLOCK = 256
GATHER_SLOTS = 2
ISSUE_AHEAD = 4
EXPERT_ROWS = 2 * D_MODEL // LANES
VMEM_LIMIT = 56 * 1024 * 1024


def _params(*sem):
    return pltpu.CompilerParams(dimension_semantics=sem, vmem_limit_bytes=VMEM_LIMIT)


def _rms(x, g):
    return x * lax.rsqrt(jnp.mean(x * x, axis=-1, keepdims=True) + EPS) * g


def _inproj_kernel(x_ref, g_ref, w_ref, b_ref, wgh_ref, wgl_ref, bg_ref, fb_ref,
                   q_o, k_o, v_o, kf_o, vf_o, mq_o, mk_o, mv_o, mo_o, ga_o, gb_o, gt_o):
    xn = _rms(x_ref[...], g_ref[...])
    xh = xn.astype(BF16)

    def proj(off, width):
        return jnp.dot(xh, w_ref[:, off:off + width], preferred_element_type=F32) + b_ref[:, off:off + width]

    q_o[...] = (proj(0, 512) * (ATT_HD ** -0.5)).astype(BF16)
    k = proj(512, 512)
    kf_o[...] = k
    k_o[...] = k.astype(BF16)
    v = proj(1024, 512)
    vf_o[...] = v
    v_o[...] = v.astype(BF16)
    mq_o[...] = proj(1536, 512).astype(BF16)
    mk_o[...] = (proj(2048, 512) * (ML_DK ** -0.5)).astype(BF16)
    mv_o[...] = proj(2560, 512).astype(BF16)
    mo_o[...] = proj(3072, 512)
    ga_o[...] = proj(3584, D_MODEL)
    gb_o[...] = proj(3584 + D_MODEL, D_MODEL)
    xl = (xn - xh.astype(F32)).astype(BF16)
    zg = (jnp.dot(xh, wgh_ref[...], preferred_element_type=F32)
          + jnp.dot(xh, wgl_ref[...], preferred_element_type=F32)
          + jnp.dot(xl, wgh_ref[...], preferred_element_type=F32)) + bg_ref[...]
    t = zg + fb_ref[...]
    log_sig = jnp.minimum(t, 0.0) - jnp.log1p(jnp.exp(-jnp.abs(t)))
    lane = lax.broadcasted_iota(jnp.int32, zg.shape, 1)
    gt_o[...] = jnp.where(lane < ML_HEADS, zg, log_sig)


def _inproj(x, g1, w_main, b_main, wg_hi, wg_lo, bg, fb):
    n = x.shape[0]
    tm = min(ROW_TILE, n)
    row = lambda w: pl.BlockSpec((tm, w), lambda i: (i, 0))
    full = lambda a: pl.BlockSpec(a.shape, lambda i: (0, 0))
    widths = [(512, BF16)] * 3 + [(512, F32)] * 2 + [(512, BF16)] * 3 + [(512, F32), (D_MODEL, F32), (D_MODEL, F32),
                                                                     (GATE_COLS, F32)]
    return pl.pallas_call(
        _inproj_kernel,
        grid=(n // tm,),
        in_specs=[row(D_MODEL), full(g1), full(w_main), full(b_main), full(wg_hi), full(wg_lo), full(bg), full(fb)],
        out_specs=[row(w) for w, _ in widths],
        out_shape=[jax.ShapeDtypeStruct((n, w), dt) for w, dt in widths],
        compiler_params=_params("parallel"),
        name="inproj",
    )(x, g1, w_main, b_main, wg_hi, wg_lo, bg, fb)


def _attn_kernel(q_ref, k_ref, v_ref, bias_ref, o_ref, *, L, KB, pad):
    c = pl.program_id(1)
    start = pl.multiple_of(c * L, L)
    kb = k_ref[0, pl.ds(start, KB), :]
    vb = v_ref[0, pl.ds(start, KB), :]
    q = q_ref[0].astype(F32)
    col = lax.broadcasted_iota(jnp.int32, (1, KB), 1)
    valid = (start + col) >= pad
    lo = lax.broadcasted_iota(jnp.int32, (L, LANES), 1) < ATT_HD
    for hp in range(ATT_HEADS // 2):
        sl = slice(hp * LANES, (hp + 1) * LANES)
        qp, kp, vp = q[:, sl], kb[:, sl], vb[:, sl]
        qs = jnp.concatenate([jnp.where(lo, qp, 0.0), jnp.where(lo, 0.0, qp)], axis=0).astype(BF16)
        s = lax.dot_general(qs, kp, (((1,), (1,)), ((), ())), preferred_element_type=F32)
        s = jnp.where(valid, s + bias_ref[hp], NEG)
        e = jnp.exp(s - jnp.max(s, axis=-1, keepdims=True))
        pv = jnp.dot(e.astype(BF16), vp, preferred_element_type=F32) / jnp.sum(e, axis=-1, keepdims=True)
        o_ref[0, :, sl] = jnp.where(lo, pv[:L], pv[L:]).astype(BF16)


def _attention(q, k, v, bias, *, L, KB, pad):
    B, S, _ = q.shape
    ktot = k.shape[1]
    bias = bias.reshape(ATT_HEADS // 2, 2 * L, KB)
    return pl.pallas_call(
        functools.partial(_attn_kernel, L=L, KB=KB, pad=pad),
        grid=(B, S // L),
        in_specs=[pl.BlockSpec((1, L, ATT_W), lambda b, c: (b, c, 0)),
                  pl.BlockSpec((1, ktot, ATT_W), lambda b, c: (b, 0, 0)),
                  pl.BlockSpec((1, ktot, ATT_W), lambda b, c: (b, 0, 0)),
                  pl.BlockSpec(bias.shape, lambda b, c: (0, 0, 0))],
        out_specs=pl.BlockSpec((1, L, ATT_W), lambda b, c: (b, c, 0)),
        out_shape=jax.ShapeDtypeStruct((B, S, ATT_W), BF16),
        compiler_params=_params("parallel", "arbitrary"),
        name="attention",
    )(q, k, v, bias)


def _rel_bias(table, L, KB):
    m = np.arange(KB + L - 1)
    g = table[np.clip(L - 1 + ATT_CTX - m, -MAX_REL, MAX_REL) + MAX_REL].T
    return jnp.stack([g[:, L - 1 - l:L - 1 - l + KB] for l in range(L)], axis=1)


def _pair_bias(bias, n=2):
    h, l, _ = bias.shape
    neg = lambda w: jnp.full((h, l, w), NEG, bias.dtype)
    rows = [jnp.concatenate(([neg(r * l)] if r else []) + [bias] + ([neg((n - 1 - r) * l)] if r < n - 1 else []),
                            axis=2) for r in range(n)]
    return jnp.concatenate(rows, axis=1)


def _t128(x):
    rows = x.shape[0]
    if rows < LANES:
        x = jnp.concatenate([x, jnp.zeros((LANES - rows, LANES), x.dtype)], axis=0)
    return x.T[:, :rows]


def _mlstm_kernel(q_ref, k_ref, v_ref, g_ref, c0_ref, n0_ref, m0_ref,
                  h_ref, c_ref, n_ref, m_ref, cs, ns, ms, *, L):
    c = pl.program_id(1)

    @pl.when(c == 0)
    def _():
        cs[...] = c0_ref[...]
        ns[...] = n0_ref[...]
        ms[...] = m0_ref[...]

    tril = lax.broadcasted_iota(jnp.int32, (L, L), 0) >= lax.broadcasted_iota(jnp.int32, (L, L), 1)
    for bb in range(q_ref.shape[0]):
        gates = g_ref[bb]
        row = lax.broadcasted_iota(jnp.int32, gates.shape, 0)
        csum = gates
        sh = 1
        while sh < L:
            csum = csum + jnp.where(row >= sh, pltpu.roll(csum, sh, axis=0), 0.0)
            sh *= 2
        gates_t = _t128(gates)
        csum_t = _t128(csum)
        for h in range(ML_HEADS):
            sl = slice(h * ML_DK, (h + 1) * ML_DK)
            q, k, v = q_ref[bb, :, sl], k_ref[bb, :, sl], v_ref[bb, :, sl]
            b_col = csum[:, ML_HEADS + h:ML_HEADS + h + 1]
            b_row = csum_t[ML_HEADS + h:ML_HEADS + h + 1, :]
            ig_col = gates[:, h:h + 1]
            ig_row = gates_t[h:h + 1, :]
            m_prev = ms[bb, h:h + 1, 0:1]
            a_col = b_col + m_prev
            dmat = jnp.where(tril, b_col - b_row + ig_row, NEG)
            m_t = jnp.maximum(a_col, jnp.max(dmat, axis=-1, keepdims=True))
            w_inter = jnp.exp(a_col - m_t)
            wmat = jnp.exp(dmat - m_t)
            qk = lax.dot_general(q, k, (((1,), (1,)), ((), ())), preferred_element_type=F32)
            wqk = wmat * qk
            c_old = cs[bb, h]
            n_old = ns[bb, h:h + 1, :]
            num = (w_inter * jnp.dot(q, c_old.astype(BF16), preferred_element_type=F32)
                   + jnp.dot(wqk.astype(BF16), v, preferred_element_type=F32))
            den = (w_inter * jnp.sum(q.astype(F32) * n_old, axis=-1, keepdims=True)
                   + jnp.sum(wqk, axis=-1, keepdims=True))
            h_ref[bb, :, sl] = num / jnp.maximum(jnp.abs(den), jnp.exp(-m_t))
            m_new = m_t[L - 1:L, :]
            decay = jnp.exp(a_col[L - 1:L, :] - m_new)
            wl_col = jnp.exp(b_col[L - 1:L, :] - b_col + ig_col - m_new)
            kw = k.astype(F32) * wl_col
            cs[bb, h] = decay * c_old + lax.dot_general(kw.astype(BF16), v, (((0,), (0,)), ((), ())),
                                                        preferred_element_type=F32)
            ns[bb, h:h + 1, :] = decay * n_old + jnp.sum(kw, axis=0, keepdims=True)
            ms[bb, h:h + 1, :] = jnp.broadcast_to(m_new, (1, ML_DK))

    @pl.when(c == pl.num_programs(1) - 1)
    def _():
        c_ref[...] = cs[...]
        n_ref[...] = ns[...]
        m_ref[...] = ms[...]


def _mlstm(q, k, v, gates, c0, n0, m0, *, L):
    B, S, _ = q.shape
    bb = ML_BATCH_ROWS
    seq = lambda w: pl.BlockSpec((bb, L, w), lambda b, c: (b, c, 0))
    st4 = pl.BlockSpec((bb, ML_HEADS, ML_DK, ML_DK), lambda b, c: (b, 0, 0, 0))
    st3 = pl.BlockSpec((bb, ML_HEADS, ML_DK), lambda b, c: (b, 0, 0))
    return pl.pallas_call(
        functools.partial(_mlstm_kernel, L=L),
        grid=(B // bb, S // L),
        in_specs=[seq(ML_W), seq(ML_W), seq(ML_W), seq(GATE_COLS), st4, st3, st3],
        out_specs=[seq(ML_W), st4, st3, st3],
        out_shape=[jax.ShapeDtypeStruct((B, S, ML_W), F32),
                   jax.ShapeDtypeStruct((B, ML_HEADS, ML_DK, ML_DK), F32),
                   jax.ShapeDtypeStruct((B, ML_HEADS, ML_DK), F32),
                   jax.ShapeDtypeStruct((B, ML_HEADS, ML_DK), F32)],
        scratch_shapes=[pltpu.VMEM((bb, ML_HEADS, ML_DK, ML_DK), F32),
                        pltpu.VMEM((bb, ML_HEADS, ML_DK), F32),
                        pltpu.VMEM((bb, ML_HEADS, ML_DK), F32)],
        compiler_params=_params("parallel", "arbitrary"),
        name="mlstm",
    )(q, k, v, gates, c0, n0, m0)


def _merge_kernel(x_ref, att_ref, h_ref, mo_ref, ga_ref, gb_ref, wa_ref, wm_ref, wo_ref, o_ref):
    ml = (jax.nn.sigmoid(mo_ref[...]) * h_ref[...]).astype(BF16)
    a = jnp.dot(att_ref[...], wa_ref[...], preferred_element_type=F32)
    b = jnp.dot(ml, wm_ref[...], preferred_element_type=F32)
    merged = jax.nn.sigmoid(ga_ref[...]) * a + jax.nn.sigmoid(gb_ref[...]) * b
    o_ref[...] = x_ref[...] + jnp.dot(merged.astype(BF16), wo_ref[...], preferred_element_type=F32)


def _merge(x, att, h, mo, ga, gb, wa, wm, wo):
    n = x.shape[0]
    tm = min(ROW_TILE, n)
    row = lambda w: pl.BlockSpec((tm, w), lambda i: (i, 0))
    full = lambda a: pl.BlockSpec(a.shape, lambda i: (0, 0))
    return pl.pallas_call(
        _merge_kernel,
        grid=(n // tm,),
        in_specs=[row(D_MODEL), row(ATT_W), row(ML_W), row(ML_W), row(D_MODEL), row(D_MODEL),
                  full(wa), full(wm), full(wo)],
        out_specs=row(D_MODEL),
        out_shape=jax.ShapeDtypeStruct((n, D_MODEL), F32),
        compiler_params=_params("parallel"),
        name="merge",
    )(x, att, h, mo, ga, gb, wa, wm, wo)


def _top16(s, ids, big):
    vals, win = [], []
    for _ in range(PEER_TOPK):
        m = jnp.max(s, axis=0, keepdims=True)
        i = jnp.min(jnp.where(s == m, ids, big), axis=0, keepdims=True)
        vals.append(m)
        win.append(i)
        s = jnp.where(ids == i, -jnp.inf, s)
    return jnp.concatenate(vals, axis=0), jnp.concatenate(win, axis=0)


_PAIR_IDS = np.array([b for b in range(16)] + [a * 16 + b for a in range(1, 8) for b in range(8)]
                     + [a * 16 for a in range(8, 16)], np.int32)


def _select(rank, table):
    out = jnp.zeros_like(table)
    for a in range(PEER_TOPK):
        out = jnp.where(rank == a, table[a:a + 1], out)
    return out


def _route_kernel(x_ref, g_ref, wq_ref, keys_ref, pair_ref, xn_o, xt_o, idx_o, gw_o, st_ref):
    x = x_ref[...]
    xn = _rms(x, g_ref[...])
    xn_o[...] = xn.reshape(xn_o.shape)
    xt_o[...] = x.reshape(xt_o.shape)
    q = jnp.dot(xn.astype(BF16), wq_ref[...], preferred_element_type=F32)
    st_ref[...] = lax.dot_general(keys_ref[...], q.astype(BF16), (((1,), (1,)), ((), ())),
                                  preferred_element_type=F32)
    lane_tiles = st_ref.shape[1] // LANES
    key_ids = lax.broadcasted_iota(jnp.int32, (PEER_NKEYS, LANES), 0)

    def head_tile(it, carry):
        h = it // lane_tiles
        lanes = pl.ds(pl.multiple_of((it % lane_tiles) * LANES, LANES), LANES)
        r0 = pl.multiple_of(h * 2 * PEER_NKEYS, 2 * PEER_NKEYS)
        sv0, si0 = _top16(st_ref[pl.ds(r0, PEER_NKEYS), lanes], key_ids, PEER_NKEYS)
        sv1, si1 = _top16(st_ref[pl.ds(r0 + PEER_NKEYS, PEER_NKEYS), lanes], key_ids, PEER_NKEYS)
        half = PEER_TOPK // 2
        cand = jnp.concatenate(
            [jnp.broadcast_to(sv0[0:1], (PEER_TOPK, LANES)) + sv1]
            + [jnp.broadcast_to(sv0[a:a + 1], (half, LANES)) + sv1[0:half] for a in range(1, half)]
            + [sv0[half:] + jnp.broadcast_to(sv1[0:1], (half, LANES))], axis=0)
        fv, fi = _top16(cand, pair_ref[...], PEER_TOPK * PEER_TOPK)
        eidx = _select(fi >> 4, si0) * PEER_NKEYS + _select(fi & (PEER_TOPK - 1), si1)
        e = jnp.exp(fv - fv[0:1])
        o0 = pl.multiple_of(h * PEER_TOPK, PEER_TOPK)
        gw_o[pl.ds(o0, PEER_TOPK), lanes] = e / jnp.sum(e, axis=0, keepdims=True)
        idx_o[pl.ds(o0, PEER_TOPK), lanes] = eidx
        return carry

    lax.fori_loop(0, PEER_HEADS * lane_tiles, head_tile, 0)


def _route(x, g2, wq, keys_bd):
    n = x.shape[0]
    tm = min(ROW_TILE, n)
    full = lambda a: pl.BlockSpec(a.shape, lambda i: (0, 0))
    pair_ids = jnp.asarray(np.broadcast_to(_PAIR_IDS[:, None], (_PAIR_IDS.size, LANES)))
    return pl.pallas_call(
        _route_kernel,
        grid=(n // tm,),
        scratch_shapes=[pltpu.VMEM((PEER_HEADS * 2 * PEER_NKEYS, tm), F32)],
        in_specs=[pl.BlockSpec((tm, D_MODEL), lambda i: (i, 0)), full(g2), full(wq), full(keys_bd), full(pair_ids)],
        out_specs=[pl.BlockSpec((tm, SUBLANES, LANES), lambda i: (i, 0, 0)),
                   pl.BlockSpec((tm, SUBLANES, LANES), lambda i: (i, 0, 0)),
                   pl.BlockSpec((PEER_PICKS, tm), lambda i: (0, i)),
                   pl.BlockSpec((PEER_PICKS, tm), lambda i: (0, i))],
        out_shape=[jax.ShapeDtypeStruct((n, SUBLANES, LANES), F32),
                   jax.ShapeDtypeStruct((n, SUBLANES, LANES), F32),
                   jax.ShapeDtypeStruct((PEER_PICKS, n), jnp.int32),
                   jax.ShapeDtypeStruct((PEER_PICKS, n), F32)],
        compiler_params=_params("parallel"),
        name="peer_route",
    )(x, g2, wq, keys_bd, pair_ids)


def _apply_kernel(idx_hbm, gwt_ref, xn_ref, x_ref, fg_ref, tab_hbm, y_ref, idx_s, buf, gsem, isem, coef_ref, *, nblk):
    i = pl.program_id(0)
    ngroups = PEER_BLOCK // GROUP
    rows = GROUP * PEER_PICKS
    blk_words = PEER_BLOCK * PEER_PICKS
    assert ngroups % GATHER_SLOTS == 0 and GATHER_SLOTS == 2

    def idx_copy(blk, islot):
        return pltpu.make_async_copy(idx_hbm.at[pl.ds(blk * blk_words, blk_words)],
                                     idx_s.at[pl.ds(islot * blk_words, blk_words)], isem.at[islot])

    def row_copy(e, slot, r):
        return pltpu.make_async_copy(tab_hbm.at[e], buf.at[slot, r], gsem.at[slot])

    def wait_group(slot):
        pltpu.make_async_copy(tab_hbm.at[pl.ds(0, rows)], buf.at[slot], gsem.at[slot]).wait()

    islot = i % 2
    has_next = i + 1 < nblk

    @pl.when(i == 0)
    def _():
        idx_copy(0, 0).start()
        idx_copy(0, 0).wait()

        def body(r, carry):
            row_copy(idx_s[r], 0, r).start()
            return carry

        lax.fori_loop(0, rows, body, 0, unroll=8)

    @pl.when(has_next)
    def _():
        idx_copy(i + 1, 1 - islot).start()

    lane = lax.broadcasted_iota(jnp.int32, (PEER_PICKS, LANES), 1)
    sub = lax.broadcasted_iota(jnp.int32, (SUBLANES, LANES), 0)
    masks = {1: (sub & 1) == 0, 2: (sub & 2) == 0, 4: (sub & 4) == 0}

    def fold(a, b, h):
        return jnp.where(masks[h], a, b) + pltpu.roll(jnp.where(masks[h], b, a), h, axis=0)

    def row_sums(ps):
        c = [fold(ps[2 * m], ps[2 * m + 1], 1) for m in range(4)]
        d = [fold(c[0], c[1], 2), fold(c[2], c[3], 2)]
        return fold(d[0], d[1], 4)

    def do_group(g, slot, next_word0):
        nslot = 1 - slot
        wait_group(slot)
        t0 = pl.multiple_of(g * GROUP, GROUP)
        lane0 = pl.multiple_of((t0 // LANES) * LANES, LANES)
        gw_tile = gwt_ref[:, pl.ds(lane0, LANES)]
        ys = []
        for j in range(GROUP):
            for k in range(PEER_PICKS):
                r = j * PEER_PICKS + k
                row_copy(idx_s[next_word0 + r], nslot, r).start(priority=k % 2)
            x = xn_ref[t0 + j]
            sums = []
            for c in range(PEER_PICKS // SUBLANES):
                r0 = j * PEER_PICKS + c * SUBLANES
                sums.append(row_sums([buf[slot, r0 + k, 0:SUBLANES, :] * x for k in range(SUBLANES)]))
            act = jnp.sum(jnp.concatenate(sums, axis=0), axis=-1, keepdims=True)
            gelu = 0.5 * act * (1.0 + lax.erf(act * (2.0 ** -0.5)))
            gw_col = jnp.sum(jnp.where(lane == t0 - lane0 + j, gw_tile, 0.0), axis=-1, keepdims=True)
            coef_ref[...] = jnp.broadcast_to(gw_col * gelu, (PEER_PICKS, LANES))
            accs = [jnp.zeros((SUBLANES, LANES), F32) for _ in range(4)]
            for k in range(PEER_PICKS):
                accs[k % 4] = accs[k % 4] + coef_ref[k:k + 1, :] * buf[slot, j * PEER_PICKS + k, SUBLANES:, :]
            y = x_ref[t0 + j] + ((accs[0] + accs[1]) + (accs[2] + accs[3]))
            ms = jnp.sum(jnp.sum(y * y, axis=0, keepdims=True), axis=-1, keepdims=True) * (1.0 / D_MODEL)
            ys.append(y * lax.rsqrt(ms + EPS) * fg_ref[...])
        y_ref[pl.ds(t0, GROUP), :] = jnp.stack(ys, axis=0).reshape(GROUP, D_MODEL)

    def pair_body(gg, carry):
        g = 2 * gg
        do_group(g, 0, islot * blk_words + (g + 1) * rows)
        last = gg + 1 == ngroups // 2

        @pl.when(jnp.logical_and(last, has_next))
        def _():
            idx_copy(i + 1, 1 - islot).wait()

        do_group(g + 1, 1, jnp.where(last, jnp.where(has_next, 1 - islot, islot) * blk_words,
                                     islot * blk_words + (g + 2) * rows))
        return carry

    lax.fori_loop(0, ngroups // 2, pair_body, 0)

    @pl.when(jnp.logical_not(has_next))
    def _():
        wait_group(0)


def _apply(idx, gwt, xn, x, fg, table):
    n = x.shape[0]
    nblk = n // PEER_BLOCK
    tile = pl.BlockSpec((PEER_BLOCK, SUBLANES, LANES), lambda i: (i, 0, 0))
    return pl.pallas_call(
        functools.partial(_apply_kernel, nblk=nblk),
        grid=(nblk,),
        in_specs=[pl.BlockSpec(memory_space=pl.ANY), pl.BlockSpec((PEER_PICKS, PEER_BLOCK), lambda i: (0, i)),
                  tile, tile,
                  pl.BlockSpec(fg.shape, lambda i: (0, 0)), pl.BlockSpec(memory_space=pl.ANY)],
        out_specs=pl.BlockSpec((PEER_BLOCK, D_MODEL), lambda i: (i, 0)),
        out_shape=jax.ShapeDtypeStruct((n, D_MODEL), F32),
        scratch_shapes=[pltpu.SMEM((2 * PEER_BLOCK * PEER_PICKS,), jnp.int32),
                        pltpu.VMEM((GATHER_SLOTS, GROUP * PEER_PICKS, EXPERT_ROWS, LANES), F32),
                        pltpu.SemaphoreType.DMA((GATHER_SLOTS,)),
                        pltpu.SemaphoreType.DMA((2,)),
                        pltpu.VMEM((PEER_PICKS, LANES), F32)],
        compiler_params=_params("arbitrary"),
        name="peer_apply",
    )(idx, gwt, xn, x, fg, table)


def _head_tile_steps(it, st_ref, pair_ids, gw_dst, idx_dst, lane_tiles):
    h = it // lane_tiles
    lanes = pl.ds(pl.multiple_of((it % lane_tiles) * LANES, LANES), LANES)
    r0 = pl.multiple_of(h * 2 * PEER_NKEYS, 2 * PEER_NKEYS)
    key_ids = lax.broadcasted_iota(jnp.int32, (PEER_NKEYS, LANES), 0)
    s = [st_ref[pl.ds(r0, PEER_NKEYS), lanes], st_ref[pl.ds(r0 + PEER_NKEYS, PEER_NKEYS), lanes]]
    vals, wins = ([], []), ([], [])

    def take_max(x, ids, big):
        m = jnp.max(x, axis=0, keepdims=True)
        w = jnp.min(jnp.where(x == m, ids, big), axis=0, keepdims=True)
        return m, w, jnp.where(ids == w, -jnp.inf, x)

    for _ in range(PEER_TOPK):
        for p in range(2):
            m, w, s[p] = take_max(s[p], key_ids, PEER_NKEYS)
            vals[p].append(m)
            wins[p].append(w)
        yield
    sv0, sv1 = jnp.concatenate(vals[0], axis=0), jnp.concatenate(vals[1], axis=0)
    si0, si1 = jnp.concatenate(wins[0], axis=0), jnp.concatenate(wins[1], axis=0)
    half = PEER_TOPK // 2
    cand = jnp.concatenate(
        [jnp.broadcast_to(sv0[0:1], (PEER_TOPK, LANES)) + sv1]
        + [jnp.broadcast_to(sv0[a:a + 1], (half, LANES)) + sv1[0:half] for a in range(1, half)]
        + [sv0[half:] + jnp.broadcast_to(sv1[0:1], (half, LANES))], axis=0)
    fvals, fwins = [], []
    for r in range(PEER_TOPK):
        m, w, cand = take_max(cand, pair_ids, PEER_TOPK * PEER_TOPK)
        fvals.append(m)
        fwins.append(w)
        if r % 2:
            yield
    fv, fi = jnp.concatenate(fvals, axis=0), jnp.concatenate(fwins, axis=0)
    eidx = _select(fi >> 4, si0) * PEER_NKEYS + _select(fi & (PEER_TOPK - 1), si1)
    e = jnp.exp(fv - fv[0:1])
    o0 = pl.multiple_of(h * PEER_TOPK, PEER_TOPK)
    gw_dst[pl.ds(o0, PEER_TOPK), lanes] = e / jnp.sum(e, axis=0, keepdims=True)
    idx_dst[pl.ds(o0, PEER_TOPK), lanes] = eidx
    yield


def _fused_kernel(x0_ref, x1_ref, xnext_ref, g_ref, wq_ref, keys_ref, pair_ref, fg_ref, tab_hbm, y_ref,
                  st_ref, xn_t, x_t, gw_v, idx_tv, idx_rows, idx_s, buf, gsem, isem, coef_ref, *, nblk):
    i = pl.program_id(0)
    ngroups = PEER_BLOCK // GROUP
    rows = GROUP * PEER_PICKS
    lane_tiles = PEER_BLOCK // LANES
    head_tiles = PEER_HEADS * lane_tiles
    assert ngroups // 2 == head_tiles and nblk >= 2
    islot = i % 2
    a3 = i % 3
    w3 = (i + 2) % 3
    has_next = i + 1 < nblk
    has_next2 = i + 2 < nblk
    pair_ids = pair_ref[...]

    def scores(x, slot):
        xn = _rms(x, g_ref[...])
        xn_t[slot] = xn.reshape(xn_t.shape[1:])
        x_t[slot] = x.reshape(x_t.shape[1:])
        q = jnp.dot(xn.astype(BF16), wq_ref[...], preferred_element_type=F32)
        st_ref[...] = lax.dot_general(keys_ref[...], q.astype(BF16), (((1,), (1,)), ((), ())),
                                      preferred_element_type=F32)

    def publish_copy(smem_slot):
        return pltpu.make_async_copy(idx_rows, idx_s.at[pl.ds(smem_slot * PEER_BLOCK, PEER_BLOCK), :],
                                     isem.at[smem_slot])

    def row_copy(e, slot, r):
        return pltpu.make_async_copy(tab_hbm.at[e], buf.at[slot, r], gsem.at[slot])

    def wait_group(slot):
        pltpu.make_async_copy(tab_hbm.at[pl.ds(0, rows)], buf.at[slot], gsem.at[slot]).wait()

    @pl.when(i == 0)
    def _():
        for blk, x_ref in ((0, x0_ref), (1, x1_ref)):
            scores(x_ref[...], blk)

            def body(it, carry):
                for _ in _head_tile_steps(it, st_ref, pair_ids, gw_v.at[blk], idx_tv, lane_tiles):
                    pass
                return carry

            lax.fori_loop(0, head_tiles, body, 0)
            idx_rows[...] = idx_tv[...].T
            publish_copy(blk).start()
            publish_copy(blk).wait()

        def issue(r, carry):
            row_copy(idx_s[r // PEER_PICKS, r % PEER_PICKS], 0, r).start()
            return carry

        lax.fori_loop(0, rows, issue, 0, unroll=8)

    @pl.when(has_next2)
    def _():
        scores(xnext_ref[...], w3)

    lane = lax.broadcasted_iota(jnp.int32, (PEER_PICKS, LANES), 1)
    sub = lax.broadcasted_iota(jnp.int32, (SUBLANES, LANES), 0)
    masks = {1: (sub & 1) == 0, 2: (sub & 2) == 0, 4: (sub & 4) == 0}

    def fold(a, b, h):
        return jnp.where(masks[h], a, b) + pltpu.roll(jnp.where(masks[h], b, a), h, axis=0)

    def row_sums(ps):
        c = [fold(ps[2 * m], ps[2 * m + 1], 1) for m in range(4)]
        d = [fold(c[0], c[1], 2), fold(c[2], c[3], 2)]
        return fold(d[0], d[1], 4)

    def do_group(g, slot, next_row0, between):
        nslot = 1 - slot

        def issue_steps(j):
            for k in range(PEER_PICKS):
                row_copy(idx_s[next_row0 + j, k], nslot, j * PEER_PICKS + k).start(priority=k % 2)
                if k % 4 == 3:
                    yield

        for j in range(ISSUE_AHEAD):
            for _ in issue_steps(j):
                pass
        wait_group(slot)
        t0 = pl.multiple_of(g * GROUP, GROUP)
        lane0 = pl.multiple_of((t0 // LANES) * LANES, LANES)
        gw_tile = gw_v[a3, :, pl.ds(lane0, LANES)]
        ys = []
        for j in range(GROUP):
            issue = issue_steps(j + ISSUE_AHEAD) if j + ISSUE_AHEAD < GROUP else iter(())
            x = xn_t[a3, t0 + j]
            sums = []
            for c in range(PEER_PICKS // SUBLANES):
                next(issue, None)
                r0 = j * PEER_PICKS + c * SUBLANES
                sums.append(row_sums([buf[slot, r0 + k, 0:SUBLANES, :] * x for k in range(SUBLANES)]))
            act = jnp.sum(jnp.concatenate(sums, axis=0), axis=-1, keepdims=True)
            gelu = 0.5 * act * (1.0 + lax.erf(act * (2.0 ** -0.5)))
            gw_col = jnp.sum(jnp.where(lane == t0 - lane0 + j, gw_tile, 0.0), axis=-1, keepdims=True)
            coef_ref[...] = jnp.broadcast_to(gw_col * gelu, (PEER_PICKS, LANES))
            accs = [jnp.zeros((SUBLANES, LANES), F32) for _ in range(4)]
            for k in range(PEER_PICKS):
                if k % SUBLANES == 0:
                    next(issue, None)
                accs[k % 4] = accs[k % 4] + coef_ref[k:k + 1, :] * buf[slot, j * PEER_PICKS + k, SUBLANES:, :]
            for _ in issue:
                pass
            y = x_t[a3, t0 + j] + ((accs[0] + accs[1]) + (accs[2] + accs[3]))
            ms = jnp.sum(jnp.sum(y * y, axis=0, keepdims=True), axis=-1, keepdims=True) * (1.0 / D_MODEL)
            ys.append(y * lax.rsqrt(ms + EPS) * fg_ref[...])
            between()
        y_ref[pl.ds(t0, GROUP), :] = jnp.stack(ys, axis=0).reshape(GROUP, D_MODEL)

    def pair_body(gg, carry):
        g = 2 * gg
        last = gg + 1 == ngroups // 2

        @pl.when(jnp.logical_and(last, jnp.logical_and(has_next, i >= 1)))
        def _():
            publish_copy(1 - islot).wait()

        steps = _head_tile_steps(gg, st_ref, pair_ids, gw_v.at[w3], idx_tv, lane_tiles)

        def between():
            next(steps, None)
            next(steps, None)

        do_group(g, 0, islot * PEER_BLOCK + (g + 1) * GROUP, between)
        do_group(g + 1, 1, jnp.where(last, jnp.where(has_next, 1 - islot, islot) * PEER_BLOCK,
                                     islot * PEER_BLOCK + (g + 2) * GROUP), between)
        for _ in steps:
            pass
        return carry

    lax.fori_loop(0, ngroups // 2, pair_body, 0)

    @pl.when(has_next2)
    def _():
        idx_rows[...] = idx_tv[...].T
        publish_copy(islot).start()

    @pl.when(jnp.logical_not(has_next))
    def _():
        wait_group(0)


def _route_apply(x, g2, wq, keys_bd, fg, table):
    n = x.shape[0]
    nblk = n // PEER_BLOCK
    full = lambda a: pl.BlockSpec(a.shape, lambda i: (0,) * a.ndim)
    pair_ids = jnp.asarray(np.broadcast_to(_PAIR_IDS[:, None], (_PAIR_IDS.size, LANES)))
    blk = lambda f: pl.BlockSpec((PEER_BLOCK, D_MODEL), f)
    return pl.pallas_call(
        functools.partial(_fused_kernel, nblk=nblk),
        grid=(nblk,),
        in_specs=[blk(lambda i: (0, 0)), blk(lambda i: (1, 0)), blk(lambda i: (jnp.minimum(i + 2, nblk - 1), 0)),
                  full(g2), full(wq), full(keys_bd), full(pair_ids), full(fg), pl.BlockSpec(memory_space=pl.ANY)],
        out_specs=blk(lambda i: (i, 0)),
        out_shape=jax.ShapeDtypeStruct((n, D_MODEL), F32),
        scratch_shapes=[pltpu.VMEM((PEER_HEADS * 2 * PEER_NKEYS, PEER_BLOCK), F32),
                        pltpu.VMEM((3, PEER_BLOCK, SUBLANES, LANES), F32),
                        pltpu.VMEM((3, PEER_BLOCK, SUBLANES, LANES), F32),
                        pltpu.VMEM((3, PEER_PICKS, PEER_BLOCK), F32),
                        pltpu.VMEM((PEER_PICKS, PEER_BLOCK), jnp.int32),
                        pltpu.VMEM((PEER_BLOCK, PEER_PICKS), jnp.int32),
                        pltpu.SMEM((2 * PEER_BLOCK, PEER_PICKS), jnp.int32),
                        pltpu.VMEM((GATHER_SLOTS, GROUP * PEER_PICKS, EXPERT_ROWS, LANES), F32),
                        pltpu.SemaphoreType.DMA((GATHER_SLOTS,)),
                        pltpu.SemaphoreType.DMA((2,)),
                        pltpu.VMEM((PEER_PICKS, LANES), F32)],
        compiler_params=_params("arbitrary"),
        name="peer_route_apply",
    )(x, x, x, g2, wq, keys_bd, pair_ids, fg, table)


def _layer(x3, weights, att_kv, att_geom, ml_state, L):
    B, S, _ = x3.shape
    n = B * S
    x = x3.reshape(n, D_MODEL)
    (q, k, v, kf, vf, mq, mk, mv, mo, ga, gb, gates) = _inproj(
        x, weights["g1"], weights["w_main"], weights["b_main"], weights["wg_hi"], weights["wg_lo"],
        weights["bg"], weights["fb"])
    sh = lambda a: a.reshape(B, S, a.shape[-1])
    k_all, v_all = att_kv(sh(k), sh(v))
    att = _attention(sh(q), k_all, v_all, att_geom["bias"], L=att_geom["L"], KB=att_geom["KB"], pad=att_geom["pad"])
    h, c_new, n_new, m_new = _mlstm(sh(mq), sh(mk), sh(mv), sh(gates), *ml_state, L=L)
    x2 = _merge(x, att.reshape(n, ATT_W), h.reshape(n, ML_W), mo, ga, gb,
                weights["wa"], weights["wm"], weights["wo"])
    fg = weights["fg"].reshape(SUBLANES, LANES)
    if n >= 2 * PEER_BLOCK:
        y = _route_apply(x2, weights["g2"], weights["wq"], weights["keys_bd"], fg, weights["table"])
    else:
        xn2, x2_tiles, idx_t, gw_t = _route(x2, weights["g2"], weights["wq"], weights["keys_bd"])
        y = _apply(idx_t.T.reshape(n * PEER_PICKS), gw_t, xn2, x2_tiles, fg, weights["table"])
    return (y.reshape(B, S, D_MODEL), sh(kf).reshape(B, S, ATT_HEADS, ATT_HD), sh(vf).reshape(B, S, ATT_HEADS, ATT_HD),
            c_new, n_new, m_new[:, :, 0])


def kernel(x_prompt, x_sample, cache_att_k, cache_att_v, state_mlstm_C, state_mlstm_n, state_mlstm_m, norm1_g, w_in, b_in, ml_f_bias, att_rel_bias, w_att_branch, w_ml_branch, w_out, norm2_g, peer_wq, peer_sub_keys, peer_u, peer_v, final_g):
    depth = w_in.shape[0]
    assert depth == 1, "single-layer step"
    l = 0
    w = w_in[l]
    b = b_in[l]
    gate_lo, gate_hi = 7 * 512, 7 * 512 + 2 * ML_HEADS
    w_main = jnp.concatenate([w[:, :gate_lo], w[:, gate_hi:]], axis=1).astype(BF16)
    b_main = jnp.concatenate([b[:gate_lo], b[gate_hi:]])[None, :]
    wg = jnp.pad(w[:, gate_lo:gate_hi], ((0, 0), (0, GATE_COLS - 2 * ML_HEADS)))
    wg_hi = wg.astype(BF16)
    wg_lo = (wg - wg_hi.astype(F32)).astype(BF16)
    bg = jnp.pad(b[gate_lo:gate_hi], (0, GATE_COLS - 2 * ML_HEADS))[None, :]
    fb = jnp.pad(ml_f_bias[l], (ML_HEADS, GATE_COLS - 2 * ML_HEADS))[None, :]
    sk = peer_sub_keys[l].reshape(PEER_HEADS * 2, PEER_NKEYS, PEER_DKEY // 2)
    eye = jnp.eye(PEER_HEADS * 2, dtype=F32)
    keys_bd = (sk[:, :, None, :] * eye[:, None, :, None]).reshape(PEER_HEADS * 2 * PEER_NKEYS, D_MODEL).astype(BF16)
    weights = dict(
        g1=norm1_g[l][None, :], w_main=w_main, b_main=b_main, wg_hi=wg_hi, wg_lo=wg_lo, bg=bg, fb=fb,
        wa=w_att_branch[l].astype(BF16), wm=w_ml_branch[l].astype(BF16), wo=w_out[l].astype(BF16),
        g2=norm2_g[l][None, :], wq=peer_wq[l].astype(BF16), keys_bd=keys_bd, fg=final_g[None, :],
        table=jnp.concatenate([peer_u[l].reshape(-1, SUBLANES, LANES), peer_v[l].reshape(-1, SUBLANES, LANES)],
                              axis=1))

    Bs, T, _ = x_sample.shape
    P = cache_att_k.shape[2]
    assert P == ATT_CTX
    ck = cache_att_k[l].reshape(Bs, P, ATT_W).astype(BF16)
    cv = cache_att_v[l].reshape(Bs, P, ATT_W).astype(BF16)
    cat_kv = lambda k, v: (jnp.concatenate([ck, k], axis=1), jnp.concatenate([cv, v], axis=1))
    geom_s = dict(bias=_rel_bias(att_rel_bias[l], T, P + T), L=T, KB=P + T, pad=0)
    state_s = (state_mlstm_C[l], state_mlstm_n[l],
               jnp.broadcast_to(state_mlstm_m[l][:, :, None], (Bs, ML_HEADS, ML_DK)))
    y_s, k_s, v_s, c_s, n_s, m_s = _layer(x_sample, weights, cat_kv, geom_s, state_s, T)

    B, S, _ = x_prompt.shape
    pad_kv = lambda k, v: (jnp.pad(k, ((0, 0), (ATT_CTX, 0), (0, 0))), jnp.pad(v, ((0, 0), (ATT_CTX, 0), (0, 0))))
    geom_p = dict(bias=_pair_bias(_rel_bias(att_rel_bias[l], CHUNK, ATT_CTX + CHUNK), ATT_CHUNKS_PER_STEP),
                  L=ATT_CHUNKS_PER_STEP * CHUNK, KB=ATT_CTX + ATT_CHUNKS_PER_STEP * CHUNK, pad=ATT_CTX)
    state_p = (jnp.zeros((B, ML_HEADS, ML_DK, ML_DK), F32), jnp.zeros((B, ML_HEADS, ML_DK), F32),
               jnp.full((B, ML_HEADS, ML_DK), NEG, F32))
    y_p, k_p, v_p, c_p, n_p, m_p = _layer(x_prompt, weights, pad_kv, geom_p, state_p, CHUNK)
    keep = min(ATT_CTX, S)

    st = lambda a: a[None]
    return (y_p, y_s, st(k_p[:, -keep:]), st(v_p[:, -keep:]), st(c_p), st(n_p), st(m_p),
            st(k_s), st(v_s), st(c_s), st(n_s), st(m_s))
```

```python
import functools

import jax
import jax.numpy as jnp
import numpy as np
from jax import lax
from jax.experimental import pallas as pl
from jax.experimental.pallas import tpu as pltpu

F32 = jnp.float32
BF16 = jnp.bfloat16

D_MODEL = 1024
CHUNK = 64
ATT_HEADS = 8
ATT_HD = 64
ATT_W = ATT_HEADS * ATT_HD
ATT_CTX = 8 * CHUNK
MAX_REL = 128
ML_HEADS = 4
ML_DK = 128
ML_W = ML_HEADS * ML_DK
PEER_HEADS = 8
PEER_NKEYS = 128
PEER_DKEY = 128
PEER_TOPK = 16
PEER_PICKS = PEER_HEADS * PEER_TOPK
EPS = 1e-6
NEG = -1e30

LANES = 128
SUBLANES = 8
ROW_TILE = 256
GATE_COLS = LANES
ATT_CHUNKS_PER_STEP = 4
ML_BATCH_ROWS = 2
GROUP = SUBLANES
PEER_BLOCK = 256
GATHER_SLOTS = 2
ISSUE_AHEAD = 4
EXPERT_ROWS = 2 * D_MODEL // LANES
VMEM_LIMIT = 56 * 1024 * 1024


def _params(*sem):
    return pltpu.CompilerParams(dimension_semantics=sem, vmem_limit_bytes=VMEM_LIMIT)


def _rms(x, g):
    return x * lax.rsqrt(jnp.mean(x * x, axis=-1, keepdims=True) + EPS) * g


def _inproj_kernel(x_ref, g_ref, w_ref, b_ref, wgh_ref, wgl_ref, bg_ref, fb_ref,
                   q_o, k_o, v_o, kf_o, vf_o, mq_o, mk_o, mv_o, mo_o, ga_o, gb_o, gt_o):
    xn = _rms(x_ref[...], g_ref[...])
    xh = xn.astype(BF16)

    def proj(off, width):
        return jnp.dot(xh, w_ref[:, off:off + width], preferred_element_type=F32) + b_ref[:, off:off + width]

    q_o[...] = (proj(0, 512) * (ATT_HD ** -0.5)).astype(BF16)
    k = proj(512, 512)
    kf_o[...] = k
    k_o[...] = k.astype(BF16)
    v = proj(1024, 512)
    vf_o[...] = v
    v_o[...] = v.astype(BF16)
    mq_o[...] = proj(1536, 512).astype(BF16)
    mk_o[...] = (proj(2048, 512) * (ML_DK ** -0.5)).astype(BF16)
    mv_o[...] = proj(2560, 512).astype(BF16)
    mo_o[...] = proj(3072, 512)
    ga_o[...] = proj(3584, D_MODEL)
    gb_o[...] = proj(3584 + D_MODEL, D_MODEL)
    xl = (xn - xh.astype(F32)).astype(BF16)
    zg = (jnp.dot(xh, wgh_ref[...], preferred_element_type=F32)
          + jnp.dot(xh, wgl_ref[...], preferred_element_type=F32)
          + jnp.dot(xl, wgh_ref[...], preferred_element_type=F32)) + bg_ref[...]
    t = zg + fb_ref[...]
    log_sig = jnp.minimum(t, 0.0) - jnp.log1p(jnp.exp(-jnp.abs(t)))
    lane = lax.broadcasted_iota(jnp.int32, zg.shape, 1)
    gt_o[...] = jnp.where(lane < ML_HEADS, zg, log_sig)


def _inproj(x, g1, w_main, b_main, wg_hi, wg_lo, bg, fb):
    n = x.shape[0]
    tm = min(ROW_TILE, n)
    row = lambda w: pl.BlockSpec((tm, w), lambda i: (i, 0))
    full = lambda a: pl.BlockSpec(a.shape, lambda i: (0, 0))
    widths = [(512, BF16)] * 3 + [(512, F32)] * 2 + [(512, BF16)] * 3 + [(512, F32), (D_MODEL, F32), (D_MODEL, F32),
                                                                     (GATE_COLS, F32)]
    return pl.pallas_call(
        _inproj_kernel,
        grid=(n // tm,),
        in_specs=[row(D_MODEL), full(g1), full(w_main), full(b_main), full(wg_hi), full(wg_lo), full(bg), full(fb)],
        out_specs=[row(w) for w, _ in widths],
        out_shape=[jax.ShapeDtypeStruct((n, w), dt) for w, dt in widths],
        compiler_params=_params("parallel"),
        name="inproj",
    )(x, g1, w_main, b_main, wg_hi, wg_lo, bg, fb)


def _attn_kernel(q_ref, k_ref, v_ref, bias_ref, o_ref, *, L, KB, pad):
    c = pl.program_id(1)
    start = pl.multiple_of(c * L, L)
    kb = k_ref[0, pl.ds(start, KB), :]
    vb = v_ref[0, pl.ds(start, KB), :]
    q = q_ref[0].astype(F32)
    col = lax.broadcasted_iota(jnp.int32, (1, KB), 1)
    valid = (start + col) >= pad
    lo = lax.broadcasted_iota(jnp.int32, (L, LANES), 1) < ATT_HD
    for hp in range(ATT_HEADS // 2):
        sl = slice(hp * LANES, (hp + 1) * LANES)
        qp, kp, vp = q[:, sl], kb[:, sl], vb[:, sl]
        qs = jnp.concatenate([jnp.where(lo, qp, 0.0), jnp.where(lo, 0.0, qp)], axis=0).astype(BF16)
        s = lax.dot_general(qs, kp, (((1,), (1,)), ((), ())), preferred_element_type=F32)
        s = jnp.where(valid, s + bias_ref[hp], NEG)
        e = jnp.exp(s - jnp.max(s, axis=-1, keepdims=True))
        pv = jnp.dot(e.astype(BF16), vp, preferred_element_type=F32) / jnp.sum(e, axis=-1, keepdims=True)
        o_ref[0, :, sl] = jnp.where(lo, pv[:L], pv[L:]).astype(BF16)


def _attention(q, k, v, bias, *, L, KB, pad):
    B, S, _ = q.shape
    ktot = k.shape[1]
    bias = bias.reshape(ATT_HEADS // 2, 2 * L, KB)
    return pl.pallas_call(
        functools.partial(_attn_kernel, L=L, KB=KB, pad=pad),
        grid=(B, S // L),
        in_specs=[pl.BlockSpec((1, L, ATT_W), lambda b, c: (b, c, 0)),
                  pl.BlockSpec((1, ktot, ATT_W), lambda b, c: (b, 0, 0)),
                  pl.BlockSpec((1, ktot, ATT_W), lambda b, c: (b, 0, 0)),
                  pl.BlockSpec(bias.shape, lambda b, c: (0, 0, 0))],
        out_specs=pl.BlockSpec((1, L, ATT_W), lambda b, c: (b, c, 0)),
        out_shape=jax.ShapeDtypeStruct((B, S, ATT_W), BF16),
        compiler_params=_params("parallel", "arbitrary"),
        name="attention",
    )(q, k, v, bias)


def _rel_bias(table, L, KB):
    m = np.arange(KB + L - 1)
    g = table[np.clip(L - 1 + ATT_CTX - m, -MAX_REL, MAX_REL) + MAX_REL].T
    return jnp.stack([g[:, L - 1 - l:L - 1 - l + KB] for l in range(L)], axis=1)


def _pair_bias(bias, n=2):
    h, l, _ = bias.shape
    neg = lambda w: jnp.full((h, l, w), NEG, bias.dtype)
    rows = [jnp.concatenate(([neg(r * l)] if r else []) + [bias] + ([neg((n - 1 - r) * l)] if r < n - 1 else []),
                            axis=2) for r in range(n)]
    return jnp.concatenate(rows, axis=1)


def _t128(x):
    rows = x.shape[0]
    if rows < LANES:
        x = jnp.concatenate([x, jnp.zeros((LANES - rows, LANES), x.dtype)], axis=0)
    return x.T[:, :rows]


def _mlstm_kernel(q_ref, k_ref, v_ref, g_ref, c0_ref, n0_ref, m0_ref,
                  h_ref, c_ref, n_ref, m_ref, cs, ns, ms, *, L):
    c = pl.program_id(1)

    @pl.when(c == 0)
    def _():
        cs[...] = c0_ref[...]
        ns[...] = n0_ref[...]
        ms[...] = m0_ref[...]

    tril = lax.broadcasted_iota(jnp.int32, (L, L), 0) >= lax.broadcasted_iota(jnp.int32, (L, L), 1)
    for bb in range(q_ref.shape[0]):
        gates = g_ref[bb]
        row = lax.broadcasted_iota(jnp.int32, gates.shape, 0)
        csum = gates
        sh = 1
        while sh < L:
            csum = csum + jnp.where(row >= sh, pltpu.roll(csum, sh, axis=0), 0.0)
            sh *= 2
        gates_t = _t128(gates)
        csum_t = _t128(csum)
        for h in range(ML_HEADS):
            sl = slice(h * ML_DK, (h + 1) * ML_DK)
            q, k, v = q_ref[bb, :, sl], k_ref[bb, :, sl], v_ref[bb, :, sl]
            b_col = csum[:, ML_HEADS + h:ML_HEADS + h + 1]
            b_row = csum_t[ML_HEADS + h:ML_HEADS + h + 1, :]
            ig_col = gates[:, h:h + 1]
            ig_row = gates_t[h:h + 1, :]
            m_prev = ms[bb, h:h + 1, 0:1]
            a_col = b_col + m_prev
            dmat = jnp.where(tril, b_col - b_row + ig_row, NEG)
            m_t = jnp.maximum(a_col, jnp.max(dmat, axis=-1, keepdims=True))
            w_inter = jnp.exp(a_col - m_t)
            wmat = jnp.exp(dmat - m_t)
            qk = lax.dot_general(q, k, (((1,), (1,)), ((), ())), preferred_element_type=F32)
            wqk = wmat * qk
            c_old = cs[bb, h]
            n_old = ns[bb, h:h + 1, :]
            num = (w_inter * jnp.dot(q, c_old.astype(BF16), preferred_element_type=F32)
                   + jnp.dot(wqk.astype(BF16), v, preferred_element_type=F32))
            den = (w_inter * jnp.sum(q.astype(F32) * n_old, axis=-1, keepdims=True)
                   + jnp.sum(wqk, axis=-1, keepdims=True))
            h_ref[bb, :, sl] = num / jnp.maximum(jnp.abs(den), jnp.exp(-m_t))
            m_new = m_t[L - 1:L, :]
            decay = jnp.exp(a_col[L - 1:L, :] - m_new)
            wl_col = jnp.exp(b_col[L - 1:L, :] - b_col + ig_col - m_new)
            kw = k.astype(F32) * wl_col
            cs[bb, h] = decay * c_old + lax.dot_general(kw.astype(BF16), v, (((0,), (0,)), ((), ())),
                                                        preferred_element_type=F32)
            ns[bb, h:h + 1, :] = decay * n_old + jnp.sum(kw, axis=0, keepdims=True)
            ms[bb, h:h + 1, :] = jnp.broadcast_to(m_new, (1, ML_DK))

    @pl.when(c == pl.num_programs(1) - 1)
    def _():
        c_ref[...] = cs[...]
        n_ref[...] = ns[...]
        m_ref[...] = ms[...]


def _mlstm(q, k, v, gates, c0, n0, m0, *, L):
    B, S, _ = q.shape
    bb = ML_BATCH_ROWS
    seq = lambda w: pl.BlockSpec((bb, L, w), lambda b, c: (b, c, 0))
    st4 = pl.BlockSpec((bb, ML_HEADS, ML_DK, ML_DK), lambda b, c: (b, 0, 0, 0))
    st3 = pl.BlockSpec((bb, ML_HEADS, ML_DK), lambda b, c: (b, 0, 0))
    return pl.pallas_call(
        functools.partial(_mlstm_kernel, L=L),
        grid=(B // bb, S // L),
        in_specs=[seq(ML_W), seq(ML_W), seq(ML_W), seq(GATE_COLS), st4, st3, st3],
        out_specs=[seq(ML_W), st4, st3, st3],
        out_shape=[jax.ShapeDtypeStruct((B, S, ML_W), F32),
                   jax.ShapeDtypeStruct((B, ML_HEADS, ML_DK, ML_DK), F32),
                   jax.ShapeDtypeStruct((B, ML_HEADS, ML_DK), F32),
                   jax.ShapeDtypeStruct((B, ML_HEADS, ML_DK), F32)],
        scratch_shapes=[pltpu.VMEM((bb, ML_HEADS, ML_DK, ML_DK), F32),
                        pltpu.VMEM((bb, ML_HEADS, ML_DK), F32),
                        pltpu.VMEM((bb, ML_HEADS, ML_DK), F32)],
        compiler_params=_params("parallel", "arbitrary"),
        name="mlstm",
    )(q, k, v, gates, c0, n0, m0)


def _merge_kernel(x_ref, att_ref, h_ref, mo_ref, ga_ref, gb_ref, wa_ref, wm_ref, wo_ref, o_ref):
    ml = (jax.nn.sigmoid(mo_ref[...]) * h_ref[...]).astype(BF16)
    a = jnp.dot(att_ref[...], wa_ref[...], preferred_element_type=F32)
    b = jnp.dot(ml, wm_ref[...], preferred_element_type=F32)
    merged = jax.nn.sigmoid(ga_ref[...]) * a + jax.nn.sigmoid(gb_ref[...]) * b
    o_ref[...] = x_ref[...] + jnp.dot(merged.astype(BF16), wo_ref[...], preferred_element_type=F32)


def _merge(x, att, h, mo, ga, gb, wa, wm, wo):
    n = x.shape[0]
    tm = min(ROW_TILE, n)
    row = lambda w: pl.BlockSpec((tm, w), lambda i: (i, 0))
    full = lambda a: pl.BlockSpec(a.shape, lambda i: (0, 0))
    return pl.pallas_call(
        _merge_kernel,
        grid=(n // tm,),
        in_specs=[row(D_MODEL), row(ATT_W), row(ML_W), row(ML_W), row(D_MODEL), row(D_MODEL),
                  full(wa), full(wm), full(wo)],
        out_specs=row(D_MODEL),
        out_shape=jax.ShapeDtypeStruct((n, D_MODEL), F32),
        compiler_params=_params("parallel"),
        name="merge",
    )(x, att, h, mo, ga, gb, wa, wm, wo)


def _top16(s, ids, big):
    vals, win = [], []
    for _ in range(PEER_TOPK):
        m = jnp.max(s, axis=0, keepdims=True)
        i = jnp.min(jnp.where(s == m, ids, big), axis=0, keepdims=True)
        vals.append(m)
        win.append(i)
        s = jnp.where(ids == i, -jnp.inf, s)
    return jnp.concatenate(vals, axis=0), jnp.concatenate(win, axis=0)


_PAIR_IDS = np.array([b for b in range(16)] + [a * 16 + b for a in range(1, 8) for b in range(8)]
                     + [a * 16 for a in range(8, 16)], np.int32)


def _select(rank, table):
    out = jnp.zeros_like(table)
    for a in range(PEER_TOPK):
        out = jnp.where(rank == a, table[a:a + 1], out)
    return out


def _route_kernel(x_ref, g_ref, wq_ref, keys_ref, pair_ref, xn_o, xt_o, idx_o, gw_o, st_ref):
    x = x_ref[...]
    xn = _rms(x, g_ref[...])
    xn_o[...] = xn.reshape(xn_o.shape)
    xt_o[...] = x.reshape(xt_o.shape)
    q = jnp.dot(xn.astype(BF16), wq_ref[...], preferred_element_type=F32)
    st_ref[...] = lax.dot_general(keys_ref[...], q.astype(BF16), (((1,), (1,)), ((), ())),
                                  preferred_element_type=F32)
    lane_tiles = st_ref.shape[1] // LANES
    key_ids = lax.broadcasted_iota(jnp.int32, (PEER_NKEYS, LANES), 0)

    def head_tile(it, carry):
        h = it // lane_tiles
        lanes = pl.ds(pl.multiple_of((it % lane_tiles) * LANES, LANES), LANES)
        r0 = pl.multiple_of(h * 2 * PEER_NKEYS, 2 * PEER_NKEYS)
        sv0, si0 = _top16(st_ref[pl.ds(r0, PEER_NKEYS), lanes], key_ids, PEER_NKEYS)
        sv1, si1 = _top16(st_ref[pl.ds(r0 + PEER_NKEYS, PEER_NKEYS), lanes], key_ids, PEER_NKEYS)
        half = PEER_TOPK // 2
        cand = jnp.concatenate(
            [jnp.broadcast_to(sv0[0:1], (PEER_TOPK, LANES)) + sv1]
            + [jnp.broadcast_to(sv0[a:a + 1], (half, LANES)) + sv1[0:half] for a in range(1, half)]
            + [sv0[half:] + jnp.broadcast_to(sv1[0:1], (half, LANES))], axis=0)
        fv, fi = _top16(cand, pair_ref[...], PEER_TOPK * PEER_TOPK)
        eidx = _select(fi >> 4, si0) * PEER_NKEYS + _select(fi & (PEER_TOPK - 1), si1)
        e = jnp.exp(fv - fv[0:1])
        o0 = pl.multiple_of(h * PEER_TOPK, PEER_TOPK)
        gw_o[pl.ds(o0, PEER_TOPK), lanes] = e / jnp.sum(e, axis=0, keepdims=True)
        idx_o[pl.ds(o0, PEER_TOPK), lanes] = eidx
        return carry

    lax.fori_loop(0, PEER_HEADS * lane_tiles, head_tile, 0)


def _route(x, g2, wq, keys_bd):
    n = x.shape[0]
    tm = min(ROW_TILE, n)
    full = lambda a: pl.BlockSpec(a.shape, lambda i: (0, 0))
    pair_ids = jnp.asarray(np.broadcast_to(_PAIR_IDS[:, None], (_PAIR_IDS.size, LANES)))
    return pl.pallas_call(
        _route_kernel,
        grid=(n // tm,),
        scratch_shapes=[pltpu.VMEM((PEER_HEADS * 2 * PEER_NKEYS, tm), F32)],
        in_specs=[pl.BlockSpec((tm, D_MODEL), lambda i: (i, 0)), full(g2), full(wq), full(keys_bd), full(pair_ids)],
        out_specs=[pl.BlockSpec((tm, SUBLANES, LANES), lambda i: (i, 0, 0)),
                   pl.BlockSpec((tm, SUBLANES, LANES), lambda i: (i, 0, 0)),
                   pl.BlockSpec((PEER_PICKS, tm), lambda i: (0, i)),
                   pl.BlockSpec((PEER_PICKS, tm), lambda i: (0, i))],
        out_shape=[jax.ShapeDtypeStruct((n, SUBLANES, LANES), F32),
                   jax.ShapeDtypeStruct((n, SUBLANES, LANES), F32),
                   jax.ShapeDtypeStruct((PEER_PICKS, n), jnp.int32),
                   jax.ShapeDtypeStruct((PEER_PICKS, n), F32)],
        compiler_params=_params("parallel"),
        name="peer_route",
    )(x, g2, wq, keys_bd, pair_ids)


def _apply_kernel(idx_hbm, gwt_ref, xn_ref, x_ref, fg_ref, tab_hbm, y_ref, idx_s, buf, gsem, isem, coef_ref, *, nblk):
    i = pl.program_id(0)
    ngroups = PEER_BLOCK // GROUP
    rows = GROUP * PEER_PICKS
    blk_words = PEER_BLOCK * PEER_PICKS
    assert ngroups % GATHER_SLOTS == 0 and GATHER_SLOTS == 2

    def idx_copy(blk, islot):
        return pltpu.make_async_copy(idx_hbm.at[pl.ds(blk * blk_words, blk_words)],
                                     idx_s.at[pl.ds(islot * blk_words, blk_words)], isem.at[islot])

    def row_copy(e, slot, r):
        return pltpu.make_async_copy(tab_hbm.at[e], buf.at[slot, r], gsem.at[slot])

    def wait_group(slot):
        pltpu.make_async_copy(tab_hbm.at[pl.ds(0, rows)], buf.at[slot], gsem.at[slot]).wait()

    islot = i % 2
    has_next = i + 1 < nblk

    @pl.when(i == 0)
    def _():
        idx_copy(0, 0).start()
        idx_copy(0, 0).wait()

        def body(r, carry):
            row_copy(idx_s[r], 0, r).start()
            return carry

        lax.fori_loop(0, rows, body, 0, unroll=8)

    @pl.when(has_next)
    def _():
        idx_copy(i + 1, 1 - islot).start()

    lane = lax.broadcasted_iota(jnp.int32, (PEER_PICKS, LANES), 1)
    sub = lax.broadcasted_iota(jnp.int32, (SUBLANES, LANES), 0)
    masks = {1: (sub & 1) == 0, 2: (sub & 2) == 0, 4: (sub & 4) == 0}

    def fold(a, b, h):
        return jnp.where(masks[h], a, b) + pltpu.roll(jnp.where(masks[h], b, a), h, axis=0)

    def row_sums(ps):
        c = [fold(ps[2 * m], ps[2 * m + 1], 1) for m in range(4)]
        d = [fold(c[0], c[1], 2), fold(c[2], c[3], 2)]
        return fold(d[0], d[1], 4)

    def do_group(g, slot, next_word0):
        nslot = 1 - slot
        wait_group(slot)
        t0 = pl.multiple_of(g * GROUP, GROUP)
        lane0 = pl.multiple_of((t0 // LANES) * LANES, LANES)
        gw_tile = gwt_ref[:, pl.ds(lane0, LANES)]
        ys = []
        for j in range(GROUP):
            for k in range(PEER_PICKS):
                r = j * PEER_PICKS + k
                row_copy(idx_s[next_word0 + r], nslot, r).start(priority=k % 2)
            x = xn_ref[t0 + j]
            sums = []
            for c in range(PEER_PICKS // SUBLANES):
                r0 = j * PEER_PICKS + c * SUBLANES
                sums.append(row_sums([buf[slot, r0 + k, 0:SUBLANES, :] * x for k in range(SUBLANES)]))
            act = jnp.sum(jnp.concatenate(sums, axis=0), axis=-1, keepdims=True)
            gelu = 0.5 * act * (1.0 + lax.erf(act * (2.0 ** -0.5)))
            gw_col = jnp.sum(jnp.where(lane == t0 - lane0 + j, gw_tile, 0.0), axis=-1, keepdims=True)
            coef_ref[...] = jnp.broadcast_to(gw_col * gelu, (PEER_PICKS, LANES))
            accs = [jnp.zeros((SUBLANES, LANES), F32) for _ in range(4)]
            for k in range(PEER_PICKS):
                accs[k % 4] = accs[k % 4] + coef_ref[k:k + 1, :] * buf[slot, j * PEER_PICKS + k, SUBLANES:, :]
            y = x_ref[t0 + j] + ((accs[0] + accs[1]) + (accs[2] + accs[3]))
            ms = jnp.sum(jnp.sum(y * y, axis=0, keepdims=True), axis=-1, keepdims=True) * (1.0 / D_MODEL)
            ys.append(y * lax.rsqrt(ms + EPS) * fg_ref[...])
        y_ref[pl.ds(t0, GROUP), :] = jnp.stack(ys, axis=0).reshape(GROUP, D_MODEL)

    def pair_body(gg, carry):
        g = 2 * gg
        do_group(g, 0, islot * blk_words + (g + 1) * rows)
        last = gg + 1 == ngroups // 2

        @pl.when(jnp.logical_and(last, has_next))
        def _():
            idx_copy(i + 1, 1 - islot).wait()

        do_group(g + 1, 1, jnp.where(last, jnp.where(has_next, 1 - islot, islot) * blk_words,
                                     islot * blk_words + (g + 2) * rows))
        return carry

    lax.fori_loop(0, ngroups // 2, pair_body, 0)

    @pl.when(jnp.logical_not(has_next))
    def _():
        wait_group(0)


def _apply(idx, gwt, xn, x, fg, table):
    n = x.shape[0]
    nblk = n // PEER_BLOCK
    tile = pl.BlockSpec((PEER_BLOCK, SUBLANES, LANES), lambda i: (i, 0, 0))
    return pl.pallas_call(
        functools.partial(_apply_kernel, nblk=nblk),
        grid=(nblk,),
        in_specs=[pl.BlockSpec(memory_space=pl.ANY), pl.BlockSpec((PEER_PICKS, PEER_BLOCK), lambda i: (0, i)),
                  tile, tile,
                  pl.BlockSpec(fg.shape, lambda i: (0, 0)), pl.BlockSpec(memory_space=pl.ANY)],
        out_specs=pl.BlockSpec((PEER_BLOCK, D_MODEL), lambda i: (i, 0)),
        out_shape=jax.ShapeDtypeStruct((n, D_MODEL), F32),
        scratch_shapes=[pltpu.SMEM((2 * PEER_BLOCK * PEER_PICKS,), jnp.int32),
                        pltpu.VMEM((GATHER_SLOTS, GROUP * PEER_PICKS, EXPERT_ROWS, LANES), F32),
                        pltpu.SemaphoreType.DMA((GATHER_SLOTS,)),
                        pltpu.SemaphoreType.DMA((2,)),
                        pltpu.VMEM((PEER_PICKS, LANES), F32)],
        compiler_params=_params("arbitrary"),
        name="peer_apply",
    )(idx, gwt, xn, x, fg, table)


def _head_tile_steps(it, st_ref, pair_ids, gw_dst, idx_dst, lane_tiles):
    h = it // lane_tiles
    lanes = pl.ds(pl.multiple_of((it % lane_tiles) * LANES, LANES), LANES)
    r0 = pl.multiple_of(h * 2 * PEER_NKEYS, 2 * PEER_NKEYS)
    key_ids = lax.broadcasted_iota(jnp.int32, (PEER_NKEYS, LANES), 0)
    s = [st_ref[pl.ds(r0, PEER_NKEYS), lanes], st_ref[pl.ds(r0 + PEER_NKEYS, PEER_NKEYS), lanes]]
    vals, wins = ([], []), ([], [])

    def take_max(x, ids, big):
        m = jnp.max(x, axis=0, keepdims=True)
        w = jnp.min(jnp.where(x == m, ids, big), axis=0, keepdims=True)
        return m, w, jnp.where(ids == w, -jnp.inf, x)

    for _ in range(PEER_TOPK):
        for p in range(2):
            m, w, s[p] = take_max(s[p], key_ids, PEER_NKEYS)
            vals[p].append(m)
            wins[p].append(w)
        yield
    sv0, sv1 = jnp.concatenate(vals[0], axis=0), jnp.concatenate(vals[1], axis=0)
    si0, si1 = jnp.concatenate(wins[0], axis=0), jnp.concatenate(wins[1], axis=0)
    half = PEER_TOPK // 2
    cand = jnp.concatenate(
        [jnp.broadcast_to(sv0[0:1], (PEER_TOPK, LANES)) + sv1]
        + [jnp.broadcast_to(sv0[a:a + 1], (half, LANES)) + sv1[0:half] for a in range(1, half)]
        + [sv0[half:] + jnp.broadcast_to(sv1[0:1], (half, LANES))], axis=0)
    fvals, fwins = [], []
    for r in range(PEER_TOPK):
        m, w, cand = take_max(cand, pair_ids, PEER_TOPK * PEER_TOPK)
        fvals.append(m)
        fwins.append(w)
        if r % 2:
            yield
    fv, fi = jnp.concatenate(fvals, axis=0), jnp.concatenate(fwins, axis=0)
    eidx = _select(fi >> 4, si0) * PEER_NKEYS + _select(fi & (PEER_TOPK - 1), si1)
    e = jnp.exp(fv - fv[0:1])
    o0 = pl.multiple_of(h * PEER_TOPK, PEER_TOPK)
    gw_dst[pl.ds(o0, PEER_TOPK), lanes] = e / jnp.sum(e, axis=0, keepdims=True)
    idx_dst[pl.ds(o0, PEER_TOPK), lanes] = eidx
    yield


def _fused_kernel(x0_ref, x1_ref, xnext_ref, g_ref, wq_ref, keys_ref, pair_ref, fg_ref, tab_hbm, y_ref,
                  st_ref, xn_t, x_t, gw_v, idx_tv, idx_rows, idx_s, buf, gsem, isem, coef_ref, *, nblk):
    i = pl.program_id(0)
    ngroups = PEER_BLOCK // GROUP
    rows = GROUP * PEER_PICKS
    lane_tiles = PEER_BLOCK // LANES
    head_tiles = PEER_HEADS * lane_tiles
    assert ngroups // 2 == head_tiles and nblk >= 2
    islot = i % 2
    a3 = i % 3
    w3 = (i + 2) % 3
    has_next = i + 1 < nblk
    has_next2 = i + 2 < nblk
    pair_ids = pair_ref[...]

    def scores(x, slot):
        xn = _rms(x, g_ref[...])
        xn_t[slot] = xn.reshape(xn_t.shape[1:])
        x_t[slot] = x.reshape(x_t.shape[1:])
        q = jnp.dot(xn.astype(BF16), wq_ref[...], preferred_element_type=F32)
        st_ref[...] = lax.dot_general(keys_ref[...], q.astype(BF16), (((1,), (1,)), ((), ())),
                                      preferred_element_type=F32)

    def publish_copy(smem_slot):
        return pltpu.make_async_copy(idx_rows, idx_s.at[pl.ds(smem_slot * PEER_BLOCK, PEER_BLOCK), :],
                                     isem.at[smem_slot])

    def row_copy(e, slot, r):
        return pltpu.make_async_copy(tab_hbm.at[e], buf.at[slot, r], gsem.at[slot])

    def wait_group(slot):
        pltpu.make_async_copy(tab_hbm.at[pl.ds(0, rows)], buf.at[slot], gsem.at[slot]).wait()

    @pl.when(i == 0)
    def _():
        for blk, x_ref in ((0, x0_ref), (1, x1_ref)):
            scores(x_ref[...], blk)

            def body(it, carry):
                for _ in _head_tile_steps(it, st_ref, pair_ids, gw_v.at[blk], idx_tv, lane_tiles):
                    pass
                return carry

            lax.fori_loop(0, head_tiles, body, 0)
            idx_rows[...] = idx_tv[...].T
            publish_copy(blk).start()
            publish_copy(blk).wait()

        def issue(r, carry):
            row_copy(idx_s[r // PEER_PICKS, r % PEER_PICKS], 0, r).start()
            return carry

        lax.fori_loop(0, rows, issue, 0, unroll=8)

    @pl.when(has_next2)
    def _():
        scores(xnext_ref[...], w3)

    lane = lax.broadcasted_iota(jnp.int32, (PEER_PICKS, LANES), 1)
    sub = lax.broadcasted_iota(jnp.int32, (SUBLANES, LANES), 0)
    masks = {1: (sub & 1) == 0, 2: (sub & 2) == 0, 4: (sub & 4) == 0}

    def fold(a, b, h):
        return jnp.where(masks[h], a, b) + pltpu.roll(jnp.where(masks[h], b, a), h, axis=0)

    def row_sums(ps):
        c = [fold(ps[2 * m], ps[2 * m + 1], 1) for m in range(4)]
        d = [fold(c[0], c[1], 2), fold(c[2], c[3], 2)]
        return fold(d[0], d[1], 4)

    def do_group(g, slot, next_row0, between):
        nslot = 1 - slot

        def issue_steps(j):
            for k in range(PEER_PICKS):
                row_copy(idx_s[next_row0 + j, k], nslot, j * PEER_PICKS + k).start(priority=k % 2)
                if k % 4 == 3:
                    yield

        for j in range(ISSUE_AHEAD):
            for n, _ in enumerate(issue_steps(j)):
                if n % 16 == 15:
                    between()
        wait_group(slot)
        t0 = pl.multiple_of(g * GROUP, GROUP)
        lane0 = pl.multiple_of((t0 // LANES) * LANES, LANES)
        gw_tile = gw_v[a3, :, pl.ds(lane0, LANES)]
        ys = []
        for j in range(GROUP):
            issue = issue_steps(j + ISSUE_AHEAD) if j + ISSUE_AHEAD < GROUP else iter(())
            x = xn_t[a3, t0 + j]
            sums = []
            for c in range(PEER_PICKS // SUBLANES):
                next(issue, None)
                r0 = j * PEER_PICKS + c * SUBLANES
                sums.append(row_sums([buf[slot, r0 + k, 0:SUBLANES, :] * x for k in range(SUBLANES)]))
            act = jnp.sum(jnp.concatenate(sums, axis=0), axis=-1, keepdims=True)
            gelu = 0.5 * act * (1.0 + lax.erf(act * (2.0 ** -0.5)))
            gw_col = jnp.sum(jnp.where(lane == t0 - lane0 + j, gw_tile, 0.0), axis=-1, keepdims=True)
            coef_ref[...] = jnp.broadcast_to(gw_col * gelu, (PEER_PICKS, LANES))
            accs = [jnp.zeros((SUBLANES, LANES), F32) for _ in range(4)]
            for k in range(PEER_PICKS):
                if k % SUBLANES == 0:
                    next(issue, None)
                accs[k % 4] = accs[k % 4] + coef_ref[k:k + 1, :] * buf[slot, j * PEER_PICKS + k, SUBLANES:, :]
            for _ in issue:
                pass
            y = x_t[a3, t0 + j] + ((accs[0] + accs[1]) + (accs[2] + accs[3]))
            ms = jnp.sum(jnp.sum(y * y, axis=0, keepdims=True), axis=-1, keepdims=True) * (1.0 / D_MODEL)
            ys.append(y * lax.rsqrt(ms + EPS) * fg_ref[...])
            between()
        y_ref[pl.ds(t0, GROUP), :] = jnp.stack(ys, axis=0).reshape(GROUP, D_MODEL)

    def pair_body(gg, carry):
        g = 2 * gg
        last = gg + 1 == ngroups // 2

        @pl.when(jnp.logical_and(last, jnp.logical_and(has_next, i >= 1)))
        def _():
            publish_copy(1 - islot).wait()

        steps = _head_tile_steps(gg, st_ref, pair_ids, gw_v.at[w3], idx_tv, lane_tiles)

        def between():
            next(steps, None)
            next(steps, None)

        do_group(g, 0, islot * PEER_BLOCK + (g + 1) * GROUP, between)
        do_group(g + 1, 1, jnp.where(last, jnp.where(has_next, 1 - islot, islot) * PEER_BLOCK,
                                     islot * PEER_BLOCK + (g + 2) * GROUP), between)
        for _ in steps:
            pass
        return carry

    lax.fori_loop(0, ngroups // 2, pair_body, 0)

    @pl.when(has_next2)
    def _():
        idx_rows[...] = idx_tv[...].T
        publish_copy(islot).start()

    @pl.when(jnp.logical_not(has_next))
    def _():
        wait_group(0)


def _route_apply(x, g2, wq, keys_bd, fg, table):
    n = x.shape[0]
    nblk = n // PEER_BLOCK
    full = lambda a: pl.BlockSpec(a.shape, lambda i: (0,) * a.ndim)
    pair_ids = jnp.asarray(np.broadcast_to(_PAIR_IDS[:, None], (_PAIR_IDS.size, LANES)))
    blk = lambda f: pl.BlockSpec((PEER_BLOCK, D_MODEL), f)
    return pl.pallas_call(
        functools.partial(_fused_kernel, nblk=nblk),
        grid=(nblk,),
        in_specs=[blk(lambda i: (0, 0)), blk(lambda i: (1, 0)), blk(lambda i: (jnp.minimum(i + 2, nblk - 1), 0)),
                  full(g2), full(wq), full(keys_bd), full(pair_ids), full(fg), pl.BlockSpec(memory_space=pl.ANY)],
        out_specs=blk(lambda i: (i, 0)),
        out_shape=jax.ShapeDtypeStruct((n, D_MODEL), F32),
        scratch_shapes=[pltpu.VMEM((PEER_HEADS * 2 * PEER_NKEYS, PEER_BLOCK), F32),
                        pltpu.VMEM((3, PEER_BLOCK, SUBLANES, LANES), F32),
                        pltpu.VMEM((3, PEER_BLOCK, SUBLANES, LANES), F32),
                        pltpu.VMEM((3, PEER_PICKS, PEER_BLOCK), F32),
                        pltpu.VMEM((PEER_PICKS, PEER_BLOCK), jnp.int32),
                        pltpu.VMEM((PEER_BLOCK, PEER_PICKS), jnp.int32),
                        pltpu.SMEM((2 * PEER_BLOCK, PEER_PICKS), jnp.int32),
                        pltpu.VMEM((GATHER_SLOTS, GROUP * PEER_PICKS, EXPERT_ROWS, LANES), F32),
                        pltpu.SemaphoreType.DMA((GATHER_SLOTS,)),
                        pltpu.SemaphoreType.DMA((2,)),
                        pltpu.VMEM((PEER_PICKS, LANES), F32)],
        compiler_params=_params("arbitrary"),
        name="peer_route_apply",
    )(x, x, x, g2, wq, keys_bd, pair_ids, fg, table)


def _layer(x3, weights, att_kv, att_geom, ml_state, L):
    B, S, _ = x3.shape
    n = B * S
    x = x3.reshape(n, D_MODEL)
    (q, k, v, kf, vf, mq, mk, mv, mo, ga, gb, gates) = _inproj(
        x, weights["g1"], weights["w_main"], weights["b_main"], weights["wg_hi"], weights["wg_lo"],
        weights["bg"], weights["fb"])
    sh = lambda a: a.reshape(B, S, a.shape[-1])
    k_all, v_all = att_kv(sh(k), sh(v))
    att = _attention(sh(q), k_all, v_all, att_geom["bias"], L=att_geom["L"], KB=att_geom["KB"], pad=att_geom["pad"])
    h, c_new, n_new, m_new = _mlstm(sh(mq), sh(mk), sh(mv), sh(gates), *ml_state, L=L)
    x2 = _merge(x, att.reshape(n, ATT_W), h.reshape(n, ML_W), mo, ga, gb,
                weights["wa"], weights["wm"], weights["wo"])
    fg = weights["fg"].reshape(SUBLANES, LANES)
    if n >= 2 * PEER_BLOCK:
        y = _route_apply(x2, weights["g2"], weights["wq"], weights["keys_bd"], fg, weights["table"])
    else:
        xn2, x2_tiles, idx_t, gw_t = _route(x2, weights["g2"], weights["wq"], weights["keys_bd"])
        y = _apply(idx_t.T.reshape(n * PEER_PICKS), gw_t, xn2, x2_tiles, fg, weights["table"])
    return (y.reshape(B, S, D_MODEL), sh(kf).reshape(B, S, ATT_HEADS, ATT_HD), sh(vf).reshape(B, S, ATT_HEADS, ATT_HD),
            c_new, n_new, m_new[:, :, 0])


def kernel(x_prompt, x_sample, cache_att_k, cache_att_v, state_mlstm_C, state_mlstm_n, state_mlstm_m, norm1_g, w_in, b_in, ml_f_bias, att_rel_bias, w_att_branch, w_ml_branch, w_out, norm2_g, peer_wq, peer_sub_keys, peer_u, peer_v, final_g):
    depth = w_in.shape[0]
    assert depth == 1, "single-layer step"
    l = 0
    w = w_in[l]
    b = b_in[l]
    gate_lo, gate_hi = 7 * 512, 7 * 512 + 2 * ML_HEADS
    w_main = jnp.concatenate([w[:, :gate_lo], w[:, gate_hi:]], axis=1).astype(BF16)
    b_main = jnp.concatenate([b[:gate_lo], b[gate_hi:]])[None, :]
    wg = jnp.pad(w[:, gate_lo:gate_hi], ((0, 0), (0, GATE_COLS - 2 * ML_HEADS)))
    wg_hi = wg.astype(BF16)
    wg_lo = (wg - wg_hi.astype(F32)).astype(BF16)
    bg = jnp.pad(b[gate_lo:gate_hi], (0, GATE_COLS - 2 * ML_HEADS))[None, :]
    fb = jnp.pad(ml_f_bias[l], (ML_HEADS, GATE_COLS - 2 * ML_HEADS))[None, :]
    sk = peer_sub_keys[l].reshape(PEER_HEADS * 2, PEER_NKEYS, PEER_DKEY // 2)
    eye = jnp.eye(PEER_HEADS * 2, dtype=F32)
    keys_bd = (sk[:, :, None, :] * eye[:, None, :, None]).reshape(PEER_HEADS * 2 * PEER_NKEYS, D_MODEL).astype(BF16)
    weights = dict(
        g1=norm1_g[l][None, :], w_main=w_main, b_main=b_main, wg_hi=wg_hi, wg_lo=wg_lo, bg=bg, fb=fb,
        wa=w_att_branch[l].astype(BF16), wm=w_ml_branch[l].astype(BF16), wo=w_out[l].astype(BF16),
        g2=norm2_g[l][None, :], wq=peer_wq[l].astype(BF16), keys_bd=keys_bd, fg=final_g[None, :],
        table=jnp.concatenate([peer_u[l].reshape(-1, SUBLANES, LANES), peer_v[l].reshape(-1, SUBLANES, LANES)],
                              axis=1))

    Bs, T, _ = x_sample.shape
    P = cache_att_k.shape[2]
    assert P == ATT_CTX
    ck = cache_att_k[l].reshape(Bs, P, ATT_W).astype(BF16)
    cv = cache_att_v[l].reshape(Bs, P, ATT_W).astype(BF16)
    cat_kv = lambda k, v: (jnp.concatenate([ck, k], axis=1), jnp.concatenate([cv, v], axis=1))
    geom_s = dict(bias=_rel_bias(att_rel_bias[l], T, P + T), L=T, KB=P + T, pad=0)
    state_s = (state_mlstm_C[l], state_mlstm_n[l],
               jnp.broadcast_to(state_mlstm_m[l][:, :, None], (Bs, ML_HEADS, ML_DK)))
    y_s, k_s, v_s, c_s, n_s, m_s = _layer(x_sample, weights, cat_kv, geom_s, state_s, T)

    B, S, _ = x_prompt.shape
    pad_kv = lambda k, v: (jnp.pad(k, ((0, 0), (ATT_CTX, 0), (0, 0))), jnp.pad(v, ((0, 0), (ATT_CTX, 0), (0, 0))))
    geom_p = dict(bias=_pair_bias(_rel_bias(att_rel_bias[l], CHUNK, ATT_CTX + CHUNK), ATT_CHUNKS_PER_STEP),
                  L=ATT_CHUNKS_PER_STEP * CHUNK, KB=ATT_CTX + ATT_CHUNKS_PER_STEP * CHUNK, pad=ATT_CTX)
    state_p = (jnp.zeros((B, ML_HEADS, ML_DK, ML_DK), F32), jnp.zeros((B, ML_HEADS, ML_DK), F32),
               jnp.full((B, ML_HEADS, ML_DK), NEG, F32))
    y_p, k_p, v_p, c_p, n_p, m_p = _layer(x_prompt, weights, pad_kv, geom_p, state_p, CHUNK)
    keep = min(ATT_CTX, S)

    st = lambda a: a[None]
    return (y_p, y_s, st(k_p[:, -keep:]), st(v_p[:, -keep:]), st(c_p), st(n_p), st(m_p),
            st(k_s), st(v_s), st(c_s), st(n_s), st(m_s))
```
